```python
import math
import jax, jax.numpy as jnp
from jax import lax
import numpy as np

D_MODEL = 1024
BATCH = 8
SEQ = 4096
DEPTH = 1

HY_WIDTH = 512
HY_GROUPS = 8
HY_ORDER = 2
HY_EMB = 33
HY_BANDS = (HY_EMB - 1) // 2
HY_FILT_HIDDEN = 64
HY_FAST_DECAY = 0.3
HY_SLOW_DECAY = 1.5
HY_DECAY_TARGET = 1e-2
HY_MAX_DECAY = math.log(HY_DECAY_TARGET) / HY_FAST_DECAY
HY_MIN_DECAY = math.log(HY_DECAY_TARGET) / HY_SLOW_DECAY
HY_COLS = 3 * HY_WIDTH

RW_WIDTH = 512
RW_HEAD = 64
RW_HEADS = RW_WIDTH // RW_HEAD
RW_LORA_W = 64
RW_LORA_A = 64
RW_LORA_G = 128
RW_GN_EPS = 64e-5
RW_SIZES = [RW_WIDTH, RW_WIDTH, RW_WIDTH, RW_LORA_W, RW_LORA_W, RW_LORA_A, RW_LORA_A, RW_LORA_G]
RW_OFFSETS = [sum(RW_SIZES[:i + 1]) for i in range(len(RW_SIZES) - 1)]
RW_COLS = sum(RW_SIZES)

GATE_COLS = 2 * D_MODEL
IN_COLS = HY_COLS + RW_COLS + GATE_COLS

FFN_HIDDEN = ((8 * D_MODEL + 3 * 256 - 1) // (3 * 256)) * 256

DN_ALPHA = (2.0 * DEPTH) ** 0.25
DN_BETA = (8.0 * DEPTH) ** -0.25
LN_EPS = 1e-5

kernel_name = "hyena_rwkv7_gated_hybrid_encoder_layer"


def layer_norm(x, w, b):
    xf = x.astype(jnp.float32)
    mu = jnp.mean(xf, axis=-1, keepdims=True)
    var = jnp.mean(jnp.square(xf - mu), axis=-1, keepdims=True)
    return ((xf - mu) * lax.rsqrt(var + LN_EPS) * w + b).astype(x.dtype)


def shift_prev(z):
    return jnp.pad(z[:, :-1], ((0, 0), (1, 0), (0, 0)))


def shift_next(z):
    return jnp.pad(z[:, 1:], ((0, 0), (0, 1), (0, 0)))


def short_conv3(z, w, b):
    return w[0] * shift_prev(z) + w[1] * z + w[2] * shift_next(z) + b


def hyena_filters(L, fw1, fb1, fw2, fb2, fw3, fb3, fw4, sin_freq):
    f32 = jnp.float32
    t = jnp.linspace(0.0, 1.0, L, dtype=f32)[:, None]
    w = 2.0 * math.pi * jnp.arange(L, dtype=f32) / L
    f = jnp.linspace(1e-4, HY_BANDS - 1, HY_BANDS, dtype=f32)
    ang = w[:, None] * f[None, :]
    feats = jnp.concatenate([t, jnp.cos(ang), -jnp.sin(ang)], axis=-1)
    sf = sin_freq.astype(f32)
    h = jnp.sin(sf[0] * (feats @ fw1.astype(f32) + fb1.astype(f32)))
    h = jnp.sin(sf[1] * (h @ fw2.astype(f32) + fb2.astype(f32)))
    h = jnp.sin(sf[2] * (h @ fw3.astype(f32) + fb3.astype(f32)))
    h = (h @ fw4.astype(f32)).reshape(L, HY_ORDER, 2, HY_WIDTH)
    deltas = jnp.abs(jnp.linspace(HY_MIN_DECAY, HY_MAX_DECAY, HY_WIDTH, dtype=f32))
    window = jnp.exp(-t * deltas[None, :])
    return h * window[:, None, None, :]


def long_conv(z, h_fwd, h_bwd, d_skip):
    L = z.shape[1]
    n = 2 * L
    zf = jnp.fft.rfft(z, n=n, axis=1)
    hf = jnp.fft.rfft(h_fwd, n=n, axis=0) + jnp.conj(jnp.fft.rfft(h_bwd, n=n, axis=0))
    y = jnp.fft.irfft(zf * hf[None], n=n, axis=1)[:, :L]
    return y + z * d_skip


def hyena_mixer(u, conv_w, conv_b, fw1, fb1, fw2, fb2, fw3, fb3, fw4, sin_freq, skip):
    L = u.shape[1]
    u = short_conv3(u, conv_w, conv_b).astype(jnp.float32)
    v, x1, x2 = jnp.split(u, 3, axis=-1)
    h = hyena_filters(L, fw1, fb1, fw2, fb2, fw3, fb3, fw4, sin_freq)
    sk = skip.astype(jnp.float32)
    z = x1 * long_conv(v, h[:, 0, 0], h[:, 0, 1], sk[0])
    z = x2 * long_conv(z, h[:, 1, 0], h[:, 1, 1], sk[1])
    return z


def rwkv_step(S, inp):
    r, w, k, v, kk, a = inp
    sa = jnp.einsum('dbhvk,dbhk->dbhv', S, -kk)
    S = S * w[..., None, :] + sa[..., :, None] * (kk * a)[..., None, :] + v[..., :, None] * k[..., None, :]
    y = jnp.einsum('dbhvk,dbhk->dbhv', S, r)
    return S, y


def rwkv_mixer(u, mu, w0, w2, a0, a2, g2, k_k, k_a, r_k, gn_w, gn_b):
    f32 = jnp.float32
    B, L, _ = u.shape
    u = u + mu * (0.5 * (shift_prev(u) + shift_next(u)) - u)
    r, k, v, wd_f, wd_b, ad_f, ad_b, gd = jnp.split(u.astype(f32), RW_OFFSETS, axis=-1)
    d = w0.astype(f32)[:, None, None, :] + jnp.einsum('dblr,drc->dblc', jnp.tanh(jnp.stack([wd_f, wd_b])), w2.astype(f32))
    decay = jnp.exp(-math.exp(-0.5) * jax.nn.sigmoid(d))
    a = jax.nn.sigmoid(a0.astype(f32)[:, None, None, :] + jnp.einsum('dblr,drc->dblc', jnp.stack([ad_f, ad_b]), a2.astype(f32)))
    g = jax.nn.sigmoid(gd) @ g2.astype(f32)

    def heads(z):
        return z.reshape(z.shape[:-1] + (RW_HEADS, RW_HEAD))

    kk = heads(k * k_k.astype(f32))
    kk = kk / jnp.maximum(jnp.sqrt(jnp.sum(kk * kk, axis=-1, keepdims=True)), 1e-12)
    kd = heads(k[None] * (1.0 + (a - 1.0) * k_a.astype(f32)))
    r_h, v_h, w_h, a_h = heads(r), heads(v), heads(decay), heads(a)

    def dirs(z_f, z_b):
        return jnp.moveaxis(jnp.stack([z_f, z_b[:, ::-1]]), 2, 0)

    xs = (dirs(r_h, r_h), dirs(w_h[0], w_h[1]), dirs(kd[0], kd[1]),
          dirs(v_h, v_h), dirs(kk, kk), dirs(a_h[0], a_h[1]))
    S0 = jnp.zeros((2, B, RW_HEADS, RW_HEAD, RW_HEAD), f32)
    _, ys = lax.scan(rwkv_step, S0, xs)
    ys = jnp.moveaxis(ys, 0, 2)
    y = ys[0] + ys[1][:, ::-1]
    mu_y = jnp.mean(y, axis=-1, keepdims=True)
    var_y = jnp.mean(jnp.square(y - mu_y), axis=-1, keepdims=True)
    y = (y - mu_y) * lax.rsqrt(var_y + RW_GN_EPS) * heads(gn_w.astype(f32)) + heads(gn_b.astype(f32))
    bonus = jnp.sum(r_h[None] * kd * r_k.astype(f32), axis=-1, keepdims=True) * v_h[None]
    y = y + jnp.sum(bonus, axis=0)
    return y.reshape(B, L, RW_WIDTH) * g


def hybrid_layer(x, w_in, hy_conv_w, hy_conv_b, hy_filt_w1, hy_filt_b1, hy_filt_w2, hy_filt_b2,
                 hy_filt_w3, hy_filt_b3, hy_filt_w4, hy_sin_freq, hy_skip, rw_mu, rw_w0, rw_w2,
                 rw_a0, rw_a2, rw_g2, rw_k_k, rw_k_a, rw_r_k, rw_gn_w, rw_gn_b, w_hy_out, w_rw_out,
                 w_o, ln1_w, ln1_b, ffn_w_gate, ffn_w_up, ffn_w_down, ln2_w, ln2_b):
    dt = x.dtype
    proj = x @ w_in
    u_h, u_r, gates = jnp.split(proj, [HY_COLS, HY_COLS + RW_COLS], axis=-1)
    y_h = hyena_mixer(u_h, hy_conv_w, hy_conv_b, hy_filt_w1, hy_filt_b1, hy_filt_w2, hy_filt_b2,
                      hy_filt_w3, hy_filt_b3, hy_filt_w4, hy_sin_freq, hy_skip).astype(dt) @ w_hy_out
    y_r = rwkv_mixer(u_r, rw_mu, rw_w0, rw_w2, rw_a0, rw_a2, rw_g2, rw_k_k, rw_k_a, rw_r_k,
                     rw_gn_w, rw_gn_b).astype(dt) @ w_rw_out
    g_h, g_r = jnp.split(jax.nn.sigmoid(gates), 2, axis=-1)
    mix = (g_h * y_h + g_r * y_r) @ w_o
    x = layer_norm(DN_ALPHA * x + mix, ln1_w, ln1_b)
    ffn = (jax.nn.silu(x @ ffn_w_gate) * (x @ ffn_w_up)) @ ffn_w_down
    return layer_norm(DN_ALPHA * x + ffn, ln2_w, ln2_b)


def setup_inputs(seed: int = 0) -> dict:
    key = jax.random.key(seed)
    keys = iter(jax.random.split(key, 48))

    def nrm(shape, scale):
        return jax.random.normal(next(keys), shape, jnp.float32) * scale

    Ld = DEPTH
    return {
        "x": nrm((BATCH, SEQ, D_MODEL), 1.0),
        "w_in": nrm((Ld, D_MODEL, IN_COLS), D_MODEL ** -0.5),
        "hy_conv_w": nrm((Ld, 3, HY_COLS), 3 ** -0.5),
        "hy_conv_b": nrm((Ld, HY_COLS), 0.02),
        "hy_filt_w1": nrm((Ld, HY_EMB, HY_FILT_HIDDEN), HY_EMB ** -0.5),
        "hy_filt_b1": nrm((Ld, HY_FILT_HIDDEN), 0.1),
        "hy_filt_w2": nrm((Ld, HY_FILT_HIDDEN, HY_FILT_HIDDEN), HY_FILT_HIDDEN ** -0.5),
        "hy_filt_b2": nrm((Ld, HY_FILT_HIDDEN), 0.1),
        "hy_filt_w3": nrm((Ld, HY_FILT_HIDDEN, HY_FILT_HIDDEN), HY_FILT_HIDDEN ** -0.5),
        "hy_filt_b3": nrm((Ld, HY_FILT_HIDDEN), 0.1),
        "hy_filt_w4": nrm((Ld, HY_FILT_HIDDEN, HY_ORDER * 2 * HY_WIDTH), 0.1 * HY_FILT_HIDDEN ** -0.5),
        "hy_sin_freq": 1.0 + nrm((Ld, 3, HY_FILT_HIDDEN), 0.05),
        "hy_skip": nrm((Ld, HY_ORDER, HY_WIDTH), 0.5),
        "rw_mu": 0.5 + nrm((Ld, RW_COLS), 0.1),
        "rw_w0": -0.5 + nrm((Ld, 2, RW_WIDTH), 0.5),
        "rw_w2": nrm((Ld, 2, RW_LORA_W, RW_WIDTH), 0.5 * RW_LORA_W ** -0.5),
        "rw_a0": nrm((Ld, 2, RW_WIDTH), 0.1),
        "rw_a2": nrm((Ld, 2, RW_LORA_A, RW_WIDTH), 0.5 * RW_LORA_A ** -0.5),
        "rw_g2": nrm((Ld, RW_LORA_G, RW_WIDTH), RW_LORA_G ** -0.5),
        "rw_k_k": 0.85 + nrm((Ld, RW_WIDTH), 0.05),
        "rw_k_a": 1.0 + nrm((Ld, RW_WIDTH), 0.05),
        "rw_r_k": nrm((Ld, RW_HEADS, RW_HEAD), 0.1),
        "rw_gn_w": 1.0 + nrm((Ld, RW_WIDTH), 0.05),
        "rw_gn_b": nrm((Ld, RW_WIDTH), 0.02),
        "w_hy_out": nrm((Ld, HY_WIDTH, D_MODEL), HY_WIDTH ** -0.5),
        "w_rw_out": nrm((Ld, RW_WIDTH, D_MODEL), RW_WIDTH ** -0.5),
        "w_o": nrm((Ld, D_MODEL, D_MODEL), DN_BETA * D_MODEL ** -0.5),
        "ln1_w": 1.0 + nrm((Ld, D_MODEL), 0.05),
        "ln1_b": nrm((Ld, D_MODEL), 0.02),
        "ffn_w_gate": nrm((Ld, D_MODEL, FFN_HIDDEN), D_MODEL ** -0.5),
        "ffn_w_up": nrm((Ld, D_MODEL, FFN_HIDDEN), D_MODEL ** -0.5),
        "ffn_w_down": nrm((Ld, FFN_HIDDEN, D_MODEL), DN_BETA * FFN_HIDDEN ** -0.5),
        "ln2_w": 1.0 + nrm((Ld, D_MODEL), 0.05),
        "ln2_b": nrm((Ld, D_MODEL), 0.02),
    }


def reference(x, w_in, hy_conv_w, hy_conv_b, hy_filt_w1, hy_filt_b1, hy_filt_w2, hy_filt_b2,
              hy_filt_w3, hy_filt_b3, hy_filt_w4, hy_sin_freq, hy_skip, rw_mu, rw_w0, rw_w2,
              rw_a0, rw_a2, rw_g2, rw_k_k, rw_k_a, rw_r_k, rw_gn_w, rw_gn_b, w_hy_out, w_rw_out,
              w_o, ln1_w, ln1_b, ffn_w_gate, ffn_w_up, ffn_w_down, ln2_w, ln2_b):
    for l in range(DEPTH):
        x = hybrid_layer(x, w_in[l], hy_conv_w[l], hy_conv_b[l], hy_filt_w1[l], hy_filt_b1[l],
                         hy_filt_w2[l], hy_filt_b2[l], hy_filt_w3[l], hy_filt_b3[l], hy_filt_w4[l],
                         hy_sin_freq[l], hy_skip[l], rw_mu[l], rw_w0[l], rw_w2[l], rw_a0[l],
                         rw_a2[l], rw_g2[l], rw_k_k[l], rw_k_a[l], rw_r_k[l], rw_gn_w[l], rw_gn_b[l],
                         w_hy_out[l], w_rw_out[l], w_o[l], ln1_w[l], ln1_b[l], ffn_w_gate[l],
                         ffn_w_up[l], ffn_w_down[l], ln2_w[l], ln2_b[l])
    return x
```

```python
import functools
import math

import jax
import jax.numpy as jnp
from jax import lax
from jax.experimental import pallas as pl
from jax.experimental.pallas import tpu as pltpu

F32 = jnp.float32
BF16 = jnp.bfloat16
HIGHEST = lax.Precision.HIGHEST

D_MODEL = 1024
HY_WIDTH = 512
HY_ORDER = 2
HY_BANDS = 16
HY_FILT_HIDDEN = 64
HY_FAST_DECAY = 0.3
HY_SLOW_DECAY = 1.5
HY_DECAY_TARGET = 1e-2
HY_MAX_DECAY = math.log(HY_DECAY_TARGET) / HY_FAST_DECAY
HY_MIN_DECAY = math.log(HY_DECAY_TARGET) / HY_SLOW_DECAY
HY_COLS = 3 * HY_WIDTH
RW_WIDTH = 512
RW_HEAD = 64
RW_HEADS = RW_WIDTH // RW_HEAD
RW_LORA = 64
RW_LORA_G = 128
RW_GN_EPS = 64e-5
RW_COLS = 3 * RW_WIDTH + 4 * RW_LORA + RW_LORA_G
GATE_COLS = 2 * D_MODEL
FFN_HIDDEN = ((8 * D_MODEL + 3 * 256 - 1) // (3 * 256)) * 256
DEPTH = 1
DN_ALPHA = (2.0 * DEPTH) ** 0.25
LN_EPS = 1e-5
RW_DECAY_SCALE = math.exp(-0.5)

LANES = 128
VMEM_LIMIT = 56 * 1024 * 1024

FFT_N1 = 32
FFT_F1 = FFT_N1 // 2 + 1
RW_CHUNK = 64


def _cparams(sem, vmem=VMEM_LIMIT):
    return pltpu.CompilerParams(dimension_semantics=sem, vmem_limit_bytes=vmem)


def _dot(a, b, precision=None):
    return jnp.dot(a, b, preferred_element_type=F32, precision=precision)


def _dot_nt(a, b):
    return lax.dot_general(a, b, (((1,), (1,)), ((), ())), preferred_element_type=F32)


def _dot_tn(a, b):
    return lax.dot_general(a, b, (((0,), (0,)), ((), ())), preferred_element_type=F32)


def _layer_norm(h, w, b):
    mu = jnp.mean(h, axis=-1, keepdims=True)
    c = h - mu
    var = jnp.mean(c * c, axis=-1, keepdims=True)
    return c * lax.rsqrt(var + LN_EPS) * w + b


def _segsum(x, p):
    hi = x.astype(BF16)
    lo = (x - hi.astype(F32)).astype(BF16)
    return _dot(hi, p) + _dot(lo, p)


def _in_proj_kernel(x_ref, w_ref, uh_ref, ur_ref, g_ref):
    xb = x_ref[...].astype(BF16)
    uh_ref[...] = _dot(xb, w_ref[:, :HY_COLS])
    ur_ref[...] = _dot(xb, w_ref[:, HY_COLS:HY_COLS + RW_COLS])
    g_ref[...] = _dot(xb, w_ref[:, HY_COLS + RW_COLS:])


def _in_proj(x2, w_in_bf, tm=256):
    m, d = x2.shape
    n = w_in_bf.shape[1]
    return pl.pallas_call(
        _in_proj_kernel,
        grid=(m // tm,),
        in_specs=[pl.BlockSpec((tm, d), lambda i: (i, 0)),
                  pl.BlockSpec((d, n), lambda i: (0, 0))],
        out_specs=[pl.BlockSpec((tm, HY_COLS), lambda i: (i, 0)),
                   pl.BlockSpec((tm, RW_COLS), lambda i: (i, 0)),
                   pl.BlockSpec((tm, GATE_COLS), lambda i: (i, 0))],
        out_shape=[jax.ShapeDtypeStruct((m, HY_COLS), F32),
                   jax.ShapeDtypeStruct((m, RW_COLS), F32),
                   jax.ShapeDtypeStruct((m, GATE_COLS), F32)],
        compiler_params=_cparams(("parallel",)),
        name="in_proj",
    )(x2, w_in_bf)


def _shortconv_kernel(u_ref, w_ref, b_ref, o_ref):
    u = u_ref[0]
    L = u.shape[0]
    row = lax.broadcasted_iota(jnp.int32, (L, 1), 0)
    prev = jnp.where(row == 0, 0.0, pltpu.roll(u, 1, 0))
    nxt = jnp.where(row == L - 1, 0.0, pltpu.roll(u, L - 1, 0))
    o_ref[0] = w_ref[0:1, :] * prev + w_ref[1:2, :] * u + w_ref[2:3, :] * nxt + b_ref[...]


def _shortconv(u_h, conv_w, conv_b):
    B, L, C = u_h.shape
    return pl.pallas_call(
        _shortconv_kernel,
        grid=(B, C // LANES),
        in_specs=[pl.BlockSpec((1, L, LANES), lambda b, j: (b, 0, j)),
                  pl.BlockSpec((3, LANES), lambda b, j: (0, j)),
                  pl.BlockSpec((1, LANES), lambda b, j: (0, j))],
        out_specs=pl.BlockSpec((1, L, LANES), lambda b, j: (b, 0, j)),
        out_shape=jax.ShapeDtypeStruct((B, L, C), F32),
        compiler_params=_cparams(("parallel", "parallel")),
        name="hy_shortconv",
    )(u_h, conv_w, conv_b.reshape(1, C))


def _filter_kernel(L, tb, w1t_ref, w1c_ref, w1s_ref, b1_ref, w2_ref, b2_ref, w3_ref, b3_ref,
                   w4_ref, sf_ref, freq_ref, delta_ref, kc_ref):
    n = 2 * L
    i = pl.program_id(0)
    row = i * tb + lax.broadcasted_iota(jnp.int32, (tb, 1), 0)
    second = row >= L
    pos = jnp.where(second, n - row, row).astype(F32)
    t = pos / float(L - 1)
    ang = (2.0 * math.pi * pos / float(L)) * freq_ref[...]
    pre = t * w1t_ref[...] + _dot(jnp.cos(ang), w1c_ref[...], HIGHEST) \
        - _dot(jnp.sin(ang), w1s_ref[...], HIGHEST) + b1_ref[...]
    h = jnp.sin(sf_ref[0:1, :] * pre)
    h = jnp.sin(sf_ref[1:2, :] * (_dot(h, w2_ref[...], HIGHEST) + b2_ref[...]))
    h = jnp.sin(sf_ref[2:3, :] * (_dot(h, w3_ref[...], HIGHEST) + b3_ref[...]))
    h4 = _dot(h, w4_ref[...], HIGHEST)
    window = jnp.exp(-t * delta_ref[...])
    keep = jnp.where(row == L, 0.0, 1.0)
    first = jnp.where(row == 0, 1.0, 0.0)
    for o in range(HY_ORDER):
        base = o * 2 * HY_WIDTH
        fwd = h4[:, base:base + HY_WIDTH]
        bwd = h4[:, base + HY_WIDTH:base + 2 * HY_WIDTH]
        kc_ref[o] = window * (jnp.where(second, bwd, fwd) * keep + first * bwd)


def _filters(L, fw1, fb1, fw2, fb2, fw3, fb3, fw4, sin_freq, tb=512):
    n = 2 * L
    tb = min(tb, n)
    freqs = jnp.linspace(1e-4, HY_BANDS - 1, HY_BANDS, dtype=F32).reshape(1, HY_BANDS)
    deltas = jnp.abs(jnp.linspace(HY_MIN_DECAY, HY_MAX_DECAY, HY_WIDTH, dtype=F32)).reshape(1, HY_WIDTH)
    hid = HY_FILT_HIDDEN
    full = lambda shape: pl.BlockSpec(shape, lambda i: tuple(0 for _ in shape))
    return pl.pallas_call(
        functools.partial(_filter_kernel, L, tb),
        grid=(n // tb,),
        in_specs=[full((1, hid)), full((HY_BANDS, hid)), full((HY_BANDS, hid)), full((1, hid)),
                  full((hid, hid)), full((1, hid)), full((hid, hid)), full((1, hid)),
                  full((hid, HY_ORDER * 2 * HY_WIDTH)), full((3, hid)),
                  full((1, HY_BANDS)), full((1, HY_WIDTH))],
        out_specs=pl.BlockSpec((HY_ORDER, tb, HY_WIDTH), lambda i: (0, i, 0)),
        out_shape=jax.ShapeDtypeStruct((HY_ORDER, n, HY_WIDTH), F32),
        compiler_params=_cparams(("parallel",)),
        name="hy_filters",
    )(fw1[0:1], fw1[1:1 + HY_BANDS], fw1[1 + HY_BANDS:], fb1.reshape(1, hid), fw2, fb2.reshape(1, hid),
      fw3, fb3.reshape(1, hid), fw4, sin_freq, freqs, deltas)


def _fft_tables(n2):
    n = FFT_N1 * n2
    f1 = jnp.arange(FFT_F1, dtype=jnp.int32)[:, None, None]
    a = jnp.arange(n2, dtype=jnp.int32)[None, :, None]
    b = jnp.arange(n2, dtype=jnp.int32)[None, None, :]
    ph_f = ((b * (f1 + FFT_N1 * a)) % n).astype(F32) * (2.0 * math.pi / n)
    cr, ci = jnp.cos(ph_f), -jnp.sin(ph_f)
    m_fwd = jnp.concatenate([jnp.concatenate([cr, -ci], axis=2),
                             jnp.concatenate([ci, cr], axis=2)], axis=1).astype(BF16)
    ph_i = ((a * (f1 + FFT_N1 * b)) % n).astype(F32) * (2.0 * math.pi / n)
    herm = jnp.where((f1 == 0) | (f1 == FFT_N1 // 2), 1.0, 2.0) / n
    dr, di = herm * jnp.cos(ph_i), herm * jnp.sin(ph_i)
    m_inv = jnp.concatenate([jnp.concatenate([dr, -di], axis=2),
                             jnp.concatenate([di, dr], axis=2)], axis=1).astype(BF16)
    k = (jnp.arange(FFT_F1, dtype=jnp.int32)[:, None] * jnp.arange(FFT_N1, dtype=jnp.int32)[None, :]) % FFT_N1
    ph = k.astype(F32) * (2.0 * math.pi / FFT_N1)
    coef = jnp.stack([jnp.cos(ph), -jnp.sin(ph)], axis=-1).reshape(FFT_F1, 2 * FFT_N1)
    gcoef = jnp.stack([jnp.cos(ph), jnp.sin(ph)], axis=-1).reshape(FFT_F1, 2 * FFT_N1)
    return m_fwd, m_inv, coef, gcoef


def _row_block(n2):
    return 32 if n2 % 32 == 0 else n2


def _fft_stage1(z_ref, coef_ref, f1, s1_count, n2, a_scr):
    rb = _row_block(n2)
    for r in range(n2 // rb):
        acc_r = None
        acc_i = None
        for s1 in range(s1_count):
            zt = z_ref[0, pl.ds(s1 * n2 + r * rb, rb), :]
            cr = coef_ref[f1, 2 * s1]
            ci = coef_ref[f1, 2 * s1 + 1]
            acc_r = cr * zt if acc_r is None else acc_r + cr * zt
            acc_i = ci * zt if acc_i is None else acc_i + ci * zt
        a_scr[pl.ds(r * rb, rb), :] = acc_r.astype(BF16)
        a_scr[pl.ds(n2 + r * rb, rb), :] = acc_i.astype(BF16)


def _spectrum_kernel(n2, coef_ref, kc_ref, mf_ref, hr_ref, hi_ref, a_scr):
    f1 = pl.program_id(2)
    _fft_stage1(kc_ref, coef_ref, f1, FFT_N1, n2, a_scr)
    x = _dot(mf_ref[0], a_scr[...])
    hr_ref[0] = x[:n2]
    hi_ref[0] = x[n2:]


def _spectrum(kc, m_fwd, coef):
    order, n, C = kc.shape
    n2 = n // FFT_N1
    spec = lambda: pl.BlockSpec((1, n2, LANES), lambda o, j, f: (o, f, j))
    return pl.pallas_call(
        functools.partial(_spectrum_kernel, n2),
        grid=(order, C // LANES, FFT_F1),
        in_specs=[pl.BlockSpec(memory_space=pltpu.SMEM),
                  pl.BlockSpec((1, n, LANES), lambda o, j, f: (o, 0, j)),
                  pl.BlockSpec((1, 2 * n2, 2 * n2), lambda o, j, f: (f, 0, 0))],
        out_specs=[spec(), spec()],
        out_shape=[jax.ShapeDtypeStruct((order, FFT_F1 * n2, C), F32)] * 2,
        scratch_shapes=[pltpu.VMEM((2 * n2, LANES), BF16)],
        compiler_params=_cparams(("parallel", "parallel", "arbitrary")),
        name="hy_spectrum",
    )(coef, kc, m_fwd)


def _longconv_kernel(n2, coef_ref, gcoef_ref, z_ref, gate_ref, skip_ref, hr_ref, hi_ref, mf_ref, mi_ref,
                     o_ref, a_scr, y_scr):
    f1 = pl.program_id(2)
    s1_count = FFT_N1 // 2
    _fft_stage1(z_ref, coef_ref, f1, s1_count, n2, a_scr)
    x = _dot(mf_ref[0], a_scr[...])
    xr, xi = x[:n2], x[n2:]
    hr, hi = hr_ref[0], hi_ref[0]
    y_scr[pl.ds(0, n2), :] = (xr * hr - xi * hi).astype(BF16)
    y_scr[pl.ds(n2, n2), :] = (xr * hi + xi * hr).astype(BF16)
    bm = _dot(mi_ref[0], y_scr[...])
    br, bi = bm[:n2], bm[n2:]

    @pl.when(f1 == 0)
    def _():
        for t1 in range(s1_count):
            o_ref[0, pl.ds(t1 * n2, n2), :] = br

    @pl.when(f1 > 0)
    def _():
        for t1 in range(s1_count):
            gr = gcoef_ref[f1, 2 * t1]
            gi = gcoef_ref[f1, 2 * t1 + 1]
            o_ref[0, pl.ds(t1 * n2, n2), :] += gr * br - gi * bi

    @pl.when(f1 == FFT_F1 - 1)
    def _():
        z = z_ref[0]
        o_ref[0] = gate_ref[0] * (o_ref[0] + skip_ref[...] * z)


def _longconv(u, z_col, gate_col, z_arr, order, skip, hr, hi, tables):
    m_fwd, m_inv, coef, gcoef = tables
    B, L, _ = z_arr.shape
    n2 = 2 * L // FFT_N1
    nct = HY_WIDTH // LANES
    return pl.pallas_call(
        functools.partial(_longconv_kernel, n2),
        grid=(B, nct, FFT_F1),
        in_specs=[pl.BlockSpec(memory_space=pltpu.SMEM),
                  pl.BlockSpec(memory_space=pltpu.SMEM),
                  pl.BlockSpec((1, L, LANES), lambda b, j, f: (b, 0, z_col * nct + j)),
                  pl.BlockSpec((1, L, LANES), lambda b, j, f: (b, 0, gate_col * nct + j)),
                  pl.BlockSpec((1, LANES), lambda b, j, f: (0, j)),
                  pl.BlockSpec((1, n2, LANES), lambda b, j, f: (order, f, j)),
                  pl.BlockSpec((1, n2, LANES), lambda b, j, f: (order, f, j)),
                  pl.BlockSpec((1, 2 * n2, 2 * n2), lambda b, j, f: (f, 0, 0)),
                  pl.BlockSpec((1, 2 * n2, 2 * n2), lambda b, j, f: (f, 0, 0))],
        out_specs=pl.BlockSpec((1, L, LANES), lambda b, j, f: (b, 0, j)),
        out_shape=jax.ShapeDtypeStruct((B, L, HY_WIDTH), F32),
        scratch_shapes=[pltpu.VMEM((2 * n2, LANES), BF16), pltpu.VMEM((2 * n2, LANES), BF16)],
        compiler_params=_cparams(("parallel", "parallel", "arbitrary")),
        name=f"hy_longconv{order}",
    )(coef, gcoef, z_arr, u, skip[order].reshape(1, HY_WIDTH), hr, hi, m_fwd, m_inv)


def _hyena(u_h, conv_w, conv_b, fw1, fb1, fw2, fb2, fw3, fb3, fw4, sin_freq, skip):
    B, L, _ = u_h.shape
    u = _shortconv(u_h, conv_w, conv_b)
    kc = _filters(L, fw1, fb1, fw2, fb2, fw3, fb3, fw4, sin_freq)
    tables = _fft_tables(2 * L // FFT_N1)
    hr, hi = _spectrum(kc, tables[0], tables[2])
    z1 = _longconv(u, 0, 1, u, 0, skip, hr, hi, tables)
    return _longconv(u, 0, 2, z1, 1, skip, hr, hi, tables)


def _rw_prep_kernel(u_ref, up_ref, un_ref, mu_ref, w0_ref, w2f_ref, w2b_ref, a0_ref, a2f_ref, a2b_ref,
                    g2_ref, kk_ref, ka_ref, rk_ref, p_ref,
                    r_o, v_o, kk_o, lwf_o, lwb_o, kdf_o, kdb_o, bf_o, bb_o, g_o, bonus_o):
    j = pl.program_id(1)
    nj = pl.num_programs(1)
    u = u_ref[0]
    tb = u.shape[0]
    prow = jnp.where(j == 0, 0.0, up_ref[0, 7:8, :])
    nrow = jnp.where(j == nj - 1, 0.0, un_ref[0, 0:1, :])
    row = lax.broadcasted_iota(jnp.int32, (tb, 1), 0)
    prev = jnp.where(row == 0, prow, pltpu.roll(u, 1, 0))
    nxt = jnp.where(row == tb - 1, nrow, pltpu.roll(u, tb - 1, 0))
    xs = u + mu_ref[...] * (0.5 * (prev + nxt) - u)
    W = RW_WIDTH
    r, k, v = xs[:, 0:W], xs[:, W:2 * W], xs[:, 2 * W:3 * W]
    wd = jnp.tanh(xs[:, 3 * W:3 * W + 2 * RW_LORA]).astype(BF16)
    ad = xs[:, 3 * W + 2 * RW_LORA:3 * W + 4 * RW_LORA].astype(BF16)
    gd = jax.nn.sigmoid(xs[:, 3 * W + 4 * RW_LORA:]).astype(BF16)
    p = p_ref[...]
    kkn = k * kk_ref[...]
    nrm = jnp.sqrt(_segsum(kkn * kkn, p))
    kk = kkn / jnp.maximum(nrm, 1e-12)
    ka = ka_ref[...]
    lw_f = -RW_DECAY_SCALE * jax.nn.sigmoid(w0_ref[0:1, :] + _dot(wd, w2f_ref[...]))
    lw_b = -RW_DECAY_SCALE * jax.nn.sigmoid(w0_ref[1:2, :] + _dot(wd, w2b_ref[...]))
    a_f = jax.nn.sigmoid(a0_ref[0:1, :] + _dot(ad, a2f_ref[...]))
    a_b = jax.nn.sigmoid(a0_ref[1:2, :] + _dot(ad, a2b_ref[...]))
    kd_f = k * (1.0 + (a_f - 1.0) * ka)
    kd_b = k * (1.0 + (a_b - 1.0) * ka)
    r_o[0] = r
    v_o[0] = v
    kk_o[0] = kk
    lwf_o[0] = lw_f
    lwb_o[0] = lw_b
    kdf_o[0] = kd_f
    kdb_o[0] = kd_b
    bf_o[0] = kk * a_f
    bb_o[0] = kk * a_b
    g_o[0] = _dot(gd, g2_ref[...])
    bonus_o[0] = _segsum(r * (kd_f + kd_b) * rk_ref[...], p) * v


def _head_ones():
    h = jnp.arange(RW_WIDTH, dtype=jnp.int32) // RW_HEAD
    return (h[:, None] == h[None, :]).astype(BF16)


def _rw_prep(u_r, mu, w0, w2, a0, a2, g2, k_k, k_a, r_k, p_ones, tb=256):
    B, L, C = u_r.shape
    tb = min(tb, L)
    W = RW_WIDTH
    zeros = jnp.zeros((RW_LORA, W), F32)
    w2f = jnp.concatenate([w2[0], zeros], axis=0).astype(BF16)
    w2b = jnp.concatenate([zeros, w2[1]], axis=0).astype(BF16)
    a2f = jnp.concatenate([a2[0], zeros], axis=0).astype(BF16)
    a2b = jnp.concatenate([zeros, a2[1]], axis=0).astype(BF16)
    full = lambda shape: pl.BlockSpec(shape, lambda b, j: tuple(0 for _ in shape))
    tok = lambda: pl.BlockSpec((1, tb, W), lambda b, j: (b, j, 0))
    g8 = tb // 8
    return pl.pallas_call(
        _rw_prep_kernel,
        grid=(B, L // tb),
        in_specs=[pl.BlockSpec((1, tb, C), lambda b, j: (b, j, 0)),
                  pl.BlockSpec((1, 8, C), lambda b, j: (b, jnp.maximum(j * g8 - 1, 0), 0)),
                  pl.BlockSpec((1, 8, C), lambda b, j: (b, jnp.minimum((j + 1) * g8, L // 8 - 1), 0)),
                  full((1, C)), full((2, W)), full((2 * RW_LORA, W)), full((2 * RW_LORA, W)),
                  full((2, W)), full((2 * RW_LORA, W)), full((2 * RW_LORA, W)),
                  full((RW_LORA_G, W)), full((1, W)), full((1, W)), full((1, W)), full((W, W))],
        out_specs=[tok() for _ in range(11)],
        out_shape=[jax.ShapeDtypeStruct((B, L, W), F32)] * 11,
        compiler_params=_cparams(("parallel", "parallel")),
        name="rw_prep",
    )(u_r, u_r, u_r, mu.reshape(1, C), w0, w2f, w2b, a0, a2f, a2b, g2.astype(BF16),
      k_k.reshape(1, W), k_a.reshape(1, W), r_k.reshape(1, W), p_ones)


def _tri_inverse(a_b):
    c = a_b.shape[0]
    eye = (lax.broadcasted_iota(jnp.int32, (c, c), 0) == lax.broadcasted_iota(jnp.int32, (c, c), 1)).astype(F32)
    nmat = -a_b
    x = eye + nmat
    p = nmat
    steps = int(math.log2(c))
    for s in range(1, steps):
        pb = p.astype(BF16)
        p = _dot(pb, pb)
        x = x + _dot(x.astype(BF16), p.astype(BF16))
    return x


def _rw_direction(r, v, kk, lw, kd, b, s_ref, o_ref, reverse):
    c = r.shape[0]
    ri = lax.broadcasted_iota(jnp.int32, (c, c), 0)
    ci = lax.broadcasted_iota(jnp.int32, (c, c), 1)
    if reverse:
        tri = (ci >= ri).astype(F32)
        strict, causal = ci > ri, ci >= ri
        last = 0
    else:
        tri = (ci <= ri).astype(F32)
        strict, causal = ci < ri, ci <= ri
        last = c - 1
    cum = _dot(tri, lw, HIGHEST)
    tot = cum[last:last + 1, :]
    e_neg = jnp.exp(-cum)
    kt = (kk * jnp.exp(cum - lw)).astype(BF16)
    rt = (r * jnp.exp(cum)).astype(BF16)
    khat = (kd * e_neg).astype(BF16)
    bhat = (b * e_neg).astype(BF16)
    e_rem = jnp.exp(tot - cum)
    kbar = (kd * e_rem).astype(BF16)
    bbar = (b * e_rem).astype(BF16)
    wtot = jnp.exp(tot)
    vb = v.astype(BF16)
    for h in range(RW_HEADS):
        sl = slice(h * RW_HEAD, (h + 1) * RW_HEAD)
        lh = jnp.concatenate([kt[:, sl], rt[:, sl]], axis=0)
        gk = _dot_nt(lh, khat[:, sl])
        gb = _dot_nt(lh, bhat[:, sl])
        a_k = jnp.where(strict, gk[:c], 0.0)
        b_k = jnp.where(causal, gk[c:], 0.0)
        a_b = jnp.where(strict, gb[:c], 0.0)
        b_b = jnp.where(causal, gb[c:], 0.0)
        tmat = _tri_inverse(a_b)
        s0 = s_ref[h]
        ks = _dot_nt(lh, s0.astype(BF16))
        vh = vb[:, sl]
        av = _dot(jnp.concatenate([a_k, b_k], axis=0).astype(BF16), vh)
        u = _dot(tmat.astype(BF16), (ks[:c] + av[:c]).astype(BF16))
        ub = u.astype(BF16)
        y = ks[c:] + av[c:] - _dot(b_b.astype(BF16), ub)
        o_ref[0, :, sl] = y
        s_ref[h] = s0 * wtot[:, sl] + _dot_tn(vh, kbar[:, sl]) - _dot_tn(ub, bbar[:, sl])


def _rw_scan_kernel(rf, vf, kkf, lwf, kdf, bf, rb, vb, kkb, lwb, kdb, bb, yf_ref, yb_ref, sf_ref, sb_ref):
    @pl.when(pl.program_id(1) == 0)
    def _():
        sf_ref[...] = jnp.zeros_like(sf_ref)
        sb_ref[...] = jnp.zeros_like(sb_ref)

    _rw_direction(rf[0], vf[0], kkf[0], lwf[0], kdf[0], bf[0], sf_ref, yf_ref, False)
    _rw_direction(rb[0], vb[0], kkb[0], lwb[0], kdb[0], bb[0], sb_ref, yb_ref, True)


def _rw_scan(r, v, kk, lwf, lwb, kdf, kdb, bf, bb):
    B, L, W = r.shape
    c = min(RW_CHUNK, L)
    nc = L // c
    fwd = lambda: pl.BlockSpec((1, c, W), lambda b, i: (b, i, 0))
    bwd = lambda: pl.BlockSpec((1, c, W), lambda b, i: (b, nc - 1 - i, 0))
    return pl.pallas_call(
        _rw_scan_kernel,
        grid=(B, nc),
        in_specs=[fwd() for _ in range(6)] + [bwd() for _ in range(6)],
        out_specs=[fwd(), bwd()],
        out_shape=[jax.ShapeDtypeStruct((B, L, W), F32)] * 2,
        scratch_shapes=[pltpu.VMEM((RW_HEADS, RW_HEAD, RW_HEAD), F32)] * 2,
        compiler_params=_cparams(("parallel", "arbitrary")),
        name="rw_scan",
    )(r, v, kk, lwf, kdf, bf, r, v, kk, lwb, kdb, bb)


def _merge_kernel(x_ref, yh_ref, yf_ref, yb_ref, bonus_ref, g_ref, gates_ref, p_ref, gnw_ref, gnb_ref,
                  why_ref, wrw_ref, wo_ref, lnw_ref, lnb_ref, o_ref):
    p = p_ref[...]
    y = yf_ref[...] + yb_ref[...]
    mu = _segsum(y, p) * (1.0 / RW_HEAD)
    yc = y - mu
    var = _segsum(yc * yc, p) * (1.0 / RW_HEAD)
    yn = yc * lax.rsqrt(var + RW_GN_EPS) * gnw_ref[...] + gnb_ref[...]
    y_r = (yn + bonus_ref[...]) * g_ref[...]
    ph = _dot(yh_ref[...].astype(BF16), why_ref[...])
    pr = _dot(y_r.astype(BF16), wrw_ref[...])
    gates = jax.nn.sigmoid(gates_ref[...])
    m = gates[:, :D_MODEL] * ph + gates[:, D_MODEL:] * pr
    mix = _dot(m.astype(BF16), wo_ref[...])
    o_ref[...] = _layer_norm(DN_ALPHA * x_ref[...] + mix, lnw_ref[...], lnb_ref[...])


def _merge(x2, yh, yf, yb, bonus, g, gates, p_ones, gn_w, gn_b, w_hy_out, w_rw_out, w_o, ln_w, ln_b, tm=512):
    m, d = x2.shape
    tm = min(tm, m)
    W = RW_WIDTH
    row = lambda width: pl.BlockSpec((tm, width), lambda i: (i, 0))
    full = lambda shape: pl.BlockSpec(shape, lambda i: tuple(0 for _ in shape))
    return pl.pallas_call(
        _merge_kernel,
        grid=(m // tm,),
        in_specs=[row(d), row(HY_WIDTH), row(W), row(W), row(W), row(W), row(GATE_COLS),
                  full((W, W)), full((1, W)), full((1, W)),
                  full((HY_WIDTH, d)), full((W, d)), full((d, d)), full((1, d)), full((1, d))],
        out_specs=row(d),
        out_shape=jax.ShapeDtypeStruct((m, d), F32),
        compiler_params=_cparams(("parallel",)),
        name="merge_ln1",
    )(x2, yh, yf, yb, bonus, g, gates, p_ones, gn_w.reshape(1, W), gn_b.reshape(1, W),
      w_hy_out.astype(BF16), w_rw_out.astype(BF16), w_o.astype(BF16), ln_w.reshape(1, d), ln_b.reshape(1, d))


def _ffn_kernel(x_ref, wg_ref, wu_ref, wd_ref, lnw_ref, lnb_ref, o_ref, xb_scr, acc_scr):
    j = pl.program_id(1)

    @pl.when(j == 0)
    def _():
        xb_scr[...] = x_ref[...].astype(BF16)

    xb = xb_scr[...]
    hidden = jax.nn.silu(_dot(xb, wg_ref[...])) * _dot(xb, wu_ref[...])
    part = _dot(hidden.astype(BF16), wd_ref[...])

    @pl.when(j == 0)
    def _():
        acc_scr[...] = part

    @pl.when(j > 0)
    def _():
        acc_scr[...] += part

    @pl.when(j == pl.num_programs(1) - 1)
    def _():
        o_ref[...] = _layer_norm(DN_ALPHA * x_ref[...] + acc_scr[...], lnw_ref[...], lnb_ref[...])


def _ffn(x2, w_gate, w_up, w_down, ln_w, ln_b, tm=1024, th=256):
    m, d = x2.shape
    tm = min(tm, m)
    fh = w_gate.shape[1]
    return pl.pallas_call(
        _ffn_kernel,
        grid=(m // tm, fh // th),
        in_specs=[pl.BlockSpec((tm, d), lambda i, j: (i, 0)),
                  pl.BlockSpec((d, th), lambda i, j: (0, j)),
                  pl.BlockSpec((d, th), lambda i, j: (0, j)),
                  pl.BlockSpec((th, d), lambda i, j: (j, 0)),
                  pl.BlockSpec((1, d), lambda i, j: (0, 0)),
                  pl.BlockSpec((1, d), lambda i, j: (0, 0))],
        out_specs=pl.BlockSpec((tm, d), lambda i, j: (i, 0)),
        out_shape=jax.ShapeDtypeStruct((m, d), F32),
        scratch_shapes=[pltpu.VMEM((tm, d), BF16), pltpu.VMEM((tm, d), F32)],
        compiler_params=_cparams(("parallel", "arbitrary")),
        name="ffn_ln2",
    )(x2, w_gate.astype(BF16), w_up.astype(BF16), w_down.astype(BF16), ln_w.reshape(1, d), ln_b.reshape(1, d))


def _layer(x, w_in, hy_conv_w, hy_conv_b, hy_filt_w1, hy_filt_b1, hy_filt_w2, hy_filt_b2,
           hy_filt_w3, hy_filt_b3, hy_filt_w4, hy_sin_freq, hy_skip, rw_mu, rw_w0, rw_w2,
           rw_a0, rw_a2, rw_g2, rw_k_k, rw_k_a, rw_r_k, rw_gn_w, rw_gn_b, w_hy_out, w_rw_out,
           w_o, ln1_w, ln1_b, ffn_w_gate, ffn_w_up, ffn_w_down, ln2_w, ln2_b):
    B, L, D = x.shape
    x2 = x.reshape(B * L, D)
    u_h, u_r, gates = _in_proj(x2, w_in.astype(BF16))
    y_h = _hyena(u_h.reshape(B, L, HY_COLS), hy_conv_w, hy_conv_b, hy_filt_w1, hy_filt_b1, hy_filt_w2,
                 hy_filt_b2, hy_filt_w3, hy_filt_b3, hy_filt_w4, hy_sin_freq, hy_skip)
    p_ones = _head_ones()
    r, v, kk, lwf, lwb, kdf, kdb, bf, bb, g, bonus = _rw_prep(
        u_r.reshape(B, L, RW_COLS), rw_mu, rw_w0, rw_w2, rw_a0, rw_a2, rw_g2, rw_k_k, rw_k_a, rw_r_k, p_ones)
    yf, yb = _rw_scan(r, v, kk, lwf, lwb, kdf, kdb, bf, bb)
    flat = lambda a: a.reshape(B * L, a.shape[-1])
    h = _merge(x2, flat(y_h), flat(yf), flat(yb), flat(bonus), flat(g), gates, p_ones, rw_gn_w, rw_gn_b,
               w_hy_out, w_rw_out, w_o, ln1_w, ln1_b)
    out = _ffn(h, ffn_w_gate, ffn_w_up, ffn_w_down, ln2_w, ln2_b)
    return out.reshape(B, L, D)


def kernel(x, w_in, hy_conv_w, hy_conv_b, hy_filt_w1, hy_filt_b1, hy_filt_w2, hy_filt_b2, hy_filt_w3, hy_filt_b3, hy_filt_w4, hy_sin_freq, hy_skip, rw_mu, rw_w0, rw_w2, rw_a0, rw_a2, rw_g2, rw_k_k, rw_k_a, rw_r_k, rw_gn_w, rw_gn_b, w_hy_out, w_rw_out, w_o, ln1_w, ln1_b, ffn_w_gate, ffn_w_up, ffn_w_down, ln2_w, ln2_b):
    params = (w_in, hy_conv_w, hy_conv_b, hy_filt_w1, hy_filt_b1, hy_filt_w2, hy_filt_b2, hy_filt_w3,
              hy_filt_b3, hy_filt_w4, hy_sin_freq, hy_skip, rw_mu, rw_w0, rw_w2, rw_a0, rw_a2, rw_g2,
              rw_k_k, rw_k_a, rw_r_k, rw_gn_w, rw_gn_b, w_hy_out, w_rw_out, w_o, ln1_w, ln1_b,
              ffn_w_gate, ffn_w_up, ffn_w_down, ln2_w, ln2_b)
    for l in range(w_in.shape[0]):
        x = _layer(x, *[p[l] for p in params])
    return x
```

```python
import functools
import math

import jax
import jax.numpy as jnp
from jax import lax
from jax.experimental import pallas as pl
from jax.experimental.pallas import tpu as pltpu

F32 = jnp.float32
BF16 = jnp.bfloat16
HIGHEST = lax.Precision.HIGHEST

D_MODEL = 1024
HY_WIDTH = 512
HY_ORDER = 2
HY_BANDS = 16
HY_FILT_HIDDEN = 64
HY_FAST_DECAY = 0.3
HY_SLOW_DECAY = 1.5
HY_DECAY_TARGET = 1e-2
HY_MAX_DECAY = math.log(HY_DECAY_TARGET) / HY_FAST_DECAY
HY_MIN_DECAY = math.log(HY_DECAY_TARGET) / HY_SLOW_DECAY
HY_COLS = 3 * HY_WIDTH
RW_WIDTH = 512
RW_HEAD = 64
RW_HEADS = RW_WIDTH // RW_HEAD
RW_LORA = 64
RW_LORA_G = 128
RW_GN_EPS = 64e-5
RW_COLS = 3 * RW_WIDTH + 4 * RW_LORA + RW_LORA_G
GATE_COLS = 2 * D_MODEL
FFN_HIDDEN = ((8 * D_MODEL + 3 * 256 - 1) // (3 * 256)) * 256
DEPTH = 1
DN_ALPHA = (2.0 * DEPTH) ** 0.25
LN_EPS = 1e-5
RW_DECAY_SCALE = math.exp(-0.5)

LANES = 128
VMEM_LIMIT = 56 * 1024 * 1024

FFT_N1 = 32
FFT_F1 = FFT_N1 // 2 + 1
RW_CHUNK = 64


def _cparams(sem, vmem=VMEM_LIMIT):
    return pltpu.CompilerParams(dimension_semantics=sem, vmem_limit_bytes=vmem)


def _dot(a, b, precision=None):
    return jnp.dot(a, b, preferred_element_type=F32, precision=precision)


def _dot_nt(a, b):
    return lax.dot_general(a, b, (((1,), (1,)), ((), ())), preferred_element_type=F32)


def _dot_tn(a, b):
    return lax.dot_general(a, b, (((0,), (0,)), ((), ())), preferred_element_type=F32)


def _layer_norm(h, w, b):
    mu = jnp.mean(h, axis=-1, keepdims=True)
    c = h - mu
    var = jnp.mean(c * c, axis=-1, keepdims=True)
    return c * lax.rsqrt(var + LN_EPS) * w + b


def _segsum(x, p):
    hi = x.astype(BF16)
    lo = (x - hi.astype(F32)).astype(BF16)
    return _dot(hi, p) + _dot(lo, p)


def _split3(x):
    h1 = x.astype(BF16)
    r1 = x - h1.astype(F32)
    h2 = r1.astype(BF16)
    h3 = (r1 - h2.astype(F32)).astype(BF16)
    return h1, h2, h3


def _dot_parts(m, parts):
    out = _dot(m, parts[0])
    for part in parts[1:]:
        out = out + _dot(m, part)
    return out


def _in_proj_kernel(x_ref, w_ref, uh_ref, ur_ref, g_ref):
    xb = x_ref[...].astype(BF16)
    uh_ref[...] = _dot(xb, w_ref[:, :HY_COLS])
    ur_ref[...] = _dot(xb, w_ref[:, HY_COLS:HY_COLS + RW_COLS])
    g_ref[...] = _dot(xb, w_ref[:, HY_COLS + RW_COLS:])


def _in_proj(x2, w_in_bf, tm=256):
    m, d = x2.shape
    n = w_in_bf.shape[1]
    return pl.pallas_call(
        _in_proj_kernel,
        grid=(m // tm,),
        in_specs=[pl.BlockSpec((tm, d), lambda i: (i, 0)),
                  pl.BlockSpec((d, n), lambda i: (0, 0))],
        out_specs=[pl.BlockSpec((tm, HY_COLS), lambda i: (i, 0)),
                   pl.BlockSpec((tm, RW_COLS), lambda i: (i, 0)),
                   pl.BlockSpec((tm, GATE_COLS), lambda i: (i, 0))],
        out_shape=[jax.ShapeDtypeStruct((m, HY_COLS), F32),
                   jax.ShapeDtypeStruct((m, RW_COLS), F32),
                   jax.ShapeDtypeStruct((m, GATE_COLS), F32)],
        compiler_params=_cparams(("parallel",)),
        name="in_proj",
    )(x2, w_in_bf)


def _shortconv_kernel(u_ref, w_ref, b_ref, o_ref):
    u = u_ref[0]
    L = u.shape[0]
    row = lax.broadcasted_iota(jnp.int32, (L, 1), 0)
    prev = jnp.where(row == 0, 0.0, pltpu.roll(u, 1, 0))
    nxt = jnp.where(row == L - 1, 0.0, pltpu.roll(u, L - 1, 0))
    o_ref[0] = w_ref[0:1, :] * prev + w_ref[1:2, :] * u + w_ref[2:3, :] * nxt + b_ref[...]


def _shortconv(u_h, conv_w, conv_b):
    B, L, C = u_h.shape
    return pl.pallas_call(
        _shortconv_kernel,
        grid=(B, C // LANES),
        in_specs=[pl.BlockSpec((1, L, LANES), lambda b, j: (b, 0, j)),
                  pl.BlockSpec((3, LANES), lambda b, j: (0, j)),
                  pl.BlockSpec((1, LANES), lambda b, j: (0, j))],
        out_specs=pl.BlockSpec((1, L, LANES), lambda b, j: (b, 0, j)),
        out_shape=jax.ShapeDtypeStruct((B, L, C), F32),
        compiler_params=_cparams(("parallel", "parallel")),
        name="hy_shortconv",
    )(u_h, conv_w, conv_b.reshape(1, C))


def _filter_kernel(L, tb, w1t_ref, w1c_ref, w1s_ref, b1_ref, w2_ref, b2_ref, w3_ref, b3_ref,
                   w4_ref, sf_ref, freq_ref, delta_ref, kc_ref):
    n = 2 * L
    i = pl.program_id(0)
    row = i * tb + lax.broadcasted_iota(jnp.int32, (tb, 1), 0)
    second = row >= L
    pos = jnp.where(second, n - row, row).astype(F32)
    t = pos / float(L - 1)
    ang = (2.0 * math.pi * pos / float(L)) * freq_ref[...]
    pre = t * w1t_ref[...] + _dot(jnp.cos(ang), w1c_ref[...], HIGHEST) \
        - _dot(jnp.sin(ang), w1s_ref[...], HIGHEST) + b1_ref[...]
    h = jnp.sin(sf_ref[0:1, :] * pre)
    h = jnp.sin(sf_ref[1:2, :] * (_dot(h, w2_ref[...], HIGHEST) + b2_ref[...]))
    h = jnp.sin(sf_ref[2:3, :] * (_dot(h, w3_ref[...], HIGHEST) + b3_ref[...]))
    h4 = _dot(h, w4_ref[...], HIGHEST)
    window = jnp.exp(-t * delta_ref[...])
    keep = jnp.where(row == L, 0.0, 1.0)
    first = jnp.where(row == 0, 1.0, 0.0)
    for o in range(HY_ORDER):
        base = o * 2 * HY_WIDTH
        fwd = h4[:, base:base + HY_WIDTH]
        bwd = h4[:, base + HY_WIDTH:base + 2 * HY_WIDTH]
        kc_ref[o] = window * (jnp.where(second, bwd, fwd) * keep + first * bwd)


def _filters(L, fw1, fb1, fw2, fb2, fw3, fb3, fw4, sin_freq, tb=512):
    n = 2 * L
    tb = min(tb, n)
    freqs = jnp.linspace(1e-4, HY_BANDS - 1, HY_BANDS, dtype=F32).reshape(1, HY_BANDS)
    deltas = jnp.abs(jnp.linspace(HY_MIN_DECAY, HY_MAX_DECAY, HY_WIDTH, dtype=F32)).reshape(1, HY_WIDTH)
    hid = HY_FILT_HIDDEN
    full = lambda shape: pl.BlockSpec(shape, lambda i: tuple(0 for _ in shape))
    return pl.pallas_call(
        functools.partial(_filter_kernel, L, tb),
        grid=(n // tb,),
        in_specs=[full((1, hid)), full((HY_BANDS, hid)), full((HY_BANDS, hid)), full((1, hid)),
                  full((hid, hid)), full((1, hid)), full((hid, hid)), full((1, hid)),
                  full((hid, HY_ORDER * 2 * HY_WIDTH)), full((3, hid)),
                  full((1, HY_BANDS)), full((1, HY_WIDTH))],
        out_specs=pl.BlockSpec((HY_ORDER, tb, HY_WIDTH), lambda i: (0, i, 0)),
        out_shape=jax.ShapeDtypeStruct((HY_ORDER, n, HY_WIDTH), F32),
        compiler_params=_cparams(("parallel",)),
        name="hy_filters",
    )(fw1[0:1], fw1[1:1 + HY_BANDS], fw1[1 + HY_BANDS:], fb1.reshape(1, hid), fw2, fb2.reshape(1, hid),
      fw3, fb3.reshape(1, hid), fw4, sin_freq, freqs, deltas)


def _fft_tables(n2):
    n = FFT_N1 * n2
    f1 = jnp.arange(FFT_F1, dtype=jnp.int32)[:, None, None]
    a = jnp.arange(n2, dtype=jnp.int32)[None, :, None]
    b = jnp.arange(n2, dtype=jnp.int32)[None, None, :]
    ph_f = ((b * (f1 + FFT_N1 * a)) % n).astype(F32) * (2.0 * math.pi / n)
    cr, ci = jnp.cos(ph_f), -jnp.sin(ph_f)
    m_fwd = jnp.concatenate([jnp.concatenate([cr, -ci], axis=2),
                             jnp.concatenate([ci, cr], axis=2)], axis=1).astype(BF16)
    ph_i = ((a * (f1 + FFT_N1 * b)) % n).astype(F32) * (2.0 * math.pi / n)
    herm = jnp.where((f1 == 0) | (f1 == FFT_N1 // 2), 1.0, 2.0) / n
    dr, di = herm * jnp.cos(ph_i), herm * jnp.sin(ph_i)
    m_inv = jnp.concatenate([jnp.concatenate([dr, -di], axis=2),
                             jnp.concatenate([di, dr], axis=2)], axis=1).astype(BF16)
    k = (jnp.arange(FFT_F1, dtype=jnp.int32)[:, None] * jnp.arange(FFT_N1, dtype=jnp.int32)[None, :]) % FFT_N1
    ph = k.astype(F32) * (2.0 * math.pi / FFT_N1)
    coef = jnp.stack([jnp.cos(ph), -jnp.sin(ph)], axis=-1).reshape(FFT_F1, 2 * FFT_N1)
    gcoef = jnp.stack([jnp.cos(ph), jnp.sin(ph)], axis=-1).reshape(FFT_F1, 2 * FFT_N1)
    return m_fwd, m_inv, coef, gcoef


def _row_block(n2):
    return 32 if n2 % 32 == 0 else n2


def _fft_stage1(z_ref, coef_ref, f1, s1_count, n2, a_scr):
    rb = _row_block(n2)
    for r in range(n2 // rb):
        acc_r = None
        acc_i = None
        for s1 in range(s1_count):
            zt = z_ref[0, pl.ds(s1 * n2 + r * rb, rb), :]
            cr = coef_ref[f1, 2 * s1]
            ci = coef_ref[f1, 2 * s1 + 1]
            acc_r = cr * zt if acc_r is None else acc_r + cr * zt
            acc_i = ci * zt if acc_i is None else acc_i + ci * zt
        a_scr[pl.ds(r * rb, rb), :] = acc_r.astype(BF16)
        a_scr[pl.ds(n2 + r * rb, rb), :] = acc_i.astype(BF16)


def _spectrum_kernel(n2, coef_ref, kc_ref, mf_ref, hr_ref, hi_ref, a_scr):
    f1 = pl.program_id(2)
    _fft_stage1(kc_ref, coef_ref, f1, FFT_N1, n2, a_scr)
    x = _dot(mf_ref[0], a_scr[...])
    hr_ref[0] = x[:n2]
    hi_ref[0] = x[n2:]


def _spectrum(kc, m_fwd, coef):
    order, n, C = kc.shape
    n2 = n // FFT_N1
    spec = lambda: pl.BlockSpec((1, n2, LANES), lambda o, j, f: (o, f, j))
    return pl.pallas_call(
        functools.partial(_spectrum_kernel, n2),
        grid=(order, C // LANES, FFT_F1),
        in_specs=[pl.BlockSpec(memory_space=pltpu.SMEM),
                  pl.BlockSpec((1, n, LANES), lambda o, j, f: (o, 0, j)),
                  pl.BlockSpec((1, 2 * n2, 2 * n2), lambda o, j, f: (f, 0, 0))],
        out_specs=[spec(), spec()],
        out_shape=[jax.ShapeDtypeStruct((order, FFT_F1 * n2, C), F32)] * 2,
        scratch_shapes=[pltpu.VMEM((2 * n2, LANES), BF16)],
        compiler_params=_cparams(("parallel", "parallel", "arbitrary")),
        name="hy_spectrum",
    )(coef, kc, m_fwd)


def _longconv_kernel(n2, coef_ref, gcoef_ref, z_ref, gate_ref, skip_ref, hr_ref, hi_ref, mf_ref, mi_ref,
                     o_ref, a_scr, y_scr):
    f1 = pl.program_id(2)
    s1_count = FFT_N1 // 2
    _fft_stage1(z_ref, coef_ref, f1, s1_count, n2, a_scr)
    x = _dot(mf_ref[0], a_scr[...])
    xr, xi = x[:n2], x[n2:]
    hr, hi = hr_ref[0], hi_ref[0]
    y_scr[pl.ds(0, n2), :] = (xr * hr - xi * hi).astype(BF16)
    y_scr[pl.ds(n2, n2), :] = (xr * hi + xi * hr).astype(BF16)
    bm = _dot(mi_ref[0], y_scr[...])
    br, bi = bm[:n2], bm[n2:]

    @pl.when(f1 == 0)
    def _():
        for t1 in range(s1_count):
            o_ref[0, pl.ds(t1 * n2, n2), :] = br

    @pl.when(f1 > 0)
    def _():
        for t1 in range(s1_count):
            gr = gcoef_ref[f1, 2 * t1]
            gi = gcoef_ref[f1, 2 * t1 + 1]
            o_ref[0, pl.ds(t1 * n2, n2), :] += gr * br - gi * bi

    @pl.when(f1 == FFT_F1 - 1)
    def _():
        z = z_ref[0]
        o_ref[0] = gate_ref[0] * (o_ref[0] + skip_ref[...] * z)


def _longconv(u, z_col, gate_col, z_arr, order, skip, hr, hi, tables):
    m_fwd, m_inv, coef, gcoef = tables
    B, L, _ = z_arr.shape
    n2 = 2 * L // FFT_N1
    nct = HY_WIDTH // LANES
    return pl.pallas_call(
        functools.partial(_longconv_kernel, n2),
        grid=(B, nct, FFT_F1),
        in_specs=[pl.BlockSpec(memory_space=pltpu.SMEM),
                  pl.BlockSpec(memory_space=pltpu.SMEM),
                  pl.BlockSpec((1, L, LANES), lambda b, j, f: (b, 0, z_col * nct + j)),
                  pl.BlockSpec((1, L, LANES), lambda b, j, f: (b, 0, gate_col * nct + j)),
                  pl.BlockSpec((1, LANES), lambda b, j, f: (0, j)),
                  pl.BlockSpec((1, n2, LANES), lambda b, j, f: (order, f, j)),
                  pl.BlockSpec((1, n2, LANES), lambda b, j, f: (order, f, j)),
                  pl.BlockSpec((1, 2 * n2, 2 * n2), lambda b, j, f: (f, 0, 0)),
                  pl.BlockSpec((1, 2 * n2, 2 * n2), lambda b, j, f: (f, 0, 0))],
        out_specs=pl.BlockSpec((1, L, LANES), lambda b, j, f: (b, 0, j)),
        out_shape=jax.ShapeDtypeStruct((B, L, HY_WIDTH), F32),
        scratch_shapes=[pltpu.VMEM((2 * n2, LANES), BF16), pltpu.VMEM((2 * n2, LANES), BF16)],
        compiler_params=_cparams(("parallel", "parallel", "arbitrary")),
        name=f"hy_longconv{order}",
    )(coef, gcoef, z_arr, u, skip[order].reshape(1, HY_WIDTH), hr, hi, m_fwd, m_inv)


def _hyena(u_h, conv_w, conv_b, fw1, fb1, fw2, fb2, fw3, fb3, fw4, sin_freq, skip):
    B, L, _ = u_h.shape
    u = _shortconv(u_h, conv_w, conv_b)
    kc = _filters(L, fw1, fb1, fw2, fb2, fw3, fb3, fw4, sin_freq)
    tables = _fft_tables(2 * L // FFT_N1)
    hr, hi = _spectrum(kc, tables[0], tables[2])
    z1 = _longconv(u, 0, 1, u, 0, skip, hr, hi, tables)
    return _longconv(u, 0, 2, z1, 1, skip, hr, hi, tables)


def _rw_prep_kernel(u_ref, up_ref, un_ref, mu_ref, w0_ref, w2f_ref, w2b_ref, a0_ref, a2f_ref, a2b_ref,
                    g2_ref, kk_ref, ka_ref, rk_ref, p_ref, trif_ref, trib_ref, ones_ref, sel_ref,
                    v_o, ktf_o, rtf_o, khf_o, bhf_o, kbf_o, bbf_o, wtf_o,
                    ktb_o, rtb_o, khb_o, bhb_o, kbb_o, bbb_o, wtb_o, g_o, bonus_o):
    j = pl.program_id(1)
    nj = pl.num_programs(1)
    u = u_ref[0]
    tb = u.shape[0]
    prow = jnp.where(j == 0, 0.0, up_ref[0, 7:8, :])
    nrow = jnp.where(j == nj - 1, 0.0, un_ref[0, 0:1, :])
    row = lax.broadcasted_iota(jnp.int32, (tb, 1), 0)
    prev = jnp.where(row == 0, prow, pltpu.roll(u, 1, 0))
    nxt = jnp.where(row == tb - 1, nrow, pltpu.roll(u, tb - 1, 0))
    xs = u + mu_ref[...] * (0.5 * (prev + nxt) - u)
    W = RW_WIDTH
    r, k, v = xs[:, 0:W], xs[:, W:2 * W], xs[:, 2 * W:3 * W]
    wd = jnp.tanh(xs[:, 3 * W:3 * W + 2 * RW_LORA]).astype(BF16)
    ad = xs[:, 3 * W + 2 * RW_LORA:3 * W + 4 * RW_LORA].astype(BF16)
    gd = jax.nn.sigmoid(xs[:, 3 * W + 4 * RW_LORA:]).astype(BF16)
    p = p_ref[...]
    kkn = k * kk_ref[...]
    nrm = jnp.sqrt(_segsum(kkn * kkn, p))
    kk = kkn / jnp.maximum(nrm, 1e-12)
    ka = ka_ref[...]
    lw_f = -RW_DECAY_SCALE * jax.nn.sigmoid(w0_ref[0:1, :] + _dot(wd, w2f_ref[...]))
    lw_b = -RW_DECAY_SCALE * jax.nn.sigmoid(w0_ref[1:2, :] + _dot(wd, w2b_ref[...]))
    a_f = jax.nn.sigmoid(a0_ref[0:1, :] + _dot(ad, a2f_ref[...]))
    a_b = jax.nn.sigmoid(a0_ref[1:2, :] + _dot(ad, a2b_ref[...]))
    kd_f = k * (1.0 + (a_f - 1.0) * ka)
    kd_b = k * (1.0 + (a_b - 1.0) * ka)
    v_o[0] = v.astype(BF16)
    ones = ones_ref[...]
    sel = sel_ref[...]

    def scan_operands(lw, kd, b, tri, outs):
        parts = _split3(lw)
        cum = _dot_parts(tri, parts)
        tot = _dot_parts(ones, parts)
        e_neg = jnp.exp(-cum)
        e_rem = jnp.exp(tot - cum)
        kt_o, rt_o, kh_o, bh_o, kb_o, bb_o, wt_o = outs
        kt_o[0] = (kk * jnp.exp(cum - lw)).astype(BF16)
        rt_o[0] = (r * jnp.exp(cum)).astype(BF16)
        kh_o[0] = (kd * e_neg).astype(BF16)
        bh_o[0] = (b * e_neg).astype(BF16)
        kb_o[0] = (kd * e_rem).astype(BF16)
        bb_o[0] = (b * e_rem).astype(BF16)
        wt_o[0] = jnp.exp(_dot_parts(sel, parts))

    scan_operands(lw_f, kd_f, kk * a_f, trif_ref[...], (ktf_o, rtf_o, khf_o, bhf_o, kbf_o, bbf_o, wtf_o))
    scan_operands(lw_b, kd_b, kk * a_b, trib_ref[...], (ktb_o, rtb_o, khb_o, bhb_o, kbb_o, bbb_o, wtb_o))
    g_o[0] = _dot(gd, g2_ref[...])
    bonus_o[0] = _segsum(r * (kd_f + kd_b) * rk_ref[...], p) * v


def _head_ones():
    h = jnp.arange(RW_WIDTH, dtype=jnp.int32) // RW_HEAD
    return (h[:, None] == h[None, :]).astype(BF16)


def _chunk_matrices(tb, c):
    t = jnp.arange(tb, dtype=jnp.int32)
    same = (t[:, None] // c) == (t[None, :] // c)
    tri_f = (same & (t[None, :] <= t[:, None])).astype(BF16)
    tri_b = (same & (t[None, :] >= t[:, None])).astype(BF16)
    sel = (jnp.arange(tb // c, dtype=jnp.int32)[:, None] == (t[None, :] // c)).astype(BF16)
    return tri_f, tri_b, same.astype(BF16), sel


def _rw_prep(u_r, mu, w0, w2, a0, a2, g2, k_k, k_a, r_k, p_ones, tb=512):
    B, L, C = u_r.shape
    tb = min(tb, L)
    W = RW_WIDTH
    c = min(RW_CHUNK, L)
    ncb = tb // c
    tri_f, tri_b, ones, sel = _chunk_matrices(tb, c)
    zeros = jnp.zeros((RW_LORA, W), F32)
    w2f = jnp.concatenate([w2[0], zeros], axis=0).astype(BF16)
    w2b = jnp.concatenate([zeros, w2[1]], axis=0).astype(BF16)
    a2f = jnp.concatenate([a2[0], zeros], axis=0).astype(BF16)
    a2b = jnp.concatenate([zeros, a2[1]], axis=0).astype(BF16)
    full = lambda shape: pl.BlockSpec(shape, lambda b, j: tuple(0 for _ in shape))
    tok = lambda: pl.BlockSpec((1, tb, W), lambda b, j: (b, j, 0))
    wts = lambda: pl.BlockSpec((1, ncb, W), lambda b, j: (b, j, 0))
    bf_tok = jax.ShapeDtypeStruct((B, L, W), BF16)
    f32_tok = jax.ShapeDtypeStruct((B, L, W), F32)
    wt_shape = jax.ShapeDtypeStruct((B, L // c, W), F32)
    g8 = tb // 8
    outs = pl.pallas_call(
        _rw_prep_kernel,
        grid=(B, L // tb),
        in_specs=[pl.BlockSpec((1, tb, C), lambda b, j: (b, j, 0)),
                  pl.BlockSpec((1, 8, C), lambda b, j: (b, jnp.maximum(j * g8 - 1, 0), 0)),
                  pl.BlockSpec((1, 8, C), lambda b, j: (b, jnp.minimum((j + 1) * g8, L // 8 - 1), 0)),
                  full((1, C)), full((2, W)), full((2 * RW_LORA, W)), full((2 * RW_LORA, W)),
                  full((2, W)), full((2 * RW_LORA, W)), full((2 * RW_LORA, W)),
                  full((RW_LORA_G, W)), full((1, W)), full((1, W)), full((1, W)), full((W, W)),
                  full((tb, tb)), full((tb, tb)), full((tb, tb)), full((ncb, tb))],
        out_specs=[tok()] + ([tok() for _ in range(6)] + [wts()]) * 2 + [tok(), tok()],
        out_shape=[bf_tok] + ([bf_tok] * 6 + [wt_shape]) * 2 + [f32_tok, f32_tok],
        compiler_params=_cparams(("parallel", "parallel")),
        name="rw_prep",
    )(u_r, u_r, u_r, mu.reshape(1, C), w0, w2f, w2b, a0, a2f, a2b, g2.astype(BF16),
      k_k.reshape(1, W), k_a.reshape(1, W), r_k.reshape(1, W), p_ones, tri_f, tri_b, ones, sel)
    v, g, bonus = outs[0], outs[15], outs[16]
    fwd_ops = tuple(outs[1:7]) + (v, outs[7])
    bwd_ops = tuple(outs[8:14]) + (v, outs[14])
    return fwd_ops, bwd_ops, g, bonus, ncb


def _bmm(a, b):
    return lax.dot_general(a, b, (((2,), (1,)), ((0,), (0,))), preferred_element_type=F32)


def _bmm_nt(a, b):
    return lax.dot_general(a, b, (((2,), (2,)), ((0,), (0,))), preferred_element_type=F32)


def _bmm_tn(a, b):
    return lax.dot_general(a, b, (((1,), (1,)), ((0,), (0,))), preferred_element_type=F32)


def _tri_inverse(a_b):
    g, c, _ = a_b.shape
    eye = (lax.broadcasted_iota(jnp.int32, (1, c, c), 1) == lax.broadcasted_iota(jnp.int32, (1, c, c), 2)).astype(F32)
    nmat = -a_b
    x = eye + nmat
    p = nmat
    for _ in range(1, int(math.log2(c))):
        pb = p.astype(BF16)
        p = _bmm(pb, pb)
        x = x + _bmm(x.astype(BF16), p.astype(BF16))
    return x


def _heads(x_f, x_b):
    sl = lambda h: slice(h * RW_HEAD, (h + 1) * RW_HEAD)
    return jnp.stack([x_f[:, sl(h)] for h in range(RW_HEADS)] + [x_b[:, sl(h)] for h in range(RW_HEADS)], axis=0)


def _rw_scan_kernel(nc, ncb, ktf, rtf, khf, bhf, kbf, bbf, vf, wtf, ktb, rtb, khb, bhb, kbb, bbb, vb, wtb,
                    yf_ref, yb_ref, s_ref):
    i = pl.program_id(1)

    @pl.when(i == 0)
    def _():
        s_ref[...] = jnp.zeros_like(s_ref)

    kt, rt = _heads(ktf[0], ktb[0]), _heads(rtf[0], rtb[0])
    khat, bhat = _heads(khf[0], khb[0]), _heads(bhf[0], bhb[0])
    kbar, bbar = _heads(kbf[0], kbb[0]), _heads(bbf[0], bbb[0])
    v = _heads(vf[0], vb[0])
    wt = _heads(wtf[0, pl.ds(i % ncb, 1), :], wtb[0, pl.ds((nc - 1 - i) % ncb, 1), :])
    g, c, _ = kt.shape
    ri = lax.broadcasted_iota(jnp.int32, (g, c, c), 1)
    ci = lax.broadcasted_iota(jnp.int32, (g, c, c), 2)
    rev = lax.broadcasted_iota(jnp.int32, (g, c, c), 0) >= RW_HEADS
    ahead = jnp.where(rev, ri - ci, ci - ri)
    strict = ahead < 0
    causal = ahead <= 0

    lh = jnp.concatenate([kt, rt], axis=1)
    gk = _bmm_nt(lh, khat)
    gb = _bmm_nt(lh, bhat)
    a_k = jnp.where(strict, gk[:, :c], 0.0)
    b_k = jnp.where(causal, gk[:, c:], 0.0)
    a_b = jnp.where(strict, gb[:, :c], 0.0)
    b_b = jnp.where(causal, gb[:, c:], 0.0)
    tmat = _tri_inverse(a_b)
    s0 = s_ref[...]
    ks = _bmm_nt(lh, s0.astype(BF16))
    av = _bmm(jnp.concatenate([a_k, b_k], axis=1).astype(BF16), v)
    ub = _bmm(tmat.astype(BF16), (ks[:, :c] + av[:, :c]).astype(BF16)).astype(BF16)
    y = ks[:, c:] + av[:, c:] - _bmm(b_b.astype(BF16), ub)
    s_ref[...] = s0 * wt + _bmm_tn(v, kbar) - _bmm_tn(ub, bbar)
    for h in range(RW_HEADS):
        sl = slice(h * RW_HEAD, (h + 1) * RW_HEAD)
        yf_ref[0, :, sl] = y[h]
        yb_ref[0, :, sl] = y[RW_HEADS + h]


def _rw_scan(fwd_ops, bwd_ops, ncb):
    B, L, W = fwd_ops[0].shape
    c = min(RW_CHUNK, L)
    nc = L // c
    fwd = lambda: pl.BlockSpec((1, c, W), lambda b, i: (b, i, 0))
    bwd = lambda: pl.BlockSpec((1, c, W), lambda b, i: (b, nc - 1 - i, 0))
    wt_f = pl.BlockSpec((1, ncb, W), lambda b, i: (b, i // ncb, 0))
    wt_b = pl.BlockSpec((1, ncb, W), lambda b, i: (b, (nc - 1 - i) // ncb, 0))
    return pl.pallas_call(
        functools.partial(_rw_scan_kernel, nc, ncb),
        grid=(B, nc),
        in_specs=[fwd() for _ in range(7)] + [wt_f] + [bwd() for _ in range(7)] + [wt_b],
        out_specs=[fwd(), bwd()],
        out_shape=[jax.ShapeDtypeStruct((B, L, W), F32)] * 2,
        scratch_shapes=[pltpu.VMEM((2 * RW_HEADS, RW_HEAD, RW_HEAD), F32)],
        compiler_params=_cparams(("parallel", "arbitrary")),
        name="rw_scan",
    )(*fwd_ops, *bwd_ops)


def _merge_kernel(x_ref, yh_ref, yf_ref, yb_ref, bonus_ref, g_ref, gates_ref, p_ref, gnw_ref, gnb_ref,
                  why_ref, wrw_ref, wo_ref, lnw_ref, lnb_ref, o_ref):
    p = p_ref[...]
    y = yf_ref[...] + yb_ref[...]
    mu = _segsum(y, p) * (1.0 / RW_HEAD)
    yc = y - mu
    var = _segsum(yc * yc, p) * (1.0 / RW_HEAD)
    yn = yc * lax.rsqrt(var + RW_GN_EPS) * gnw_ref[...] + gnb_ref[...]
    y_r = (yn + bonus_ref[...]) * g_ref[...]
    ph = _dot(yh_ref[...].astype(BF16), why_ref[...])
    pr = _dot(y_r.astype(BF16), wrw_ref[...])
    gates = jax.nn.sigmoid(gates_ref[...])
    m = gates[:, :D_MODEL] * ph + gates[:, D_MODEL:] * pr
    mix = _dot(m.astype(BF16), wo_ref[...])
    o_ref[...] = _layer_norm(DN_ALPHA * x_ref[...] + mix, lnw_ref[...], lnb_ref[...])


def _merge(x2, yh, yf, yb, bonus, g, gates, p_ones, gn_w, gn_b, w_hy_out, w_rw_out, w_o, ln_w, ln_b, tm=512):
    m, d = x2.shape
    tm = min(tm, m)
    W = RW_WIDTH
    row = lambda width: pl.BlockSpec((tm, width), lambda i: (i, 0))
    full = lambda shape: pl.BlockSpec(shape, lambda i: tuple(0 for _ in shape))
    return pl.pallas_call(
        _merge_kernel,
        grid=(m // tm,),
        in_specs=[row(d), row(HY_WIDTH), row(W), row(W), row(W), row(W), row(GATE_COLS),
                  full((W, W)), full((1, W)), full((1, W)),
                  full((HY_WIDTH, d)), full((W, d)), full((d, d)), full((1, d)), full((1, d))],
        out_specs=row(d),
        out_shape=jax.ShapeDtypeStruct((m, d), F32),
        compiler_params=_cparams(("parallel",)),
        name="merge_ln1",
    )(x2, yh, yf, yb, bonus, g, gates, p_ones, gn_w.reshape(1, W), gn_b.reshape(1, W),
      w_hy_out.astype(BF16), w_rw_out.astype(BF16), w_o.astype(BF16), ln_w.reshape(1, d), ln_b.reshape(1, d))


def _ffn_kernel(x_ref, wg_ref, wu_ref, wd_ref, lnw_ref, lnb_ref, o_ref, xb_scr, acc_scr):
    j = pl.program_id(1)

    @pl.when(j == 0)
    def _():
        xb_scr[...] = x_ref[...].astype(BF16)

    xb = xb_scr[...]
    hidden = jax.nn.silu(_dot(xb, wg_ref[...])) * _dot(xb, wu_ref[...])
    part = _dot(hidden.astype(BF16), wd_ref[...])

    @pl.when(j == 0)
    def _():
        acc_scr[...] = part

    @pl.when(j > 0)
    def _():
        acc_scr[...] += part

    @pl.when(j == pl.num_programs(1) - 1)
    def _():
        o_ref[...] = _layer_norm(DN_ALPHA * x_ref[...] + acc_scr[...], lnw_ref[...], lnb_ref[...])


def _ffn(x2, w_gate, w_up, w_down, ln_w, ln_b, tm=1024, th=256):
    m, d = x2.shape
    tm = min(tm, m)
    fh = w_gate.shape[1]
    return pl.pallas_call(
        _ffn_kernel,
        grid=(m // tm, fh // th),
        in_specs=[pl.BlockSpec((tm, d), lambda i, j: (i, 0)),
                  pl.BlockSpec((d, th), lambda i, j: (0, j)),
                  pl.BlockSpec((d, th), lambda i, j: (0, j)),
                  pl.BlockSpec((th, d), lambda i, j: (j, 0)),
                  pl.BlockSpec((1, d), lambda i, j: (0, 0)),
                  pl.BlockSpec((1, d), lambda i, j: (0, 0))],
        out_specs=pl.BlockSpec((tm, d), lambda i, j: (i, 0)),
        out_shape=jax.ShapeDtypeStruct((m, d), F32),
        scratch_shapes=[pltpu.VMEM((tm, d), BF16), pltpu.VMEM((tm, d), F32)],
        compiler_params=_cparams(("parallel", "arbitrary")),
        name="ffn_ln2",
    )(x2, w_gate.astype(BF16), w_up.astype(BF16), w_down.astype(BF16), ln_w.reshape(1, d), ln_b.reshape(1, d))


def _layer(x, w_in, hy_conv_w, hy_conv_b, hy_filt_w1, hy_filt_b1, hy_filt_w2, hy_filt_b2,
           hy_filt_w3, hy_filt_b3, hy_filt_w4, hy_sin_freq, hy_skip, rw_mu, rw_w0, rw_w2,
           rw_a0, rw_a2, rw_g2, rw_k_k, rw_k_a, rw_r_k, rw_gn_w, rw_gn_b, w_hy_out, w_rw_out,
           w_o, ln1_w, ln1_b, ffn_w_gate, ffn_w_up, ffn_w_down, ln2_w, ln2_b):
    B, L, D = x.shape
    x2 = x.reshape(B * L, D)
    u_h, u_r, gates = _in_proj(x2, w_in.astype(BF16))
    y_h = _hyena(u_h.reshape(B, L, HY_COLS), hy_conv_w, hy_conv_b, hy_filt_w1, hy_filt_b1, hy_filt_w2,
                 hy_filt_b2, hy_filt_w3, hy_filt_b3, hy_filt_w4, hy_sin_freq, hy_skip)
    p_ones = _head_ones()
    fwd_ops, bwd_ops, g, bonus, ncb = _rw_prep(
        u_r.reshape(B, L, RW_COLS), rw_mu, rw_w0, rw_w2, rw_a0, rw_a2, rw_g2, rw_k_k, rw_k_a, rw_r_k, p_ones)
    yf, yb = _rw_scan(fwd_ops, bwd_ops, ncb)
    flat = lambda a: a.reshape(B * L, a.shape[-1])
    h = _merge(x2, flat(y_h), flat(yf), flat(yb), flat(bonus), flat(g), gates, p_ones, rw_gn_w, rw_gn_b,
               w_hy_out, w_rw_out, w_o, ln1_w, ln1_b)
    out = _ffn(h, ffn_w_gate, ffn_w_up, ffn_w_down, ln2_w, ln2_b)
    return out.reshape(B, L, D)


def kernel(x, w_in, hy_conv_w, hy_conv_b, hy_filt_w1, hy_filt_b1, hy_filt_w2, hy_filt_b2, hy_filt_w3, hy_filt_b3, hy_filt_w4, hy_sin_freq, hy_skip, rw_mu, rw_w0, rw_w2, rw_a0, rw_a2, rw_g2, rw_k_k, rw_k_a, rw_r_k, rw_gn_w, rw_gn_b, w_hy_out, w_rw_out, w_o, ln1_w, ln1_b, ffn_w_gate, ffn_w_up, ffn_w_down, ln2_w, ln2_b):
    params = (w_in, hy_conv_w, hy_conv_b, hy_filt_w1, hy_filt_b1, hy_filt_w2, hy_filt_b2, hy_filt_w3,
              hy_filt_b3, hy_filt_w4, hy_sin_freq, hy_skip, rw_mu, rw_w0, rw_w2, rw_a0, rw_a2, rw_g2,
              rw_k_k, rw_k_a, rw_r_k, rw_gn_w, rw_gn_b, w_hy_out, w_rw_out, w_o, ln1_w, ln1_b,
              ffn_w_gate, ffn_w_up, ffn_w_down, ln2_w, ln2_b)
    for l in range(w_in.shape[0]):
        x = _layer(x, *[p[l] for p in params])
    return x
```

```python
import functools
import math

import jax
import jax.numpy as jnp
from jax import lax
from jax.experimental import pallas as pl
from jax.experimental.pallas import tpu as pltpu

F32 = jnp.float32
BF16 = jnp.bfloat16
HIGHEST = lax.Precision.HIGHEST

D_MODEL = 1024
HY_WIDTH = 512
HY_ORDER = 2
HY_BANDS = 16
HY_FILT_HIDDEN = 64
HY_FAST_DECAY = 0.3
HY_SLOW_DECAY = 1.5
HY_DECAY_TARGET = 1e-2
HY_MAX_DECAY = math.log(HY_DECAY_TARGET) / HY_FAST_DECAY
HY_MIN_DECAY = math.log(HY_DECAY_TARGET) / HY_SLOW_DECAY
HY_COLS = 3 * HY_WIDTH
RW_WIDTH = 512
RW_HEAD = 64
RW_HEADS = RW_WIDTH // RW_HEAD
RW_LORA = 64
RW_LORA_G = 128
RW_GN_EPS = 64e-5
RW_COLS = 3 * RW_WIDTH + 4 * RW_LORA + RW_LORA_G
GATE_COLS = 2 * D_MODEL
FFN_HIDDEN = ((8 * D_MODEL + 3 * 256 - 1) // (3 * 256)) * 256
DEPTH = 1
DN_ALPHA = (2.0 * DEPTH) ** 0.25
LN_EPS = 1e-5
RW_DECAY_SCALE = math.exp(-0.5)

LANES = 128
VMEM_LIMIT = 56 * 1024 * 1024

FFT_N1 = 32
FFT_F1 = FFT_N1 // 2 + 1
RW_CHUNK = 64


def _cparams(sem, vmem=VMEM_LIMIT):
    return pltpu.CompilerParams(dimension_semantics=sem, vmem_limit_bytes=vmem)


def _dot(a, b, precision=None):
    return jnp.dot(a, b, preferred_element_type=F32, precision=precision)


def _dot_nt(a, b):
    return lax.dot_general(a, b, (((1,), (1,)), ((), ())), preferred_element_type=F32)


def _dot_tn(a, b):
    return lax.dot_general(a, b, (((0,), (0,)), ((), ())), preferred_element_type=F32)


def _layer_norm(h, w, b):
    mu = jnp.mean(h, axis=-1, keepdims=True)
    c = h - mu
    var = jnp.mean(c * c, axis=-1, keepdims=True)
    return c * lax.rsqrt(var + LN_EPS) * w + b


def _segsum(x, p):
    hi = x.astype(BF16)
    lo = (x - hi.astype(F32)).astype(BF16)
    return _dot(hi, p) + _dot(lo, p)


def _split3(x):
    h1 = x.astype(BF16)
    r1 = x - h1.astype(F32)
    h2 = r1.astype(BF16)
    h3 = (r1 - h2.astype(F32)).astype(BF16)
    return h1, h2, h3


def _dot_parts(m, parts):
    out = _dot(m, parts[0])
    for part in parts[1:]:
        out = out + _dot(m, part)
    return out


def _in_proj_kernel(x_ref, w_ref, uh_ref, ur_ref, g_ref):
    xb = x_ref[...].astype(BF16)
    uh_ref[...] = _dot(xb, w_ref[:, :HY_COLS])
    ur_ref[...] = _dot(xb, w_ref[:, HY_COLS:HY_COLS + RW_COLS])
    g_ref[...] = _dot(xb, w_ref[:, HY_COLS + RW_COLS:])


def _in_proj(x2, w_in_bf, tm=256):
    m, d = x2.shape
    n = w_in_bf.shape[1]
    return pl.pallas_call(
        _in_proj_kernel,
        grid=(m // tm,),
        in_specs=[pl.BlockSpec((tm, d), lambda i: (i, 0)),
                  pl.BlockSpec((d, n), lambda i: (0, 0))],
        out_specs=[pl.BlockSpec((tm, HY_COLS), lambda i: (i, 0)),
                   pl.BlockSpec((tm, RW_COLS), lambda i: (i, 0)),
                   pl.BlockSpec((tm, GATE_COLS), lambda i: (i, 0))],
        out_shape=[jax.ShapeDtypeStruct((m, HY_COLS), F32),
                   jax.ShapeDtypeStruct((m, RW_COLS), F32),
                   jax.ShapeDtypeStruct((m, GATE_COLS), F32)],
        compiler_params=_cparams(("parallel",)),
        name="in_proj",
    )(x2, w_in_bf)


def _shortconv_kernel(u_ref, w_ref, b_ref, o_ref):
    u = u_ref[0]
    L = u.shape[0]
    row = lax.broadcasted_iota(jnp.int32, (L, 1), 0)
    prev = jnp.where(row == 0, 0.0, pltpu.roll(u, 1, 0))
    nxt = jnp.where(row == L - 1, 0.0, pltpu.roll(u, L - 1, 0))
    o_ref[0, 0] = w_ref[0:1, :] * prev + w_ref[1:2, :] * u + w_ref[2:3, :] * nxt + b_ref[...]


def _shortconv(u_h, conv_w, conv_b):
    B, L, C = u_h.shape
    nct = HY_WIDTH // LANES
    return pl.pallas_call(
        _shortconv_kernel,
        grid=(B, C // LANES),
        in_specs=[pl.BlockSpec((1, L, LANES), lambda b, j: (b, 0, j)),
                  pl.BlockSpec((3, LANES), lambda b, j: (0, j)),
                  pl.BlockSpec((1, LANES), lambda b, j: (0, j))],
        out_specs=pl.BlockSpec((1, 1, L, LANES), lambda b, j: (j // nct, b, 0, j % nct)),
        out_shape=jax.ShapeDtypeStruct((C // HY_WIDTH, B, L, HY_WIDTH), F32),
        compiler_params=_cparams(("parallel", "parallel")),
        name="hy_shortconv",
    )(u_h, conv_w, conv_b.reshape(1, C))


def _filter_kernel(L, tb, w1t_ref, w1c_ref, w1s_ref, b1_ref, w2_ref, b2_ref, w3_ref, b3_ref,
                   w4_ref, sf_ref, freq_ref, delta_ref, kc_ref):
    n = 2 * L
    i = pl.program_id(0)
    row = i * tb + lax.broadcasted_iota(jnp.int32, (tb, 1), 0)
    second = row >= L
    pos = jnp.where(second, n - row, row).astype(F32)
    t = pos / float(L - 1)
    ang = (2.0 * math.pi * pos / float(L)) * freq_ref[...]
    pre = t * w1t_ref[...] + _dot(jnp.cos(ang), w1c_ref[...], HIGHEST) \
        - _dot(jnp.sin(ang), w1s_ref[...], HIGHEST) + b1_ref[...]
    h = jnp.sin(sf_ref[0:1, :] * pre)
    h = jnp.sin(sf_ref[1:2, :] * (_dot(h, w2_ref[...], HIGHEST) + b2_ref[...]))
    h = jnp.sin(sf_ref[2:3, :] * (_dot(h, w3_ref[...], HIGHEST) + b3_ref[...]))
    h4 = _dot(h, w4_ref[...], HIGHEST)
    window = jnp.exp(-t * delta_ref[...])
    keep = jnp.where(row == L, 0.0, 1.0)
    first = jnp.where(row == 0, 1.0, 0.0)
    for o in range(HY_ORDER):
        base = o * 2 * HY_WIDTH
        fwd = h4[:, base:base + HY_WIDTH]
        bwd = h4[:, base + HY_WIDTH:base + 2 * HY_WIDTH]
        kc_ref[o] = window * (jnp.where(second, bwd, fwd) * keep + first * bwd)


def _filters(L, fw1, fb1, fw2, fb2, fw3, fb3, fw4, sin_freq, tb=512):
    n = 2 * L
    tb = min(tb, n)
    freqs = jnp.linspace(1e-4, HY_BANDS - 1, HY_BANDS, dtype=F32).reshape(1, HY_BANDS)
    deltas = jnp.abs(jnp.linspace(HY_MIN_DECAY, HY_MAX_DECAY, HY_WIDTH, dtype=F32)).reshape(1, HY_WIDTH)
    hid = HY_FILT_HIDDEN
    full = lambda shape: pl.BlockSpec(shape, lambda i: tuple(0 for _ in shape))
    return pl.pallas_call(
        functools.partial(_filter_kernel, L, tb),
        grid=(n // tb,),
        in_specs=[full((1, hid)), full((HY_BANDS, hid)), full((HY_BANDS, hid)), full((1, hid)),
                  full((hid, hid)), full((1, hid)), full((hid, hid)), full((1, hid)),
                  full((hid, HY_ORDER * 2 * HY_WIDTH)), full((3, hid)),
                  full((1, HY_BANDS)), full((1, HY_WIDTH))],
        out_specs=pl.BlockSpec((HY_ORDER, tb, HY_WIDTH), lambda i: (0, i, 0)),
        out_shape=jax.ShapeDtypeStruct((HY_ORDER, n, HY_WIDTH), F32),
        compiler_params=_cparams(("parallel",)),
        name="hy_filters",
    )(fw1[0:1], fw1[1:1 + HY_BANDS], fw1[1 + HY_BANDS:], fb1.reshape(1, hid), fw2, fb2.reshape(1, hid),
      fw3, fb3.reshape(1, hid), fw4, sin_freq, freqs, deltas)


def _fft_tables(n2):
    n = FFT_N1 * n2
    f1 = jnp.arange(FFT_F1, dtype=jnp.int32)[:, None, None]
    a = jnp.arange(n2, dtype=jnp.int32)[None, :, None]
    b = jnp.arange(n2, dtype=jnp.int32)[None, None, :]
    ph_f = ((b * (f1 + FFT_N1 * a)) % n).astype(F32) * (2.0 * math.pi / n)
    cr, ci = jnp.cos(ph_f), -jnp.sin(ph_f)
    m_fwd = jnp.concatenate([jnp.concatenate([cr, -ci], axis=2),
                             jnp.concatenate([ci, cr], axis=2)], axis=1).astype(BF16)
    ph_i = ((a * (f1 + FFT_N1 * b)) % n).astype(F32) * (2.0 * math.pi / n)
    herm = jnp.where((f1 == 0) | (f1 == FFT_N1 // 2), 1.0, 2.0) / n
    dr, di = herm * jnp.cos(ph_i), herm * jnp.sin(ph_i)
    m_inv = jnp.concatenate([jnp.concatenate([dr, -di], axis=2),
                             jnp.concatenate([di, dr], axis=2)], axis=1).astype(BF16)
    return m_fwd, m_inv


def _hi_lo(x):
    hi = x.astype(BF16)
    return hi, (x - hi.astype(F32)).astype(BF16)


def _slow_tables():
    k = (jnp.arange(FFT_F1, dtype=jnp.int32)[:, None] * jnp.arange(FFT_N1, dtype=jnp.int32)[None, :]) % FFT_N1
    ph = k.astype(F32) * (2.0 * math.pi / FFT_N1)
    fwd = jnp.stack([jnp.cos(ph), -jnp.sin(ph)], axis=1).reshape(2 * FFT_F1, FFT_N1)
    inv = fwd[:, :FFT_N1 // 2].T
    return fwd, inv


def _dot3(a_hi, a_lo, x):
    x_hi, x_lo = _hi_lo(x)
    return _dot(a_hi, x_hi) + _dot(a_lo, x_hi) + _dot(a_hi, x_lo)


def _slow_dft_kernel(fh_ref, fl_ref, z_ref, a_ref):
    a_ref[0, 0] = _dot3(fh_ref[...], fl_ref[...], z_ref[0, 0]).astype(BF16)


def _lane_block(total, target=16384):
    return min(total, target)


def _slow_dft(z4, sel, s1_count):
    _, B, T, W = z4.shape
    lanes = (T // s1_count) * W
    nb = _lane_block(lanes)
    fwd, _ = _slow_tables()
    fh, fl = _hi_lo(fwd[:, :s1_count])
    rows = 2 * FFT_F1
    return pl.pallas_call(
        _slow_dft_kernel,
        grid=(B, lanes // nb),
        in_specs=[pl.BlockSpec((rows, s1_count), lambda b, j: (0, 0)),
                  pl.BlockSpec((rows, s1_count), lambda b, j: (0, 0)),
                  pl.BlockSpec((1, 1, s1_count, nb), lambda b, j: (sel, b, 0, j))],
        out_specs=pl.BlockSpec((1, 1, rows, nb), lambda b, j: (0, b, 0, j)),
        out_shape=jax.ShapeDtypeStruct((1, B, rows, lanes), BF16),
        compiler_params=_cparams(("parallel", "parallel")),
        name="hy_slow_dft",
    )(fh, fl, z4.reshape(z4.shape[0], B, s1_count, lanes))


def _spectrum_kernel(n2, a_ref, mf_ref, hr_ref, hi_ref):
    x = _dot(mf_ref[0], a_ref[0, 0])
    hr_ref[0] = x[:n2]
    hi_ref[0] = x[n2:]


def _spectrum(kc, m_fwd):
    order, n, C = kc.shape
    n2 = n // FFT_N1
    a = _slow_dft(kc.reshape(1, order, n, C), 0, FFT_N1).reshape(order, FFT_F1, 2 * n2, C)
    spec = lambda: pl.BlockSpec((1, n2, C), lambda f, o: (o, f, 0))
    return pl.pallas_call(
        functools.partial(_spectrum_kernel, n2),
        grid=(FFT_F1, order),
        in_specs=[pl.BlockSpec((1, 1, 2 * n2, C), lambda f, o: (o, f, 0, 0)),
                  pl.BlockSpec((1, 2 * n2, 2 * n2), lambda f, o: (f, 0, 0))],
        out_specs=[spec(), spec()],
        out_shape=[jax.ShapeDtypeStruct((order, FFT_F1 * n2, C), F32)] * 2,
        compiler_params=_cparams(("parallel", "parallel")),
        name="hy_spectrum",
    )(a, m_fwd)


def _conv_mid_kernel(n2, a_ref, hr_ref, hi_ref, mf_ref, mi_ref, b_ref):
    x = _dot(mf_ref[0], a_ref[0, 0])
    xr, xi = x[:n2], x[n2:]
    hr, hi = hr_ref[0], hi_ref[0]
    y = jnp.concatenate([xr * hr - xi * hi, xr * hi + xi * hr], axis=0).astype(BF16)
    b_ref[0, 0] = _dot(mi_ref[0], y)


def _conv_mid(a, order, hr, hi, m_fwd, m_inv):
    B, _, rows, W = a.shape
    n2 = rows // 2
    blk = lambda: pl.BlockSpec((1, 1, rows, W), lambda f, b: (b, f, 0, 0))
    return pl.pallas_call(
        functools.partial(_conv_mid_kernel, n2),
        grid=(FFT_F1, B),
        in_specs=[blk(),
                  pl.BlockSpec((1, n2, W), lambda f, b: (order, f, 0)),
                  pl.BlockSpec((1, n2, W), lambda f, b: (order, f, 0)),
                  pl.BlockSpec((1, rows, rows), lambda f, b: (f, 0, 0)),
                  pl.BlockSpec((1, rows, rows), lambda f, b: (f, 0, 0))],
        out_specs=blk(),
        out_shape=jax.ShapeDtypeStruct((B, FFT_F1, rows, W), F32),
        compiler_params=_cparams(("parallel", "parallel")),
        name=f"hy_conv_mid{order}",
    )(a, hr, hi, m_fwd, m_inv)


def _slow_idft_kernel(gh_ref, gl_ref, b_ref, z_ref, gate_ref, skip_ref, o_ref):
    y = _dot3(gh_ref[...], gl_ref[...], b_ref[0])
    o_ref[0, 0] = gate_ref[0, 0] * (y + skip_ref[...] * z_ref[0, 0])


def _slow_idft(bm, z4, zsel, g4, gsel, skip_row):
    B, rows, lanes = bm.shape
    W = z4.shape[-1]
    t1 = FFT_N1 // 2
    nb = _lane_block(lanes)
    _, inv = _slow_tables()
    gh, gl = _hi_lo(inv)
    flat = lambda a: a.reshape(a.shape[0], B, t1, lanes)
    tok = lambda sel: pl.BlockSpec((1, 1, t1, nb), lambda b, j: (sel, b, 0, j))
    out = pl.pallas_call(
        _slow_idft_kernel,
        grid=(B, lanes // nb),
        in_specs=[pl.BlockSpec((t1, rows), lambda b, j: (0, 0)),
                  pl.BlockSpec((t1, rows), lambda b, j: (0, 0)),
                  pl.BlockSpec((1, rows, nb), lambda b, j: (b, 0, j)),
                  tok(zsel), tok(gsel),
                  pl.BlockSpec((1, nb), lambda b, j: (0, 0))],
        out_specs=tok(0),
        out_shape=jax.ShapeDtypeStruct((1, B, t1, lanes), F32),
        compiler_params=_cparams(("parallel", "parallel")),
        name="hy_slow_idft",
    )(gh, gl, bm, flat(z4), flat(g4), jnp.tile(skip_row, nb // W).reshape(1, nb))
    return out.reshape(1, B, z4.shape[2], W)


def _longconv(z4, zsel, g4, gsel, order, skip, hr, hi, m_fwd, m_inv):
    _, B, L, W = z4.shape
    n2 = 2 * L // FFT_N1
    a = _slow_dft(z4, zsel, FFT_N1 // 2).reshape(B, FFT_F1, 2 * n2, W)
    bm = _conv_mid(a, order, hr, hi, m_fwd, m_inv).reshape(B, 2 * FFT_F1, n2 * W)
    return _slow_idft(bm, z4, zsel, g4, gsel, skip[order])


def _hyena(u_h, conv_w, conv_b, fw1, fb1, fw2, fb2, fw3, fb3, fw4, sin_freq, skip):
    B, L, _ = u_h.shape
    u3 = _shortconv(u_h, conv_w, conv_b)
    kc = _filters(L, fw1, fb1, fw2, fb2, fw3, fb3, fw4, sin_freq)
    m_fwd, m_inv = _fft_tables(2 * L // FFT_N1)
    hr, hi = _spectrum(kc, m_fwd)
    z1 = _longconv(u3, 0, u3, 1, 0, skip, hr, hi, m_fwd, m_inv)
    out = _longconv(z1, 0, u3, 2, 1, skip, hr, hi, m_fwd, m_inv)
    return out.reshape(B, L, HY_WIDTH)


def _rw_prep_kernel(u_ref, up_ref, un_ref, mu_ref, w0_ref, w2f_ref, w2b_ref, a0_ref, a2f_ref, a2b_ref,
                    g2_ref, kk_ref, ka_ref, rk_ref, p_ref, trif_ref, trib_ref, ones_ref, sel_ref,
                    v_o, ktf_o, rtf_o, khf_o, bhf_o, kbf_o, bbf_o, wtf_o,
                    ktb_o, rtb_o, khb_o, bhb_o, kbb_o, bbb_o, wtb_o, g_o, bonus_o):
    j = pl.program_id(1)
    nj = pl.num_programs(1)
    u = u_ref[0]
    tb = u.shape[0]
    prow = jnp.where(j == 0, 0.0, up_ref[0, 7:8, :])
    nrow = jnp.where(j == nj - 1, 0.0, un_ref[0, 0:1, :])
    row = lax.broadcasted_iota(jnp.int32, (tb, 1), 0)
    prev = jnp.where(row == 0, prow, pltpu.roll(u, 1, 0))
    nxt = jnp.where(row == tb - 1, nrow, pltpu.roll(u, tb - 1, 0))
    xs = u + mu_ref[...] * (0.5 * (prev + nxt) - u)
    W = RW_WIDTH
    r, k, v = xs[:, 0:W], xs[:, W:2 * W], xs[:, 2 * W:3 * W]
    wd = jnp.tanh(xs[:, 3 * W:3 * W + 2 * RW_LORA]).astype(BF16)
    ad = xs[:, 3 * W + 2 * RW_LORA:3 * W + 4 * RW_LORA].astype(BF16)
    gd = jax.nn.sigmoid(xs[:, 3 * W + 4 * RW_LORA:]).astype(BF16)
    p = p_ref[...]
    kkn = k * kk_ref[...]
    nrm = jnp.sqrt(_segsum(kkn * kkn, p))
    kk = kkn / jnp.maximum(nrm, 1e-12)
    ka = ka_ref[...]
    lw_f = -RW_DECAY_SCALE * jax.nn.sigmoid(w0_ref[0:1, :] + _dot(wd, w2f_ref[...]))
    lw_b = -RW_DECAY_SCALE * jax.nn.sigmoid(w0_ref[1:2, :] + _dot(wd, w2b_ref[...]))
    a_f = jax.nn.sigmoid(a0_ref[0:1, :] + _dot(ad, a2f_ref[...]))
    a_b = jax.nn.sigmoid(a0_ref[1:2, :] + _dot(ad, a2b_ref[...]))
    kd_f = k * (1.0 + (a_f - 1.0) * ka)
    kd_b = k * (1.0 + (a_b - 1.0) * ka)
    v_o[0] = v.astype(BF16)
    ones = ones_ref[...]
    sel = sel_ref[...]

    def scan_operands(lw, kd, b, tri, outs):
        parts = _split3(lw)
        cum = _dot_parts(tri, parts)
        tot = _dot_parts(ones, parts)
        e_neg = jnp.exp(-cum)
        e_rem = jnp.exp(tot - cum)
        kt_o, rt_o, kh_o, bh_o, kb_o, bb_o, wt_o = outs
        kt_o[0] = (kk * jnp.exp(cum - lw)).astype(BF16)
        rt_o[0] = (r * jnp.exp(cum)).astype(BF16)
        kh_o[0] = (kd * e_neg).astype(BF16)
        bh_o[0] = (b * e_neg).astype(BF16)
        kb_o[0] = (kd * e_rem).astype(BF16)
        bb_o[0] = (b * e_rem).astype(BF16)
        wt_o[0] = jnp.exp(_dot_parts(sel, parts))

    scan_operands(lw_f, kd_f, kk * a_f, trif_ref[...], (ktf_o, rtf_o, khf_o, bhf_o, kbf_o, bbf_o, wtf_o))
    scan_operands(lw_b, kd_b, kk * a_b, trib_ref[...], (ktb_o, rtb_o, khb_o, bhb_o, kbb_o, bbb_o, wtb_o))
    g_o[0] = _dot(gd, g2_ref[...])
    bonus_o[0] = _segsum(r * (kd_f + kd_b) * rk_ref[...], p) * v


def _head_ones():
    h = jnp.arange(RW_WIDTH, dtype=jnp.int32) // RW_HEAD
    return (h[:, None] == h[None, :]).astype(BF16)


def _chunk_matrices(tb, c):
    t = jnp.arange(tb, dtype=jnp.int32)
    same = (t[:, None] // c) == (t[None, :] // c)
    tri_f = (same & (t[None, :] <= t[:, None])).astype(BF16)
    tri_b = (same & (t[None, :] >= t[:, None])).astype(BF16)
    sel = (jnp.arange(tb // c, dtype=jnp.int32)[:, None] == (t[None, :] // c)).astype(BF16)
    return tri_f, tri_b, same.astype(BF16), sel


def _rw_prep(u_r, mu, w0, w2, a0, a2, g2, k_k, k_a, r_k, p_ones, tb=512):
    B, L, C = u_r.shape
    tb = min(tb, L)
    W = RW_WIDTH
    c = min(RW_CHUNK, L)
    ncb = tb // c
    tri_f, tri_b, ones, sel = _chunk_matrices(tb, c)
    zeros = jnp.zeros((RW_LORA, W), F32)
    w2f = jnp.concatenate([w2[0], zeros], axis=0).astype(BF16)
    w2b = jnp.concatenate([zeros, w2[1]], axis=0).astype(BF16)
    a2f = jnp.concatenate([a2[0], zeros], axis=0).astype(BF16)
    a2b = jnp.concatenate([zeros, a2[1]], axis=0).astype(BF16)
    full = lambda shape: pl.BlockSpec(shape, lambda b, j: tuple(0 for _ in shape))
    tok = lambda: pl.BlockSpec((1, tb, W), lambda b, j: (b, j, 0))
    wts = lambda: pl.BlockSpec((1, ncb, W), lambda b, j: (b, j, 0))
    bf_tok = jax.ShapeDtypeStruct((B, L, W), BF16)
    f32_tok = jax.ShapeDtypeStruct((B, L, W), F32)
    wt_shape = jax.ShapeDtypeStruct((B, L // c, W), F32)
    g8 = tb // 8
    outs = pl.pallas_call(
        _rw_prep_kernel,
        grid=(B, L // tb),
        in_specs=[pl.BlockSpec((1, tb, C), lambda b, j: (b, j, 0)),
                  pl.BlockSpec((1, 8, C), lambda b, j: (b, jnp.maximum(j * g8 - 1, 0), 0)),
                  pl.BlockSpec((1, 8, C), lambda b, j: (b, jnp.minimum((j + 1) * g8, L // 8 - 1), 0)),
                  full((1, C)), full((2, W)), full((2 * RW_LORA, W)), full((2 * RW_LORA, W)),
                  full((2, W)), full((2 * RW_LORA, W)), full((2 * RW_LORA, W)),
                  full((RW_LORA_G, W)), full((1, W)), full((1, W)), full((1, W)), full((W, W)),
                  full((tb, tb)), full((tb, tb)), full((tb, tb)), full((ncb, tb))],
        out_specs=[tok()] + ([tok() for _ in range(6)] + [wts()]) * 2 + [tok(), tok()],
        out_shape=[bf_tok] + ([bf_tok] * 6 + [wt_shape]) * 2 + [f32_tok, f32_tok],
        compiler_params=_cparams(("parallel", "parallel")),
        name="rw_prep",
    )(u_r, u_r, u_r, mu.reshape(1, C), w0, w2f, w2b, a0, a2f, a2b, g2.astype(BF16),
      k_k.reshape(1, W), k_a.reshape(1, W), r_k.reshape(1, W), p_ones, tri_f, tri_b, ones, sel)
    v, g, bonus = outs[0], outs[15], outs[16]
    fwd_ops = tuple(outs[1:7]) + (v, outs[7])
    bwd_ops = tuple(outs[8:14]) + (v, outs[14])
    return fwd_ops, bwd_ops, g, bonus, ncb


def _bmm(a, b):
    return lax.dot_general(a, b, (((2,), (1,)), ((0,), (0,))), preferred_element_type=F32)


def _bmm_nt(a, b):
    return lax.dot_general(a, b, (((2,), (2,)), ((0,), (0,))), preferred_element_type=F32)


def _bmm_tn(a, b):
    return lax.dot_general(a, b, (((1,), (1,)), ((0,), (0,))), preferred_element_type=F32)


def _tri_inverse(a_b):
    g, c, _ = a_b.shape
    eye = (lax.broadcasted_iota(jnp.int32, (1, c, c), 1) == lax.broadcasted_iota(jnp.int32, (1, c, c), 2)).astype(F32)
    nmat = -a_b
    x = eye + nmat
    p = nmat
    for _ in range(1, int(math.log2(c))):
        pb = p.astype(BF16)
        p = _bmm(pb, pb)
        x = x + _bmm(x.astype(BF16), p.astype(BF16))
    return x


def _heads(x_f, x_b):
    sl = lambda h: slice(h * RW_HEAD, (h + 1) * RW_HEAD)
    return jnp.stack([x_f[:, sl(h)] for h in range(RW_HEADS)] + [x_b[:, sl(h)] for h in range(RW_HEADS)], axis=0)


def _rw_scan_kernel(nc, ncb, ktf, rtf, khf, bhf, kbf, bbf, vf, wtf, ktb, rtb, khb, bhb, kbb, bbb, vb, wtb,
                    yf_ref, yb_ref, s_ref):
    i = pl.program_id(1)

    @pl.when(i == 0)
    def _():
        s_ref[...] = jnp.zeros_like(s_ref)

    kt, rt = _heads(ktf[0], ktb[0]), _heads(rtf[0], rtb[0])
    khat, bhat = _heads(khf[0], khb[0]), _heads(bhf[0], bhb[0])
    kbar, bbar = _heads(kbf[0], kbb[0]), _heads(bbf[0], bbb[0])
    v = _heads(vf[0], vb[0])
    wt = _heads(wtf[0, pl.ds(i % ncb, 1), :], wtb[0, pl.ds((nc - 1 - i) % ncb, 1), :])
    g, c, _ = kt.shape
    ri = lax.broadcasted_iota(jnp.int32, (g, c, c), 1)
    ci = lax.broadcasted_iota(jnp.int32, (g, c, c), 2)
    rev = lax.broadcasted_iota(jnp.int32, (g, c, c), 0) >= RW_HEADS
    ahead = jnp.where(rev, ri - ci, ci - ri)
    strict = ahead < 0
    causal = ahead <= 0

    lh = jnp.concatenate([kt, rt], axis=1)
    gk = _bmm_nt(lh, khat)
    gb = _bmm_nt(lh, bhat)
    a_k = jnp.where(strict, gk[:, :c], 0.0)
    b_k = jnp.where(causal, gk[:, c:], 0.0)
    a_b = jnp.where(strict, gb[:, :c], 0.0)
    b_b = jnp.where(causal, gb[:, c:], 0.0)
    tmat = _tri_inverse(a_b)
    s0 = s_ref[...]
    ks = _bmm_nt(lh, s0.astype(BF16))
    av = _bmm(jnp.concatenate([a_k, b_k], axis=1).astype(BF16), v)
    ub = _bmm(tmat.astype(BF16), (ks[:, :c] + av[:, :c]).astype(BF16)).astype(BF16)
    y = ks[:, c:] + av[:, c:] - _bmm(b_b.astype(BF16), ub)
    s_ref[...] = s0 * wt + _bmm_tn(v, kbar) - _bmm_tn(ub, bbar)
    for h in range(RW_HEADS):
        sl = slice(h * RW_HEAD, (h + 1) * RW_HEAD)
        yf_ref[0, :, sl] = y[h]
        yb_ref[0, :, sl] = y[RW_HEADS + h]


def _rw_scan(fwd_ops, bwd_ops, ncb):
    B, L, W = fwd_ops[0].shape
    c = min(RW_CHUNK, L)
    nc = L // c
    fwd = lambda: pl.BlockSpec((1, c, W), lambda b, i: (b, i, 0))
    bwd = lambda: pl.BlockSpec((1, c, W), lambda b, i: (b, nc - 1 - i, 0))
    wt_f = pl.BlockSpec((1, ncb, W), lambda b, i: (b, i // ncb, 0))
    wt_b = pl.BlockSpec((1, ncb, W), lambda b, i: (b, (nc - 1 - i) // ncb, 0))
    return pl.pallas_call(
        functools.partial(_rw_scan_kernel, nc, ncb),
        grid=(B, nc),
        in_specs=[fwd() for _ in range(7)] + [wt_f] + [bwd() for _ in range(7)] + [wt_b],
        out_specs=[fwd(), bwd()],
        out_shape=[jax.ShapeDtypeStruct((B, L, W), F32)] * 2,
        scratch_shapes=[pltpu.VMEM((2 * RW_HEADS, RW_HEAD, RW_HEAD), F32)],
        compiler_params=_cparams(("parallel", "arbitrary")),
        name="rw_scan",
    )(*fwd_ops, *bwd_ops)


def _merge_kernel(x_ref, yh_ref, yf_ref, yb_ref, bonus_ref, g_ref, gates_ref, p_ref, gnw_ref, gnb_ref,
                  why_ref, wrw_ref, wo_ref, lnw_ref, lnb_ref, o_ref):
    p = p_ref[...]
    y = yf_ref[...] + yb_ref[...]
    mu = _segsum(y, p) * (1.0 / RW_HEAD)
    yc = y - mu
    var = _segsum(yc * yc, p) * (1.0 / RW_HEAD)
    yn = yc * lax.rsqrt(var + RW_GN_EPS) * gnw_ref[...] + gnb_ref[...]
    y_r = (yn + bonus_ref[...]) * g_ref[...]
    ph = _dot(yh_ref[...].astype(BF16), why_ref[...])
    pr = _dot(y_r.astype(BF16), wrw_ref[...])
    gates = jax.nn.sigmoid(gates_ref[...])
    m = gates[:, :D_MODEL] * ph + gates[:, D_MODEL:] * pr
    mix = _dot(m.astype(BF16), wo_ref[...])
    o_ref[...] = _layer_norm(DN_ALPHA * x_ref[...] + mix, lnw_ref[...], lnb_ref[...])


def _merge(x2, yh, yf, yb, bonus, g, gates, p_ones, gn_w, gn_b, w_hy_out, w_rw_out, w_o, ln_w, ln_b, tm=512):
    m, d = x2.shape
    tm = min(tm, m)
    W = RW_WIDTH
    row = lambda width: pl.BlockSpec((tm, width), lambda i: (i, 0))
    full = lambda shape: pl.BlockSpec(shape, lambda i: tuple(0 for _ in shape))
    return pl.pallas_call(
        _merge_kernel,
        grid=(m // tm,),
        in_specs=[row(d), row(HY_WIDTH), row(W), row(W), row(W), row(W), row(GATE_COLS),
                  full((W, W)), full((1, W)), full((1, W)),
                  full((HY_WIDTH, d)), full((W, d)), full((d, d)), full((1, d)), full((1, d))],
        out_specs=row(d),
        out_shape=jax.ShapeDtypeStruct((m, d), F32),
        compiler_params=_cparams(("parallel",)),
        name="merge_ln1",
    )(x2, yh, yf, yb, bonus, g, gates, p_ones, gn_w.reshape(1, W), gn_b.reshape(1, W),
      w_hy_out.astype(BF16), w_rw_out.astype(BF16), w_o.astype(BF16), ln_w.reshape(1, d), ln_b.reshape(1, d))


def _ffn_kernel(x_ref, wg_ref, wu_ref, wd_ref, lnw_ref, lnb_ref, o_ref, xb_scr, acc_scr):
    j = pl.program_id(1)

    @pl.when(j == 0)
    def _():
        xb_scr[...] = x_ref[...].astype(BF16)

    xb = xb_scr[...]
    hidden = jax.nn.silu(_dot(xb, wg_ref[...])) * _dot(xb, wu_ref[...])
    part = _dot(hidden.astype(BF16), wd_ref[...])

    @pl.when(j == 0)
    def _():
        acc_scr[...] = part

    @pl.when(j > 0)
    def _():
        acc_scr[...] += part

    @pl.when(j == pl.num_programs(1) - 1)
    def _():
        o_ref[...] = _layer_norm(DN_ALPHA * x_ref[...] + acc_scr[...], lnw_ref[...], lnb_ref[...])


def _ffn(x2, w_gate, w_up, w_down, ln_w, ln_b, tm=1024, th=256):
    m, d = x2.shape
    tm = min(tm, m)
    fh = w_gate.shape[1]
    return pl.pallas_call(
        _ffn_kernel,
        grid=(m // tm, fh // th),
        in_specs=[pl.BlockSpec((tm, d), lambda i, j: (i, 0)),
                  pl.BlockSpec((d, th), lambda i, j: (0, j)),
                  pl.BlockSpec((d, th), lambda i, j: (0, j)),
                  pl.BlockSpec((th, d), lambda i, j: (j, 0)),
                  pl.BlockSpec((1, d), lambda i, j: (0, 0)),
                  pl.BlockSpec((1, d), lambda i, j: (0, 0))],
        out_specs=pl.BlockSpec((tm, d), lambda i, j: (i, 0)),
        out_shape=jax.ShapeDtypeStruct((m, d), F32),
        scratch_shapes=[pltpu.VMEM((tm, d), BF16), pltpu.VMEM((tm, d), F32)],
        compiler_params=_cparams(("parallel", "arbitrary")),
        name="ffn_ln2",
    )(x2, w_gate.astype(BF16), w_up.astype(BF16), w_down.astype(BF16), ln_w.reshape(1, d), ln_b.reshape(1, d))


def _layer(x, w_in, hy_conv_w, hy_conv_b, hy_filt_w1, hy_filt_b1, hy_filt_w2, hy_filt_b2,
           hy_filt_w3, hy_filt_b3, hy_filt_w4, hy_sin_freq, hy_skip, rw_mu, rw_w0, rw_w2,
           rw_a0, rw_a2, rw_g2, rw_k_k, rw_k_a, rw_r_k, rw_gn_w, rw_gn_b, w_hy_out, w_rw_out,
           w_o, ln1_w, ln1_b, ffn_w_gate, ffn_w_up, ffn_w_down, ln2_w, ln2_b):
    B, L, D = x.shape
    x2 = x.reshape(B * L, D)
    u_h, u_r, gates = _in_proj(x2, w_in.astype(BF16))
    y_h = _hyena(u_h.reshape(B, L, HY_COLS), hy_conv_w, hy_conv_b, hy_filt_w1, hy_filt_b1, hy_filt_w2,
                 hy_filt_b2, hy_filt_w3, hy_filt_b3, hy_filt_w4, hy_sin_freq, hy_skip)
    p_ones = _head_ones()
    fwd_ops, bwd_ops, g, bonus, ncb = _rw_prep(
        u_r.reshape(B, L, RW_COLS), rw_mu, rw_w0, rw_w2, rw_a0, rw_a2, rw_g2, rw_k_k, rw_k_a, rw_r_k, p_ones)
    yf, yb = _rw_scan(fwd_ops, bwd_ops, ncb)
    flat = lambda a: a.reshape(B * L, a.shape[-1])
    h = _merge(x2, flat(y_h), flat(yf), flat(yb), flat(bonus), flat(g), gates, p_ones, rw_gn_w, rw_gn_b,
               w_hy_out, w_rw_out, w_o, ln1_w, ln1_b)
    out = _ffn(h, ffn_w_gate, ffn_w_up, ffn_w_down, ln2_w, ln2_b)
    return out.reshape(B, L, D)


def kernel(x, w_in, hy_conv_w, hy_conv_b, hy_filt_w1, hy_filt_b1, hy_filt_w2, hy_filt_b2, hy_filt_w3, hy_filt_b3, hy_filt_w4, hy_sin_freq, hy_skip, rw_mu, rw_w0, rw_w2, rw_a0, rw_a2, rw_g2, rw_k_k, rw_k_a, rw_r_k, rw_gn_w, rw_gn_b, w_hy_out, w_rw_out, w_o, ln1_w, ln1_b, ffn_w_gate, ffn_w_up, ffn_w_down, ln2_w, ln2_b):
    params = (w_in, hy_conv_w, hy_conv_b, hy_filt_w1, hy_filt_b1, hy_filt_w2, hy_filt_b2, hy_filt_w3,
              hy_filt_b3, hy_filt_w4, hy_sin_freq, hy_skip, rw_mu, rw_w0, rw_w2, rw_a0, rw_a2, rw_g2,
              rw_k_k, rw_k_a, rw_r_k, rw_gn_w, rw_gn_b, w_hy_out, w_rw_out, w_o, ln1_w, ln1_b,
              ffn_w_gate, ffn_w_up, ffn_w_down, ln2_w, ln2_b)
    for l in range(w_in.shape[0]):
        x = _layer(x, *[p[l] for p in params])
    return x
```

```python
import functools
import math

import jax
import jax.numpy as jnp
from jax import lax
from jax.experimental import pallas as pl
from jax.experimental.pallas import tpu as pltpu

F32 = jnp.float32
BF16 = jnp.bfloat16
HIGHEST = lax.Precision.HIGHEST

D_MODEL = 1024
HY_WIDTH = 512
HY_ORDER = 2
HY_BANDS = 16
HY_FILT_HIDDEN = 64
HY_FAST_DECAY = 0.3
HY_SLOW_DECAY = 1.5
HY_DECAY_TARGET = 1e-2
HY_MAX_DECAY = math.log(HY_DECAY_TARGET) / HY_FAST_DECAY
HY_MIN_DECAY = math.log(HY_DECAY_TARGET) / HY_SLOW_DECAY
HY_COLS = 3 * HY_WIDTH
RW_WIDTH = 512
RW_HEAD = 64
RW_HEADS = RW_WIDTH // RW_HEAD
RW_LORA = 64
RW_LORA_G = 128
RW_GN_EPS = 64e-5
RW_COLS = 3 * RW_WIDTH + 4 * RW_LORA + RW_LORA_G
GATE_COLS = 2 * D_MODEL
FFN_HIDDEN = ((8 * D_MODEL + 3 * 256 - 1) // (3 * 256)) * 256
DEPTH = 1
DN_ALPHA = (2.0 * DEPTH) ** 0.25
LN_EPS = 1e-5
RW_DECAY_SCALE = math.exp(-0.5)

LANES = 128
VMEM_LIMIT = 56 * 1024 * 1024

FFT_N1 = 32
FFT_F1 = FFT_N1 // 2 + 1
RW_CHUNK = 64


def _cparams(sem, vmem=VMEM_LIMIT):
    return pltpu.CompilerParams(dimension_semantics=sem, vmem_limit_bytes=vmem)


def _dot(a, b, precision=None):
    return jnp.dot(a, b, preferred_element_type=F32, precision=precision)


def _dot_nt(a, b):
    return lax.dot_general(a, b, (((1,), (1,)), ((), ())), preferred_element_type=F32)


def _dot_tn(a, b):
    return lax.dot_general(a, b, (((0,), (0,)), ((), ())), preferred_element_type=F32)


def _layer_norm(h, w, b):
    mu = jnp.mean(h, axis=-1, keepdims=True)
    c = h - mu
    var = jnp.mean(c * c, axis=-1, keepdims=True)
    return c * lax.rsqrt(var + LN_EPS) * w + b


def _segsum(x, p):
    hi = x.astype(BF16)
    lo = (x - hi.astype(F32)).astype(BF16)
    return _dot(hi, p) + _dot(lo, p)


def _split3(x):
    h1 = x.astype(BF16)
    r1 = x - h1.astype(F32)
    h2 = r1.astype(BF16)
    h3 = (r1 - h2.astype(F32)).astype(BF16)
    return h1, h2, h3


def _dot_parts(m, parts):
    out = _dot(m, parts[0])
    for part in parts[1:]:
        out = out + _dot(m, part)
    return out


def _in_proj_kernel(x_ref, w_ref, uh_ref, ur_ref, g_ref):
    xb = x_ref[...].astype(BF16)
    uh_ref[...] = _dot(xb, w_ref[:, :HY_COLS])
    ur_ref[...] = _dot(xb, w_ref[:, HY_COLS:HY_COLS + RW_COLS])
    g_ref[...] = _dot(xb, w_ref[:, HY_COLS + RW_COLS:])


def _in_proj(x2, w_in_bf, tm=256):
    m, d = x2.shape
    n = w_in_bf.shape[1]
    return pl.pallas_call(
        _in_proj_kernel,
        grid=(m // tm,),
        in_specs=[pl.BlockSpec((tm, d), lambda i: (i, 0)),
                  pl.BlockSpec((d, n), lambda i: (0, 0))],
        out_specs=[pl.BlockSpec((tm, HY_COLS), lambda i: (i, 0)),
                   pl.BlockSpec((tm, RW_COLS), lambda i: (i, 0)),
                   pl.BlockSpec((tm, GATE_COLS), lambda i: (i, 0))],
        out_shape=[jax.ShapeDtypeStruct((m, HY_COLS), F32),
                   jax.ShapeDtypeStruct((m, RW_COLS), F32),
                   jax.ShapeDtypeStruct((m, GATE_COLS), F32)],
        compiler_params=_cparams(("parallel",)),
        name="in_proj",
    )(x2, w_in_bf)


def _shortconv_kernel(u_ref, w_ref, b_ref, o_ref):
    u = u_ref[0]
    L = u.shape[0]
    row = lax.broadcasted_iota(jnp.int32, (L, 1), 0)
    prev = jnp.where(row == 0, 0.0, pltpu.roll(u, 1, 0))
    nxt = jnp.where(row == L - 1, 0.0, pltpu.roll(u, L - 1, 0))
    o_ref[0, 0, 0] = w_ref[0:1, :] * prev + w_ref[1:2, :] * u + w_ref[2:3, :] * nxt + b_ref[...]


HY_TILES = HY_WIDTH // LANES


def _shortconv(u_h, conv_w, conv_b):
    B, L, C = u_h.shape
    nct = HY_TILES
    return pl.pallas_call(
        _shortconv_kernel,
        grid=(B, C // LANES),
        in_specs=[pl.BlockSpec((1, L, LANES), lambda b, j: (b, 0, j)),
                  pl.BlockSpec((3, LANES), lambda b, j: (0, j)),
                  pl.BlockSpec((1, LANES), lambda b, j: (0, j))],
        out_specs=pl.BlockSpec((1, 1, 1, L, LANES), lambda b, j: (j // nct, b, j % nct, 0, 0)),
        out_shape=jax.ShapeDtypeStruct((C // HY_WIDTH, B, nct, L, LANES), F32),
        compiler_params=_cparams(("parallel", "parallel")),
        name="hy_shortconv",
    )(u_h, conv_w, conv_b.reshape(1, C))


def _filter_kernel(L, tb, w1t_ref, w1c_ref, w1s_ref, b1_ref, w2_ref, b2_ref, w3_ref, b3_ref,
                   w4_ref, sf_ref, freq_ref, delta_ref, kc_ref):
    n = 2 * L
    i = pl.program_id(0)
    row = i * tb + lax.broadcasted_iota(jnp.int32, (tb, 1), 0)
    second = row >= L
    pos = jnp.where(second, n - row, row).astype(F32)
    t = pos / float(L - 1)
    ang = (2.0 * math.pi * pos / float(L)) * freq_ref[...]
    pre = t * w1t_ref[...] + _dot(jnp.cos(ang), w1c_ref[...], HIGHEST) \
        - _dot(jnp.sin(ang), w1s_ref[...], HIGHEST) + b1_ref[...]
    h = jnp.sin(sf_ref[0:1, :] * pre)
    h = jnp.sin(sf_ref[1:2, :] * (_dot(h, w2_ref[...], HIGHEST) + b2_ref[...]))
    h = jnp.sin(sf_ref[2:3, :] * (_dot(h, w3_ref[...], HIGHEST) + b3_ref[...]))
    h4 = _dot(h, w4_ref[...], HIGHEST)
    window = jnp.exp(-t * delta_ref[...])
    keep = jnp.where(row == L, 0.0, 1.0)
    first = jnp.where(row == 0, 1.0, 0.0)
    for o in range(HY_ORDER):
        base = o * 2 * HY_WIDTH
        fwd = h4[:, base:base + HY_WIDTH]
        bwd = h4[:, base + HY_WIDTH:base + 2 * HY_WIDTH]
        kc = window * (jnp.where(second, bwd, fwd) * keep + first * bwd)
        for q in range(HY_TILES):
            kc_ref[o, q] = kc[:, q * LANES:(q + 1) * LANES]


def _filters(L, fw1, fb1, fw2, fb2, fw3, fb3, fw4, sin_freq, tb=512):
    n = 2 * L
    tb = min(tb, n)
    freqs = jnp.linspace(1e-4, HY_BANDS - 1, HY_BANDS, dtype=F32).reshape(1, HY_BANDS)
    deltas = jnp.abs(jnp.linspace(HY_MIN_DECAY, HY_MAX_DECAY, HY_WIDTH, dtype=F32)).reshape(1, HY_WIDTH)
    hid = HY_FILT_HIDDEN
    full = lambda shape: pl.BlockSpec(shape, lambda i: tuple(0 for _ in shape))
    return pl.pallas_call(
        functools.partial(_filter_kernel, L, tb),
        grid=(n // tb,),
        in_specs=[full((1, hid)), full((HY_BANDS, hid)), full((HY_BANDS, hid)), full((1, hid)),
                  full((hid, hid)), full((1, hid)), full((hid, hid)), full((1, hid)),
                  full((hid, HY_ORDER * 2 * HY_WIDTH)), full((3, hid)),
                  full((1, HY_BANDS)), full((1, HY_WIDTH))],
        out_specs=pl.BlockSpec((HY_ORDER, HY_TILES, tb, LANES), lambda i: (0, 0, i, 0)),
        out_shape=jax.ShapeDtypeStruct((HY_ORDER, HY_TILES, n, LANES), F32),
        compiler_params=_cparams(("parallel",)),
        name="hy_filters",
    )(fw1[0:1], fw1[1:1 + HY_BANDS], fw1[1 + HY_BANDS:], fb1.reshape(1, hid), fw2, fb2.reshape(1, hid),
      fw3, fb3.reshape(1, hid), fw4, sin_freq, freqs, deltas)


def _fft_tables(n2):
    n = FFT_N1 * n2
    f1 = jnp.arange(FFT_F1, dtype=jnp.int32)[:, None, None]
    a = jnp.arange(n2, dtype=jnp.int32)[None, :, None]
    b = jnp.arange(n2, dtype=jnp.int32)[None, None, :]
    ph_f = ((b * (f1 + FFT_N1 * a)) % n).astype(F32) * (2.0 * math.pi / n)
    cr, ci = jnp.cos(ph_f), -jnp.sin(ph_f)
    m_fwd = jnp.concatenate([jnp.concatenate([cr, -ci], axis=2),
                             jnp.concatenate([ci, cr], axis=2)], axis=1).astype(BF16)
    ph_i = ((a * (f1 + FFT_N1 * b)) % n).astype(F32) * (2.0 * math.pi / n)
    herm = jnp.where((f1 == 0) | (f1 == FFT_N1 // 2), 1.0, 2.0) / n
    dr, di = herm * jnp.cos(ph_i), herm * jnp.sin(ph_i)
    m_inv = jnp.concatenate([jnp.concatenate([dr, -di], axis=2),
                             jnp.concatenate([di, dr], axis=2)], axis=1).astype(BF16)
    return m_fwd, m_inv


def _hi_lo(x):
    hi = x.astype(BF16)
    return hi, (x - hi.astype(F32)).astype(BF16)


def _slow_tables():
    k = (jnp.arange(FFT_F1, dtype=jnp.int32)[:, None] * jnp.arange(FFT_N1, dtype=jnp.int32)[None, :]) % FFT_N1
    ph = k.astype(F32) * (2.0 * math.pi / FFT_N1)
    fwd = jnp.stack([jnp.cos(ph), -jnp.sin(ph)], axis=1).reshape(2 * FFT_F1, FFT_N1)
    inv = fwd[:, :FFT_N1 // 2].T
    return fwd, inv


def _dot3(a_hi, a_lo, x):
    x_hi, x_lo = _hi_lo(x)
    return _dot(a_hi, x_hi) + _dot(a_lo, x_hi) + _dot(a_hi, x_lo)


SLOW_UNROLL = 4
SUBLANES = 8
SLOW_TILES = 2


def _kron_rows(m):
    return jnp.kron(m, jnp.eye(SUBLANES, dtype=m.dtype))


def _slow_dft_kernel(n2, s1_count, f_ref, z_ref, a_ref):
    f_hl = f_ref[...]
    half = f_hl.shape[0] // 2
    st = z_ref.shape[2]

    def body(g, carry):
        s2 = pl.multiple_of(g * SUBLANES, SUBLANES)
        zs = jnp.concatenate(
            [_tiles_cat(z_ref, lambda q: (0, 0, q, pl.ds(s1 * n2 + s2, SUBLANES)), st) for s1 in range(s1_count)],
            axis=0).astype(BF16)
        a = _dot(f_hl, zs)
        a = a[:half] + a[half:]
        for f in range(half // SUBLANES):
            _tiles_put(a_ref, lambda q: (0, q, pl.ds(f * n2 + s2, SUBLANES)), a[f * SUBLANES:(f + 1) * SUBLANES])
        return carry

    lax.fori_loop(0, n2 // SUBLANES, body, 0, unroll=min(SLOW_UNROLL, n2 // SUBLANES))


def _tiles_cat(ref, index, ntiles):
    return jnp.concatenate([ref[index(q) + (slice(None),)] for q in range(ntiles)], axis=1)


def _tiles_put(ref, index, value):
    for q in range(value.shape[1] // LANES):
        ref[index(q) + (slice(None),)] = value[:, q * LANES:(q + 1) * LANES]


def _slow_dft(z5, sel, s1_count):
    _, B, tiles, T, _ = z5.shape
    n2 = T // s1_count
    fwd, _ = _slow_tables()
    rows = 2 * FFT_F1
    f_hl = jnp.concatenate([_kron_rows(t) for t in _hi_lo(fwd[:, :s1_count])], axis=0)
    st = SLOW_TILES
    return pl.pallas_call(
        functools.partial(_slow_dft_kernel, n2, s1_count),
        grid=(B, tiles // st),
        in_specs=[pl.BlockSpec(f_hl.shape, lambda b, j: (0, 0)),
                  pl.BlockSpec((1, 1, st, T, LANES), lambda b, j: (sel, b, j, 0, 0))],
        out_specs=pl.BlockSpec((1, st, rows * n2, LANES), lambda b, j: (b, j, 0, 0)),
        out_shape=jax.ShapeDtypeStruct((B, tiles, rows * n2, LANES), F32),
        compiler_params=_cparams(("parallel", "parallel")),
        name="hy_slow_dft",
    )(f_hl, z5)


def _spectrum_kernel(n2, a_ref, mf_ref, hr_ref, hi_ref):
    x = _dot(mf_ref[0], _tiles_cat(a_ref, lambda q: (0, q), a_ref.shape[1]).astype(BF16))
    hr_ref[0] = x[:n2]
    hi_ref[0] = x[n2:]


def _spectrum(kc, m_fwd):
    order, tiles, n, _ = kc.shape
    n2 = n // FFT_N1
    C = tiles * LANES
    a = _slow_dft(kc.reshape(1, order, tiles, n, LANES), 0, FFT_N1)
    spec = lambda: pl.BlockSpec((1, n2, C), lambda f, o: (o, f, 0))
    return pl.pallas_call(
        functools.partial(_spectrum_kernel, n2),
        grid=(FFT_F1, order),
        in_specs=[pl.BlockSpec((1, tiles, 2 * n2, LANES), lambda f, o: (o, 0, f, 0)),
                  pl.BlockSpec((1, 2 * n2, 2 * n2), lambda f, o: (f, 0, 0))],
        out_specs=[spec(), spec()],
        out_shape=[jax.ShapeDtypeStruct((order, FFT_F1 * n2, C), F32)] * 2,
        compiler_params=_cparams(("parallel", "parallel")),
        name="hy_spectrum",
    )(a, m_fwd)


def _conv_mid_kernel(n2, a_ref, hr_ref, hi_ref, mf_ref, mi_ref, b_ref):
    x = _dot(mf_ref[0], _tiles_cat(a_ref, lambda q: (0, q), a_ref.shape[1]).astype(BF16))
    xr, xi = x[:n2], x[n2:]
    hr, hi = hr_ref[0], hi_ref[0]
    y = jnp.concatenate([xr * hr - xi * hi, xr * hi + xi * hr], axis=0).astype(BF16)
    _tiles_put(b_ref, lambda q: (0, q), _dot(mi_ref[0], y))


def _conv_mid(a, order, hr, hi, m_fwd, m_inv):
    B, tiles, total, _ = a.shape
    rows = total // FFT_F1
    n2 = rows // 2
    W = tiles * LANES
    blk = lambda: pl.BlockSpec((1, tiles, rows, LANES), lambda f, b: (b, 0, f, 0))
    return pl.pallas_call(
        functools.partial(_conv_mid_kernel, n2),
        grid=(FFT_F1, B),
        in_specs=[blk(),
                  pl.BlockSpec((1, n2, W), lambda f, b: (order, f, 0)),
                  pl.BlockSpec((1, n2, W), lambda f, b: (order, f, 0)),
                  pl.BlockSpec((1, rows, rows), lambda f, b: (f, 0, 0)),
                  pl.BlockSpec((1, rows, rows), lambda f, b: (f, 0, 0))],
        out_specs=blk(),
        out_shape=jax.ShapeDtypeStruct((B, tiles, total, LANES), F32),
        compiler_params=_cparams(("parallel", "parallel")),
        name=f"hy_conv_mid{order}",
    )(a, hr, hi, m_fwd, m_inv)


IDFT_ROWS = tuple(f for f in range(2 * FFT_F1) if f not in (1, 2 * FFT_F1 - 1))


def _slow_idft_kernel(n2, g_ref, b_ref, z_ref, gate_ref, skip_ref, o_ref):
    g_hl = g_ref[...]
    half = g_hl.shape[0] // 2
    gh = g_hl[:half]
    skip = skip_ref[...]
    st = b_ref.shape[1]

    def body(g, carry):
        t2 = pl.multiple_of(g * SUBLANES, SUBLANES)
        bs = jnp.concatenate(
            [_tiles_cat(b_ref, lambda q: (0, q, pl.ds(f * n2 + t2, SUBLANES)), st) for f in IDFT_ROWS], axis=0)
        b_hi, b_lo = _hi_lo(bs)
        y2 = _dot(g_hl, b_hi)
        y = y2[:half] + y2[half:] + _dot(gh, b_lo)
        for t1 in range(half // SUBLANES):
            tok = lambda q: (0, 0, q, pl.ds(t1 * n2 + t2, SUBLANES))
            z = _tiles_cat(z_ref, tok, st)
            gate = _tiles_cat(gate_ref, tok, st)
            _tiles_put(o_ref, tok, gate * (y[t1 * SUBLANES:(t1 + 1) * SUBLANES] + skip * z))
        return carry

    lax.fori_loop(0, n2 // SUBLANES, body, 0, unroll=min(SLOW_UNROLL, n2 // SUBLANES))


def _slow_idft(bm, z5, zsel, g5, gsel, skip_row):
    B, tiles, total, _ = bm.shape
    t1 = FFT_N1 // 2
    n2 = total // (2 * FFT_F1)
    T = t1 * n2
    _, inv = _slow_tables()
    g_hl = jnp.concatenate([_kron_rows(t) for t in _hi_lo(inv[:, jnp.array(IDFT_ROWS)])], axis=0)
    st = SLOW_TILES
    tok = lambda sel: pl.BlockSpec((1, 1, st, T, LANES), lambda b, j: (sel, b, j, 0, 0))
    return pl.pallas_call(
        functools.partial(_slow_idft_kernel, n2),
        grid=(B, tiles // st),
        in_specs=[pl.BlockSpec(g_hl.shape, lambda b, j: (0, 0)),
                  pl.BlockSpec((1, st, total, LANES), lambda b, j: (b, j, 0, 0)),
                  tok(zsel), tok(gsel),
                  pl.BlockSpec((1, st * LANES), lambda b, j: (0, j))],
        out_specs=tok(0),
        out_shape=jax.ShapeDtypeStruct((1, B, tiles, T, LANES), F32),
        compiler_params=_cparams(("parallel", "parallel")),
        name="hy_slow_idft",
    )(g_hl, bm, z5, g5, skip_row.reshape(1, tiles * LANES))


def _longconv(z5, zsel, g5, gsel, order, skip, hr, hi, m_fwd, m_inv):
    a = _slow_dft(z5, zsel, FFT_N1 // 2)
    bm = _conv_mid(a, order, hr, hi, m_fwd, m_inv)
    return _slow_idft(bm, z5, zsel, g5, gsel, skip[order])


def _hyena(u_h, conv_w, conv_b, fw1, fb1, fw2, fb2, fw3, fb3, fw4, sin_freq, skip):
    B, L, _ = u_h.shape
    u3 = _shortconv(u_h, conv_w, conv_b)
    kc = _filters(L, fw1, fb1, fw2, fb2, fw3, fb3, fw4, sin_freq)
    m_fwd, m_inv = _fft_tables(2 * L // FFT_N1)
    hr, hi = _spectrum(kc, m_fwd)
    z1 = _longconv(u3, 0, u3, 1, 0, skip, hr, hi, m_fwd, m_inv)
    return _longconv(z1, 0, u3, 2, 1, skip, hr, hi, m_fwd, m_inv)[0]


def _rw_prep_kernel(u_ref, up_ref, un_ref, mu_ref, w0_ref, w2f_ref, w2b_ref, a0_ref, a2f_ref, a2b_ref,
                    g2_ref, kk_ref, ka_ref, rk_ref, p_ref, trif_ref, trib_ref, ones_ref, sel_ref,
                    v_o, ktf_o, rtf_o, khf_o, bhf_o, kbf_o, bbf_o, wtf_o,
                    ktb_o, rtb_o, khb_o, bhb_o, kbb_o, bbb_o, wtb_o, g_o, bonus_o):
    j = pl.program_id(1)
    nj = pl.num_programs(1)
    u = u_ref[0]
    tb = u.shape[0]
    prow = jnp.where(j == 0, 0.0, up_ref[0, 7:8, :])
    nrow = jnp.where(j == nj - 1, 0.0, un_ref[0, 0:1, :])
    row = lax.broadcasted_iota(jnp.int32, (tb, 1), 0)
    prev = jnp.where(row == 0, prow, pltpu.roll(u, 1, 0))
    nxt = jnp.where(row == tb - 1, nrow, pltpu.roll(u, tb - 1, 0))
    xs = u + mu_ref[...] * (0.5 * (prev + nxt) - u)
    W = RW_WIDTH
    r, k, v = xs[:, 0:W], xs[:, W:2 * W], xs[:, 2 * W:3 * W]
    wd = jnp.tanh(xs[:, 3 * W:3 * W + 2 * RW_LORA]).astype(BF16)
    ad = xs[:, 3 * W + 2 * RW_LORA:3 * W + 4 * RW_LORA].astype(BF16)
    gd = jax.nn.sigmoid(xs[:, 3 * W + 4 * RW_LORA:]).astype(BF16)
    p = p_ref[...]
    kkn = k * kk_ref[...]
    nrm = jnp.sqrt(_segsum(kkn * kkn, p))
    kk = kkn / jnp.maximum(nrm, 1e-12)
    ka = ka_ref[...]
    lw_f = -RW_DECAY_SCALE * jax.nn.sigmoid(w0_ref[0:1, :] + _dot(wd, w2f_ref[...]))
    lw_b = -RW_DECAY_SCALE * jax.nn.sigmoid(w0_ref[1:2, :] + _dot(wd, w2b_ref[...]))
    a_f = jax.nn.sigmoid(a0_ref[0:1, :] + _dot(ad, a2f_ref[...]))
    a_b = jax.nn.sigmoid(a0_ref[1:2, :] + _dot(ad, a2b_ref[...]))
    kd_f = k * (1.0 + (a_f - 1.0) * ka)
    kd_b = k * (1.0 + (a_b - 1.0) * ka)
    v_o[0] = v.astype(BF16)
    ones = ones_ref[...]
    sel = sel_ref[...]

    def scan_operands(lw, kd, b, tri, outs):
        parts = _split3(lw)
        cum = _dot_parts(tri, parts)
        tot = _dot_parts(ones, parts)
        e_neg = jnp.exp(-cum)
        e_rem = jnp.exp(tot - cum)
        kt_o, rt_o, kh_o, bh_o, kb_o, bb_o, wt_o = outs
        kt_o[0] = (kk * jnp.exp(cum - lw)).astype(BF16)
        rt_o[0] = (r * jnp.exp(cum)).astype(BF16)
        kh_o[0] = (kd * e_neg).astype(BF16)
        bh_o[0] = (b * e_neg).astype(BF16)
        kb_o[0] = (kd * e_rem).astype(BF16)
        bb_o[0] = (b * e_rem).astype(BF16)
        wt_o[0] = jnp.exp(_dot_parts(sel, parts))

    scan_operands(lw_f, kd_f, kk * a_f, trif_ref[...], (ktf_o, rtf_o, khf_o, bhf_o, kbf_o, bbf_o, wtf_o))
    scan_operands(lw_b, kd_b, kk * a_b, trib_ref[...], (ktb_o, rtb_o, khb_o, bhb_o, kbb_o, bbb_o, wtb_o))
    g_o[0] = _dot(gd, g2_ref[...])
    bonus_o[0] = _segsum(r * (kd_f + kd_b) * rk_ref[...], p) * v


def _head_ones():
    h = jnp.arange(RW_WIDTH, dtype=jnp.int32) // RW_HEAD
    return (h[:, None] == h[None, :]).astype(BF16)


def _chunk_matrices(tb, c):
    t = jnp.arange(tb, dtype=jnp.int32)
    same = (t[:, None] // c) == (t[None, :] // c)
    tri_f = (same & (t[None, :] <= t[:, None])).astype(BF16)
    tri_b = (same & (t[None, :] >= t[:, None])).astype(BF16)
    sel = (jnp.arange(tb // c, dtype=jnp.int32)[:, None] == (t[None, :] // c)).astype(BF16)
    return tri_f, tri_b, same.astype(BF16), sel


def _rw_prep(u_r, mu, w0, w2, a0, a2, g2, k_k, k_a, r_k, p_ones, tb=512):
    B, L, C = u_r.shape
    tb = min(tb, L)
    W = RW_WIDTH
    c = min(RW_CHUNK, L)
    ncb = tb // c
    tri_f, tri_b, ones, sel = _chunk_matrices(tb, c)
    zeros = jnp.zeros((RW_LORA, W), F32)
    w2f = jnp.concatenate([w2[0], zeros], axis=0).astype(BF16)
    w2b = jnp.concatenate([zeros, w2[1]], axis=0).astype(BF16)
    a2f = jnp.concatenate([a2[0], zeros], axis=0).astype(BF16)
    a2b = jnp.concatenate([zeros, a2[1]], axis=0).astype(BF16)
    full = lambda shape: pl.BlockSpec(shape, lambda b, j: tuple(0 for _ in shape))
    tok = lambda: pl.BlockSpec((1, tb, W), lambda b, j: (b, j, 0))
    wts = lambda: pl.BlockSpec((1, ncb, W), lambda b, j: (b, j, 0))
    bf_tok = jax.ShapeDtypeStruct((B, L, W), BF16)
    f32_tok = jax.ShapeDtypeStruct((B, L, W), F32)
    wt_shape = jax.ShapeDtypeStruct((B, L // c, W), F32)
    g8 = tb // 8
    outs = pl.pallas_call(
        _rw_prep_kernel,
        grid=(B, L // tb),
        in_specs=[pl.BlockSpec((1, tb, C), lambda b, j: (b, j, 0)),
                  pl.BlockSpec((1, 8, C), lambda b, j: (b, jnp.maximum(j * g8 - 1, 0), 0)),
                  pl.BlockSpec((1, 8, C), lambda b, j: (b, jnp.minimum((j + 1) * g8, L // 8 - 1), 0)),
                  full((1, C)), full((2, W)), full((2 * RW_LORA, W)), full((2 * RW_LORA, W)),
                  full((2, W)), full((2 * RW_LORA, W)), full((2 * RW_LORA, W)),
                  full((RW_LORA_G, W)), full((1, W)), full((1, W)), full((1, W)), full((W, W)),
                  full((tb, tb)), full((tb, tb)), full((tb, tb)), full((ncb, tb))],
        out_specs=[tok()] + ([tok() for _ in range(6)] + [wts()]) * 2 + [tok(), tok()],
        out_shape=[bf_tok] + ([bf_tok] * 6 + [wt_shape]) * 2 + [f32_tok, f32_tok],
        compiler_params=_cparams(("parallel", "parallel")),
        name="rw_prep",
    )(u_r, u_r, u_r, mu.reshape(1, C), w0, w2f, w2b, a0, a2f, a2b, g2.astype(BF16),
      k_k.reshape(1, W), k_a.reshape(1, W), r_k.reshape(1, W), p_ones, tri_f, tri_b, ones, sel)
    v, g, bonus = outs[0], outs[15], outs[16]
    fwd_ops = tuple(outs[1:7]) + (v, outs[7])
    bwd_ops = tuple(outs[8:14]) + (v, outs[14])
    return fwd_ops, bwd_ops, g, bonus, ncb


def _bmm(a, b):
    return lax.dot_general(a, b, (((2,), (1,)), ((0,), (0,))), preferred_element_type=F32)


def _bmm_nt(a, b):
    return lax.dot_general(a, b, (((2,), (2,)), ((0,), (0,))), preferred_element_type=F32)


def _bmm_tn(a, b):
    return lax.dot_general(a, b, (((1,), (1,)), ((0,), (0,))), preferred_element_type=F32)


def _tri_inverse(a_b):
    g, c, _ = a_b.shape
    eye = (lax.broadcasted_iota(jnp.int32, (1, c, c), 1) == lax.broadcasted_iota(jnp.int32, (1, c, c), 2)).astype(F32)
    nmat = -a_b
    x = eye + nmat
    p = nmat
    for _ in range(1, int(math.log2(c))):
        pb = p.astype(BF16)
        p = _bmm(pb, pb)
        x = x + _bmm(x.astype(BF16), p.astype(BF16))
    return x


def _heads(x_f, x_b):
    sl = lambda h: slice(h * RW_HEAD, (h + 1) * RW_HEAD)
    return jnp.stack([x_f[:, sl(h)] for h in range(RW_HEADS)] + [x_b[:, sl(h)] for h in range(RW_HEADS)], axis=0)


def _rw_scan_kernel(nc, ncb, ktf, rtf, khf, bhf, kbf, bbf, vf, wtf, ktb, rtb, khb, bhb, kbb, bbb, vb, wtb,
                    yf_ref, yb_ref, s_ref):
    i = pl.program_id(1)

    @pl.when(i == 0)
    def _():
        s_ref[...] = jnp.zeros_like(s_ref)

    kt, rt = _heads(ktf[0], ktb[0]), _heads(rtf[0], rtb[0])
    khat, bhat = _heads(khf[0], khb[0]), _heads(bhf[0], bhb[0])
    kbar, bbar = _heads(kbf[0], kbb[0]), _heads(bbf[0], bbb[0])
    v = _heads(vf[0], vb[0])
    wt = _heads(wtf[0, pl.ds(i % ncb, 1), :], wtb[0, pl.ds((nc - 1 - i) % ncb, 1), :])
    g, c, _ = kt.shape
    ri = lax.broadcasted_iota(jnp.int32, (g, c, c), 1)
    ci = lax.broadcasted_iota(jnp.int32, (g, c, c), 2)
    rev = lax.broadcasted_iota(jnp.int32, (g, c, c), 0) >= RW_HEADS
    ahead = jnp.where(rev, ri - ci, ci - ri)
    strict = ahead < 0
    causal = ahead <= 0

    lh = jnp.concatenate([kt, rt], axis=1)
    gk = _bmm_nt(lh, khat)
    gb = _bmm_nt(lh, bhat)
    a_k = jnp.where(strict, gk[:, :c], 0.0)
    b_k = jnp.where(causal, gk[:, c:], 0.0)
    a_b = jnp.where(strict, gb[:, :c], 0.0)
    b_b = jnp.where(causal, gb[:, c:], 0.0)
    tmat = _tri_inverse(a_b)
    s0 = s_ref[...]
    ks = _bmm_nt(lh, s0.astype(BF16))
    av = _bmm(jnp.concatenate([a_k, b_k], axis=1).astype(BF16), v)
    ub = _bmm(tmat.astype(BF16), (ks[:, :c] + av[:, :c]).astype(BF16)).astype(BF16)
    y = ks[:, c:] + av[:, c:] - _bmm(b_b.astype(BF16), ub)
    s_ref[...] = s0 * wt + _bmm_tn(v, kbar) - _bmm_tn(ub, bbar)
    for h in range(RW_HEADS):
        sl = slice(h * RW_HEAD, (h + 1) * RW_HEAD)
        yf_ref[0, :, sl] = y[h]
        yb_ref[0, :, sl] = y[RW_HEADS + h]


def _rw_scan(fwd_ops, bwd_ops, ncb):
    B, L, W = fwd_ops[0].shape
    c = min(RW_CHUNK, L)
    nc = L // c
    fwd = lambda: pl.BlockSpec((1, c, W), lambda b, i: (b, i, 0))
    bwd = lambda: pl.BlockSpec((1, c, W), lambda b, i: (b, nc - 1 - i, 0))
    wt_f = pl.BlockSpec((1, ncb, W), lambda b, i: (b, i // ncb, 0))
    wt_b = pl.BlockSpec((1, ncb, W), lambda b, i: (b, (nc - 1 - i) // ncb, 0))
    return pl.pallas_call(
        functools.partial(_rw_scan_kernel, nc, ncb),
        grid=(B, nc),
        in_specs=[fwd() for _ in range(7)] + [wt_f] + [bwd() for _ in range(7)] + [wt_b],
        out_specs=[fwd(), bwd()],
        out_shape=[jax.ShapeDtypeStruct((B, L, W), F32)] * 2,
        scratch_shapes=[pltpu.VMEM((2 * RW_HEADS, RW_HEAD, RW_HEAD), F32)],
        compiler_params=_cparams(("parallel", "arbitrary")),
        name="rw_scan",
    )(*fwd_ops, *bwd_ops)


def _merge_kernel(x_ref, yh_ref, yf_ref, yb_ref, bonus_ref, g_ref, gates_ref, p_ref, gnw_ref, gnb_ref,
                  why_ref, wrw_ref, wo_ref, lnw_ref, lnb_ref, o_ref):
    p = p_ref[...]
    y = yf_ref[...] + yb_ref[...]
    mu = _segsum(y, p) * (1.0 / RW_HEAD)
    yc = y - mu
    var = _segsum(yc * yc, p) * (1.0 / RW_HEAD)
    yn = yc * lax.rsqrt(var + RW_GN_EPS) * gnw_ref[...] + gnb_ref[...]
    y_r = (yn + bonus_ref[...]) * g_ref[...]
    ph = _dot(_tiles_cat(yh_ref, lambda q: (0, q), yh_ref.shape[1]).astype(BF16), why_ref[...])
    pr = _dot(y_r.astype(BF16), wrw_ref[...])
    gates = jax.nn.sigmoid(gates_ref[...])
    m = gates[:, :D_MODEL] * ph + gates[:, D_MODEL:] * pr
    mix = _dot(m.astype(BF16), wo_ref[...])
    o_ref[...] = _layer_norm(DN_ALPHA * x_ref[...] + mix, lnw_ref[...], lnb_ref[...])


def _merge(x2, yh, yf, yb, bonus, g, gates, p_ones, gn_w, gn_b, w_hy_out, w_rw_out, w_o, ln_w, ln_b, tm=512):
    m, d = x2.shape
    _, tiles, L, _ = yh.shape
    tm = min(tm, L)
    per_seq = L // tm
    W = RW_WIDTH
    row = lambda width: pl.BlockSpec((tm, width), lambda i: (i, 0))
    full = lambda shape: pl.BlockSpec(shape, lambda i: tuple(0 for _ in shape))
    return pl.pallas_call(
        _merge_kernel,
        grid=(m // tm,),
        in_specs=[row(d), pl.BlockSpec((1, tiles, tm, LANES), lambda i: (i // per_seq, 0, i % per_seq, 0)),
                  row(W), row(W), row(W), row(W), row(GATE_COLS),
                  full((W, W)), full((1, W)), full((1, W)),
                  full((HY_WIDTH, d)), full((W, d)), full((d, d)), full((1, d)), full((1, d))],
        out_specs=row(d),
        out_shape=jax.ShapeDtypeStruct((m, d), F32),
        compiler_params=_cparams(("parallel",)),
        name="merge_ln1",
    )(x2, yh, yf, yb, bonus, g, gates, p_ones, gn_w.reshape(1, W), gn_b.reshape(1, W),
      w_hy_out.astype(BF16), w_rw_out.astype(BF16), w_o.astype(BF16), ln_w.reshape(1, d), ln_b.reshape(1, d))


def _ffn_kernel(x_ref, wg_ref, wu_ref, wd_ref, lnw_ref, lnb_ref, o_ref, xb_scr, acc_scr):
    j = pl.program_id(1)

    @pl.when(j == 0)
    def _():
        xb_scr[...] = x_ref[...].astype(BF16)

    xb = xb_scr[...]
    hidden = jax.nn.silu(_dot(xb, wg_ref[...])) * _dot(xb, wu_ref[...])
    part = _dot(hidden.astype(BF16), wd_ref[...])

    @pl.when(j == 0)
    def _():
        acc_scr[...] = part

    @pl.when(j > 0)
    def _():
        acc_scr[...] += part

    @pl.when(j == pl.num_programs(1) - 1)
    def _():
        o_ref[...] = _layer_norm(DN_ALPHA * x_ref[...] + acc_scr[...], lnw_ref[...], lnb_ref[...])


def _ffn(x2, w_gate, w_up, w_down, ln_w, ln_b, tm=1024, th=256):
    m, d = x2.shape
    tm = min(tm, m)
    fh = w_gate.shape[1]
    return pl.pallas_call(
        _ffn_kernel,
        grid=(m // tm, fh // th),
        in_specs=[pl.BlockSpec((tm, d), lambda i, j: (i, 0)),
                  pl.BlockSpec((d, th), lambda i, j: (0, j)),
                  pl.BlockSpec((d, th), lambda i, j: (0, j)),
                  pl.BlockSpec((th, d), lambda i, j: (j, 0)),
                  pl.BlockSpec((1, d), lambda i, j: (0, 0)),
                  pl.BlockSpec((1, d), lambda i, j: (0, 0))],
        out_specs=pl.BlockSpec((tm, d), lambda i, j: (i, 0)),
        out_shape=jax.ShapeDtypeStruct((m, d), F32),
        scratch_shapes=[pltpu.VMEM((tm, d), BF16), pltpu.VMEM((tm, d), F32)],
        compiler_params=_cparams(("parallel", "arbitrary")),
        name="ffn_ln2",
    )(x2, w_gate.astype(BF16), w_up.astype(BF16), w_down.astype(BF16), ln_w.reshape(1, d), ln_b.reshape(1, d))


def _layer(x, w_in, hy_conv_w, hy_conv_b, hy_filt_w1, hy_filt_b1, hy_filt_w2, hy_filt_b2,
           hy_filt_w3, hy_filt_b3, hy_filt_w4, hy_sin_freq, hy_skip, rw_mu, rw_w0, rw_w2,
           rw_a0, rw_a2, rw_g2, rw_k_k, rw_k_a, rw_r_k, rw_gn_w, rw_gn_b, w_hy_out, w_rw_out,
           w_o, ln1_w, ln1_b, ffn_w_gate, ffn_w_up, ffn_w_down, ln2_w, ln2_b):
    B, L, D = x.shape
    x2 = x.reshape(B * L, D)
    u_h, u_r, gates = _in_proj(x2, w_in.astype(BF16))
    y_h = _hyena(u_h.reshape(B, L, HY_COLS), hy_conv_w, hy_conv_b, hy_filt_w1, hy_filt_b1, hy_filt_w2,
                 hy_filt_b2, hy_filt_w3, hy_filt_b3, hy_filt_w4, hy_sin_freq, hy_skip)
    p_ones = _head_ones()
    fwd_ops, bwd_ops, g, bonus, ncb = _rw_prep(
        u_r.reshape(B, L, RW_COLS), rw_mu, rw_w0, rw_w2, rw_a0, rw_a2, rw_g2, rw_k_k, rw_k_a, rw_r_k, p_ones)
    yf, yb = _rw_scan(fwd_ops, bwd_ops, ncb)
    flat = lambda a: a.reshape(B * L, a.shape[-1])
    h = _merge(x2, y_h, flat(yf), flat(yb), flat(bonus), flat(g), gates, p_ones, rw_gn_w, rw_gn_b,
               w_hy_out, w_rw_out, w_o, ln1_w, ln1_b)
    out = _ffn(h, ffn_w_gate, ffn_w_up, ffn_w_down, ln2_w, ln2_b)
    return out.reshape(B, L, D)


def kernel(x, w_in, hy_conv_w, hy_conv_b, hy_filt_w1, hy_filt_b1, hy_filt_w2, hy_filt_b2, hy_filt_w3, hy_filt_b3, hy_filt_w4, hy_sin_freq, hy_skip, rw_mu, rw_w0, rw_w2, rw_a0, rw_a2, rw_g2, rw_k_k, rw_k_a, rw_r_k, rw_gn_w, rw_gn_b, w_hy_out, w_rw_out, w_o, ln1_w, ln1_b, ffn_w_gate, ffn_w_up, ffn_w_down, ln2_w, ln2_b):
    params = (w_in, hy_conv_w, hy_conv_b, hy_filt_w1, hy_filt_b1, hy_filt_w2, hy_filt_b2, hy_filt_w3,
              hy_filt_b3, hy_filt_w4, hy_sin_freq, hy_skip, rw_mu, rw_w0, rw_w2, rw_a0, rw_a2, rw_g2,
              rw_k_k, rw_k_a, rw_r_k, rw_gn_w, rw_gn_b, w_hy_out, w_rw_out, w_o, ln1_w, ln1_b,
              ffn_w_gate, ffn_w_up, ffn_w_down, ln2_w, ln2_b)
    for l in range(w_in.shape[0]):
        x = _layer(x, *[p[l] for p in params])
    return x
```

```python
import functools
import math

import jax
import jax.numpy as jnp
from jax import lax
from jax.experimental import pallas as pl
from jax.experimental.pallas import tpu as pltpu

F32 = jnp.float32
BF16 = jnp.bfloat16
HIGHEST = lax.Precision.HIGHEST

D_MODEL = 1024
HY_WIDTH = 512
HY_ORDER = 2
HY_BANDS = 16
HY_FILT_HIDDEN = 64
HY_FAST_DECAY = 0.3
HY_SLOW_DECAY = 1.5
HY_DECAY_TARGET = 1e-2
HY_MAX_DECAY = math.log(HY_DECAY_TARGET) / HY_FAST_DECAY
HY_MIN_DECAY = math.log(HY_DECAY_TARGET) / HY_SLOW_DECAY
HY_COLS = 3 * HY_WIDTH
RW_WIDTH = 512
RW_HEAD = 64
RW_HEADS = RW_WIDTH // RW_HEAD
RW_LORA = 64
RW_LORA_G = 128
RW_GN_EPS = 64e-5
RW_COLS = 3 * RW_WIDTH + 4 * RW_LORA + RW_LORA_G
GATE_COLS = 2 * D_MODEL
FFN_HIDDEN = ((8 * D_MODEL + 3 * 256 - 1) // (3 * 256)) * 256
DEPTH = 1
DN_ALPHA = (2.0 * DEPTH) ** 0.25
LN_EPS = 1e-5
RW_DECAY_SCALE = math.exp(-0.5)

LANES = 128
VMEM_LIMIT = 56 * 1024 * 1024

FFT_N1 = 32
FFT_F1 = FFT_N1 // 2 + 1
RW_CHUNK = 64


def _cparams(sem, vmem=VMEM_LIMIT):
    return pltpu.CompilerParams(dimension_semantics=sem, vmem_limit_bytes=vmem)


def _dot(a, b, precision=None):
    return jnp.dot(a, b, preferred_element_type=F32, precision=precision)


def _dot_nt(a, b):
    return lax.dot_general(a, b, (((1,), (1,)), ((), ())), preferred_element_type=F32)


def _dot_tn(a, b):
    return lax.dot_general(a, b, (((0,), (0,)), ((), ())), preferred_element_type=F32)


def _layer_norm(h, w, b):
    mu = jnp.mean(h, axis=-1, keepdims=True)
    c = h - mu
    var = jnp.mean(c * c, axis=-1, keepdims=True)
    return c * lax.rsqrt(var + LN_EPS) * w + b


def _segsum(x, p):
    hi = x.astype(BF16)
    lo = (x - hi.astype(F32)).astype(BF16)
    return _dot(hi, p) + _dot(lo, p)


def _split3(x):
    h1 = x.astype(BF16)
    r1 = x - h1.astype(F32)
    h2 = r1.astype(BF16)
    h3 = (r1 - h2.astype(F32)).astype(BF16)
    return h1, h2, h3


def _dot_parts(m, parts):
    out = _dot(m, parts[0])
    for part in parts[1:]:
        out = out + _dot(m, part)
    return out


def _in_proj_kernel(x_ref, w_ref, uh_ref, ur_ref, g_ref):
    xb = x_ref[...].astype(BF16)
    uh_ref[...] = _dot(xb, w_ref[:, :HY_COLS])
    ur_ref[...] = _dot(xb, w_ref[:, HY_COLS:HY_COLS + RW_COLS])
    g_ref[...] = _dot(xb, w_ref[:, HY_COLS + RW_COLS:])


def _in_proj(x2, w_in_bf, tm=256):
    m, d = x2.shape
    n = w_in_bf.shape[1]
    return pl.pallas_call(
        _in_proj_kernel,
        grid=(m // tm,),
        in_specs=[pl.BlockSpec((tm, d), lambda i: (i, 0)),
                  pl.BlockSpec((d, n), lambda i: (0, 0))],
        out_specs=[pl.BlockSpec((tm, HY_COLS), lambda i: (i, 0)),
                   pl.BlockSpec((tm, RW_COLS), lambda i: (i, 0)),
                   pl.BlockSpec((tm, GATE_COLS), lambda i: (i, 0))],
        out_shape=[jax.ShapeDtypeStruct((m, HY_COLS), F32),
                   jax.ShapeDtypeStruct((m, RW_COLS), F32),
                   jax.ShapeDtypeStruct((m, GATE_COLS), F32)],
        compiler_params=_cparams(("parallel",)),
        name="in_proj",
    )(x2, w_in_bf)


def _shortconv_kernel(u_ref, w_ref, b_ref, o_ref):
    u = u_ref[0]
    L = u.shape[0]
    row = lax.broadcasted_iota(jnp.int32, (L, 1), 0)
    prev = jnp.where(row == 0, 0.0, pltpu.roll(u, 1, 0))
    nxt = jnp.where(row == L - 1, 0.0, pltpu.roll(u, L - 1, 0))
    o_ref[0, 0, 0] = w_ref[0:1, :] * prev + w_ref[1:2, :] * u + w_ref[2:3, :] * nxt + b_ref[...]


HY_TILES = HY_WIDTH // LANES


def _shortconv(u_h, conv_w, conv_b):
    B, L, C = u_h.shape
    nct = HY_TILES
    return pl.pallas_call(
        _shortconv_kernel,
        grid=(B, C // LANES),
        in_specs=[pl.BlockSpec((1, L, LANES), lambda b, j: (b, 0, j)),
                  pl.BlockSpec((3, LANES), lambda b, j: (0, j)),
                  pl.BlockSpec((1, LANES), lambda b, j: (0, j))],
        out_specs=pl.BlockSpec((1, 1, 1, L, LANES), lambda b, j: (j // nct, b, j % nct, 0, 0)),
        out_shape=jax.ShapeDtypeStruct((C // HY_WIDTH, B, nct, L, LANES), F32),
        compiler_params=_cparams(("parallel", "parallel")),
        name="hy_shortconv",
    )(u_h, conv_w, conv_b.reshape(1, C))


def _filter_kernel(L, tb, w1t_ref, w1c_ref, w1s_ref, b1_ref, w2_ref, b2_ref, w3_ref, b3_ref,
                   w4_ref, sf_ref, freq_ref, delta_ref, kc_ref):
    n = 2 * L
    i = pl.program_id(0)
    row = i * tb + lax.broadcasted_iota(jnp.int32, (tb, 1), 0)
    second = row >= L
    pos = jnp.where(second, n - row, row).astype(F32)
    t = pos / float(L - 1)
    ang = (2.0 * math.pi * pos / float(L)) * freq_ref[...]
    pre = t * w1t_ref[...] + _dot(jnp.cos(ang), w1c_ref[...], HIGHEST) \
        - _dot(jnp.sin(ang), w1s_ref[...], HIGHEST) + b1_ref[...]
    h = jnp.sin(sf_ref[0:1, :] * pre)
    h = jnp.sin(sf_ref[1:2, :] * (_dot(h, w2_ref[...], HIGHEST) + b2_ref[...]))
    h = jnp.sin(sf_ref[2:3, :] * (_dot(h, w3_ref[...], HIGHEST) + b3_ref[...]))
    h4 = _dot(h, w4_ref[...], HIGHEST)
    window = jnp.exp(-t * delta_ref[...])
    keep = jnp.where(row == L, 0.0, 1.0)
    first = jnp.where(row == 0, 1.0, 0.0)
    for o in range(HY_ORDER):
        base = o * 2 * HY_WIDTH
        fwd = h4[:, base:base + HY_WIDTH]
        bwd = h4[:, base + HY_WIDTH:base + 2 * HY_WIDTH]
        kc = window * (jnp.where(second, bwd, fwd) * keep + first * bwd)
        for q in range(HY_TILES):
            kc_ref[o, q] = kc[:, q * LANES:(q + 1) * LANES]


def _filters(L, fw1, fb1, fw2, fb2, fw3, fb3, fw4, sin_freq, tb=512):
    n = 2 * L
    tb = min(tb, n)
    freqs = jnp.linspace(1e-4, HY_BANDS - 1, HY_BANDS, dtype=F32).reshape(1, HY_BANDS)
    deltas = jnp.abs(jnp.linspace(HY_MIN_DECAY, HY_MAX_DECAY, HY_WIDTH, dtype=F32)).reshape(1, HY_WIDTH)
    hid = HY_FILT_HIDDEN
    full = lambda shape: pl.BlockSpec(shape, lambda i: tuple(0 for _ in shape))
    return pl.pallas_call(
        functools.partial(_filter_kernel, L, tb),
        grid=(n // tb,),
        in_specs=[full((1, hid)), full((HY_BANDS, hid)), full((HY_BANDS, hid)), full((1, hid)),
                  full((hid, hid)), full((1, hid)), full((hid, hid)), full((1, hid)),
                  full((hid, HY_ORDER * 2 * HY_WIDTH)), full((3, hid)),
                  full((1, HY_BANDS)), full((1, HY_WIDTH))],
        out_specs=pl.BlockSpec((HY_ORDER, HY_TILES, tb, LANES), lambda i: (0, 0, i, 0)),
        out_shape=jax.ShapeDtypeStruct((HY_ORDER, HY_TILES, n, LANES), F32),
        compiler_params=_cparams(("parallel",)),
        name="hy_filters",
    )(fw1[0:1], fw1[1:1 + HY_BANDS], fw1[1 + HY_BANDS:], fb1.reshape(1, hid), fw2, fb2.reshape(1, hid),
      fw3, fb3.reshape(1, hid), fw4, sin_freq, freqs, deltas)


def _fft_tables(n2):
    n = FFT_N1 * n2
    f1 = jnp.arange(FFT_F1, dtype=jnp.int32)[:, None, None]
    a = jnp.arange(n2, dtype=jnp.int32)[None, :, None]
    b = jnp.arange(n2, dtype=jnp.int32)[None, None, :]
    ph_f = ((b * (f1 + FFT_N1 * a)) % n).astype(F32) * (2.0 * math.pi / n)
    cr, ci = jnp.cos(ph_f), -jnp.sin(ph_f)
    m_fwd = jnp.concatenate([jnp.concatenate([cr, -ci], axis=2),
                             jnp.concatenate([ci, cr], axis=2)], axis=1).astype(BF16)
    ph_i = ((a * (f1 + FFT_N1 * b)) % n).astype(F32) * (2.0 * math.pi / n)
    herm = jnp.where((f1 == 0) | (f1 == FFT_N1 // 2), 1.0, 2.0) / n
    dr, di = herm * jnp.cos(ph_i), herm * jnp.sin(ph_i)
    m_inv = jnp.concatenate([jnp.concatenate([dr, -di], axis=2),
                             jnp.concatenate([di, dr], axis=2)], axis=1).astype(BF16)
    return m_fwd, m_inv


def _hi_lo(x):
    hi = x.astype(BF16)
    return hi, (x - hi.astype(F32)).astype(BF16)


def _slow_tables():
    k = (jnp.arange(FFT_F1, dtype=jnp.int32)[:, None] * jnp.arange(FFT_N1, dtype=jnp.int32)[None, :]) % FFT_N1
    ph = k.astype(F32) * (2.0 * math.pi / FFT_N1)
    fwd = jnp.stack([jnp.cos(ph), -jnp.sin(ph)], axis=1).reshape(2 * FFT_F1, FFT_N1)
    inv = fwd[:, :FFT_N1 // 2].T
    return fwd, inv


def _dot3(a_hi, a_lo, x):
    x_hi, x_lo = _hi_lo(x)
    return _dot(a_hi, x_hi) + _dot(a_lo, x_hi) + _dot(a_hi, x_lo)


SLOW_UNROLL = 4
SUBLANES = 8
SLOW_TILES = 2


def _kron_rows(m):
    return jnp.kron(m, jnp.eye(SUBLANES, dtype=m.dtype))


def _slow_dft_kernel(n2, s1_count, f_ref, z_ref, a_ref):
    f_hl = f_ref[...]
    half = f_hl.shape[0] // 2
    st = z_ref.shape[2]

    def body(g, carry):
        s2 = pl.multiple_of(g * SUBLANES, SUBLANES)
        zs = jnp.concatenate(
            [_tiles_cat(z_ref, lambda q: (0, 0, q, pl.ds(s1 * n2 + s2, SUBLANES)), st) for s1 in range(s1_count)],
            axis=0).astype(BF16)
        a = _dot(f_hl, zs)
        a = a[:half] + a[half:]
        for f in range(half // SUBLANES):
            _tiles_put(a_ref, lambda q: (0, q, pl.ds(f * n2 + s2, SUBLANES)), a[f * SUBLANES:(f + 1) * SUBLANES])
        return carry

    lax.fori_loop(0, n2 // SUBLANES, body, 0, unroll=min(SLOW_UNROLL, n2 // SUBLANES))


def _tiles_cat(ref, index, ntiles):
    return jnp.concatenate([ref[index(q) + (slice(None),)] for q in range(ntiles)], axis=1)


def _tiles_put(ref, index, value):
    for q in range(value.shape[1] // LANES):
        ref[index(q) + (slice(None),)] = value[:, q * LANES:(q + 1) * LANES]


def _slow_dft(z5, sel, s1_count):
    _, B, tiles, T, _ = z5.shape
    n2 = T // s1_count
    fwd, _ = _slow_tables()
    rows = 2 * FFT_F1
    f_hl = jnp.concatenate([_kron_rows(t) for t in _hi_lo(fwd[:, :s1_count])], axis=0)
    st = SLOW_TILES
    return pl.pallas_call(
        functools.partial(_slow_dft_kernel, n2, s1_count),
        grid=(B, tiles // st),
        in_specs=[pl.BlockSpec(f_hl.shape, lambda b, j: (0, 0)),
                  pl.BlockSpec((1, 1, st, T, LANES), lambda b, j: (sel, b, j, 0, 0))],
        out_specs=pl.BlockSpec((1, st, rows * n2, LANES), lambda b, j: (b, j, 0, 0)),
        out_shape=jax.ShapeDtypeStruct((B, tiles, rows * n2, LANES), F32),
        compiler_params=_cparams(("parallel", "parallel")),
        name="hy_slow_dft",
    )(f_hl, z5)


def _spectrum_kernel(n2, a_ref, mf_ref, hr_ref, hi_ref):
    x = _dot(mf_ref[0], _tiles_cat(a_ref, lambda q: (0, q), a_ref.shape[1]).astype(BF16))
    hr_ref[0] = x[:n2]
    hi_ref[0] = x[n2:]


def _spectrum(kc, m_fwd):
    order, tiles, n, _ = kc.shape
    n2 = n // FFT_N1
    C = tiles * LANES
    a = _slow_dft(kc.reshape(1, order, tiles, n, LANES), 0, FFT_N1)
    spec = lambda: pl.BlockSpec((1, n2, C), lambda f, o: (o, f, 0))
    return pl.pallas_call(
        functools.partial(_spectrum_kernel, n2),
        grid=(FFT_F1, order),
        in_specs=[pl.BlockSpec((1, tiles, 2 * n2, LANES), lambda f, o: (o, 0, f, 0)),
                  pl.BlockSpec((1, 2 * n2, 2 * n2), lambda f, o: (f, 0, 0))],
        out_specs=[spec(), spec()],
        out_shape=[jax.ShapeDtypeStruct((order, FFT_F1 * n2, C), F32)] * 2,
        compiler_params=_cparams(("parallel", "parallel")),
        name="hy_spectrum",
    )(a, m_fwd)


def _conv_mid_kernel(n2, a_ref, hr_ref, hi_ref, mf_ref, mi_ref, b_ref):
    x = _dot(mf_ref[0], _tiles_cat(a_ref, lambda q: (0, q), a_ref.shape[1]).astype(BF16))
    xr, xi = x[:n2], x[n2:]
    hr, hi = hr_ref[0], hi_ref[0]
    y = jnp.concatenate([xr * hr - xi * hi, xr * hi + xi * hr], axis=0).astype(BF16)
    _tiles_put(b_ref, lambda q: (0, q), _dot(mi_ref[0], y))


def _conv_mid(a, order, hr, hi, m_fwd, m_inv):
    B, tiles, total, _ = a.shape
    rows = total // FFT_F1
    n2 = rows // 2
    W = tiles * LANES
    blk = lambda: pl.BlockSpec((1, tiles, rows, LANES), lambda f, b: (b, 0, f, 0))
    return pl.pallas_call(
        functools.partial(_conv_mid_kernel, n2),
        grid=(FFT_F1, B),
        in_specs=[blk(),
                  pl.BlockSpec((1, n2, W), lambda f, b: (order, f, 0)),
                  pl.BlockSpec((1, n2, W), lambda f, b: (order, f, 0)),
                  pl.BlockSpec((1, rows, rows), lambda f, b: (f, 0, 0)),
                  pl.BlockSpec((1, rows, rows), lambda f, b: (f, 0, 0))],
        out_specs=blk(),
        out_shape=jax.ShapeDtypeStruct((B, tiles, total, LANES), F32),
        compiler_params=_cparams(("parallel", "parallel")),
        name=f"hy_conv_mid{order}",
    )(a, hr, hi, m_fwd, m_inv)


IDFT_ROWS = tuple(f for f in range(2 * FFT_F1) if f not in (1, 2 * FFT_F1 - 1))


def _slow_idft_kernel(n2, g_ref, b_ref, z_ref, gate_ref, skip_ref, o_ref):
    g_hl = g_ref[...]
    half = g_hl.shape[0] // 2
    gh = g_hl[:half]
    skip = skip_ref[...]
    st = b_ref.shape[1]

    def body(g, carry):
        t2 = pl.multiple_of(g * SUBLANES, SUBLANES)
        bs = jnp.concatenate(
            [_tiles_cat(b_ref, lambda q: (0, q, pl.ds(f * n2 + t2, SUBLANES)), st) for f in IDFT_ROWS], axis=0)
        b_hi, b_lo = _hi_lo(bs)
        y2 = _dot(g_hl, b_hi)
        y = y2[:half] + y2[half:] + _dot(gh, b_lo)
        for t1 in range(half // SUBLANES):
            tok = lambda q: (0, 0, q, pl.ds(t1 * n2 + t2, SUBLANES))
            z = _tiles_cat(z_ref, tok, st)
            gate = _tiles_cat(gate_ref, tok, st)
            _tiles_put(o_ref, tok, gate * (y[t1 * SUBLANES:(t1 + 1) * SUBLANES] + skip * z))
        return carry

    lax.fori_loop(0, n2 // SUBLANES, body, 0, unroll=min(SLOW_UNROLL, n2 // SUBLANES))


def _slow_idft(bm, z5, zsel, g5, gsel, skip_row):
    B, tiles, total, _ = bm.shape
    t1 = FFT_N1 // 2
    n2 = total // (2 * FFT_F1)
    T = t1 * n2
    _, inv = _slow_tables()
    g_hl = jnp.concatenate([_kron_rows(t) for t in _hi_lo(inv[:, jnp.array(IDFT_ROWS)])], axis=0)
    st = SLOW_TILES
    tok = lambda sel: pl.BlockSpec((1, 1, st, T, LANES), lambda b, j: (sel, b, j, 0, 0))
    return pl.pallas_call(
        functools.partial(_slow_idft_kernel, n2),
        grid=(B, tiles // st),
        in_specs=[pl.BlockSpec(g_hl.shape, lambda b, j: (0, 0)),
                  pl.BlockSpec((1, st, total, LANES), lambda b, j: (b, j, 0, 0)),
                  tok(zsel), tok(gsel),
                  pl.BlockSpec((1, st * LANES), lambda b, j: (0, j))],
        out_specs=tok(0),
        out_shape=jax.ShapeDtypeStruct((1, B, tiles, T, LANES), F32),
        compiler_params=_cparams(("parallel", "parallel")),
        name="hy_slow_idft",
    )(g_hl, bm, z5, g5, skip_row.reshape(1, tiles * LANES))


def _longconv(z5, zsel, g5, gsel, order, skip, hr, hi, m_fwd, m_inv):
    a = _slow_dft(z5, zsel, FFT_N1 // 2)
    bm = _conv_mid(a, order, hr, hi, m_fwd, m_inv)
    return _slow_idft(bm, z5, zsel, g5, gsel, skip[order])


def _hyena(u_h, conv_w, conv_b, fw1, fb1, fw2, fb2, fw3, fb3, fw4, sin_freq, skip):
    B, L, _ = u_h.shape
    u3 = _shortconv(u_h, conv_w, conv_b)
    kc = _filters(L, fw1, fb1, fw2, fb2, fw3, fb3, fw4, sin_freq)
    m_fwd, m_inv = _fft_tables(2 * L // FFT_N1)
    hr, hi = _spectrum(kc, m_fwd)
    z1 = _longconv(u3, 0, u3, 1, 0, skip, hr, hi, m_fwd, m_inv)
    return _longconv(z1, 0, u3, 2, 1, skip, hr, hi, m_fwd, m_inv)[0]


def _rw_prep_kernel(u_ref, up_ref, un_ref, mu_ref, w0_ref, w2f_ref, w2b_ref, a0_ref, a2f_ref, a2b_ref,
                    g2_ref, kk_ref, ka_ref, rk_ref, p_ref, trif_ref, trib_ref, ones_ref, sel_ref,
                    v_o, ktf_o, rtf_o, khf_o, bhf_o, kbf_o, bbf_o, wtf_o,
                    ktb_o, rtb_o, khb_o, bhb_o, kbb_o, bbb_o, wtb_o, g_o, bonus_o):
    j = pl.program_id(1)
    nj = pl.num_programs(1)
    u = u_ref[0]
    tb = u.shape[0]
    prow = jnp.where(j == 0, 0.0, up_ref[0, 7:8, :])
    nrow = jnp.where(j == nj - 1, 0.0, un_ref[0, 0:1, :])
    row = lax.broadcasted_iota(jnp.int32, (tb, 1), 0)
    prev = jnp.where(row == 0, prow, pltpu.roll(u, 1, 0))
    nxt = jnp.where(row == tb - 1, nrow, pltpu.roll(u, tb - 1, 0))
    xs = u + mu_ref[...] * (0.5 * (prev + nxt) - u)
    W = RW_WIDTH
    r, k, v = xs[:, 0:W], xs[:, W:2 * W], xs[:, 2 * W:3 * W]
    wd = jnp.tanh(xs[:, 3 * W:3 * W + 2 * RW_LORA]).astype(BF16)
    ad = xs[:, 3 * W + 2 * RW_LORA:3 * W + 4 * RW_LORA].astype(BF16)
    gd = jax.nn.sigmoid(xs[:, 3 * W + 4 * RW_LORA:]).astype(BF16)
    p = p_ref[...]
    kkn = k * kk_ref[...]
    nrm = jnp.sqrt(_segsum(kkn * kkn, p))
    kk = kkn / jnp.maximum(nrm, 1e-12)
    ka = ka_ref[...]
    lw_f = -RW_DECAY_SCALE * jax.nn.sigmoid(w0_ref[0:1, :] + _dot(wd, w2f_ref[...]))
    lw_b = -RW_DECAY_SCALE * jax.nn.sigmoid(w0_ref[1:2, :] + _dot(wd, w2b_ref[...]))
    a_f = jax.nn.sigmoid(a0_ref[0:1, :] + _dot(ad, a2f_ref[...]))
    a_b = jax.nn.sigmoid(a0_ref[1:2, :] + _dot(ad, a2b_ref[...]))
    kd_f = k * (1.0 + (a_f - 1.0) * ka)
    kd_b = k * (1.0 + (a_b - 1.0) * ka)
    v_o[0] = v.astype(BF16)
    ones = ones_ref[...]
    sel = sel_ref[...]

    def scan_operands(lw, kd, b, tri, outs):
        parts = _split3(lw)
        cum = _dot_parts(tri, parts)
        tot = _dot_parts(ones, parts)
        e_neg = jnp.exp(-cum)
        e_rem = jnp.exp(tot - cum)
        kt_o, rt_o, kh_o, bh_o, kb_o, bb_o, wt_o = outs
        kt_o[0] = (kk * jnp.exp(cum - lw)).astype(BF16)
        rt_o[0] = (r * jnp.exp(cum)).astype(BF16)
        kh_o[0] = (kd * e_neg).astype(BF16)
        bh_o[0] = (b * e_neg).astype(BF16)
        kb_o[0] = (kd * e_rem).astype(BF16)
        bb_o[0] = (b * e_rem).astype(BF16)
        wt_o[0] = jnp.exp(_dot_parts(sel, parts))

    scan_operands(lw_f, kd_f, kk * a_f, trif_ref[...], (ktf_o, rtf_o, khf_o, bhf_o, kbf_o, bbf_o, wtf_o))
    scan_operands(lw_b, kd_b, kk * a_b, trib_ref[...], (ktb_o, rtb_o, khb_o, bhb_o, kbb_o, bbb_o, wtb_o))
    g_o[0] = _dot(gd, g2_ref[...])
    bonus_o[0] = _segsum(r * (kd_f + kd_b) * rk_ref[...], p) * v


def _head_ones():
    h = jnp.arange(RW_WIDTH, dtype=jnp.int32) // RW_HEAD
    return (h[:, None] == h[None, :]).astype(BF16)


def _chunk_matrices(tb, c):
    t = jnp.arange(tb, dtype=jnp.int32)
    same = (t[:, None] // c) == (t[None, :] // c)
    tri_f = (same & (t[None, :] <= t[:, None])).astype(BF16)
    tri_b = (same & (t[None, :] >= t[:, None])).astype(BF16)
    sel = (jnp.arange(tb // c, dtype=jnp.int32)[:, None] == (t[None, :] // c)).astype(BF16)
    return tri_f, tri_b, same.astype(BF16), sel


def _rw_prep(u_r, mu, w0, w2, a0, a2, g2, k_k, k_a, r_k, p_ones, tb=512):
    B, L, C = u_r.shape
    tb = min(tb, L)
    W = RW_WIDTH
    c = min(RW_CHUNK, L)
    ncb = tb // c
    tri_f, tri_b, ones, sel = _chunk_matrices(tb, c)
    zeros = jnp.zeros((RW_LORA, W), F32)
    w2f = jnp.concatenate([w2[0], zeros], axis=0).astype(BF16)
    w2b = jnp.concatenate([zeros, w2[1]], axis=0).astype(BF16)
    a2f = jnp.concatenate([a2[0], zeros], axis=0).astype(BF16)
    a2b = jnp.concatenate([zeros, a2[1]], axis=0).astype(BF16)
    full = lambda shape: pl.BlockSpec(shape, lambda b, j: tuple(0 for _ in shape))
    tok = lambda: pl.BlockSpec((1, tb, W), lambda b, j: (b, j, 0))
    wts = lambda: pl.BlockSpec((1, ncb, W), lambda b, j: (b, j, 0))
    bf_tok = jax.ShapeDtypeStruct((B, L, W), BF16)
    f32_tok = jax.ShapeDtypeStruct((B, L, W), F32)
    wt_shape = jax.ShapeDtypeStruct((B, L // c, W), F32)
    g8 = tb // 8
    outs = pl.pallas_call(
        _rw_prep_kernel,
        grid=(B, L // tb),
        in_specs=[pl.BlockSpec((1, tb, C), lambda b, j: (b, j, 0)),
                  pl.BlockSpec((1, 8, C), lambda b, j: (b, jnp.maximum(j * g8 - 1, 0), 0)),
                  pl.BlockSpec((1, 8, C), lambda b, j: (b, jnp.minimum((j + 1) * g8, L // 8 - 1), 0)),
                  full((1, C)), full((2, W)), full((2 * RW_LORA, W)), full((2 * RW_LORA, W)),
                  full((2, W)), full((2 * RW_LORA, W)), full((2 * RW_LORA, W)),
                  full((RW_LORA_G, W)), full((1, W)), full((1, W)), full((1, W)), full((W, W)),
                  full((tb, tb)), full((tb, tb)), full((tb, tb)), full((ncb, tb))],
        out_specs=[tok()] + ([tok() for _ in range(6)] + [wts()]) * 2 + [tok(), tok()],
        out_shape=[bf_tok] + ([bf_tok] * 6 + [wt_shape]) * 2 + [f32_tok, f32_tok],
        compiler_params=_cparams(("parallel", "parallel")),
        name="rw_prep",
    )(u_r, u_r, u_r, mu.reshape(1, C), w0, w2f, w2b, a0, a2f, a2b, g2.astype(BF16),
      k_k.reshape(1, W), k_a.reshape(1, W), r_k.reshape(1, W), p_ones, tri_f, tri_b, ones, sel)
    v, g, bonus = outs[0], outs[15], outs[16]
    fwd_ops = tuple(outs[1:7]) + (v, outs[7])
    bwd_ops = tuple(outs[8:14]) + (v, outs[14])
    return fwd_ops, bwd_ops, g, bonus, ncb


def _bmm(a, b):
    return lax.dot_general(a, b, (((2,), (1,)), ((0,), (0,))), preferred_element_type=F32)


def _bmm_nt(a, b):
    return lax.dot_general(a, b, (((2,), (2,)), ((0,), (0,))), preferred_element_type=F32)


def _bmm_tn(a, b):
    return lax.dot_general(a, b, (((1,), (1,)), ((0,), (0,))), preferred_element_type=F32)


RW_QUAD = 4 * RW_HEAD
RW_SEQS = 4


def _block_diag(x, same_head):
    reps = RW_QUAD // x.shape[1]
    return jnp.where(same_head, jnp.concatenate([x] * reps, axis=1), jnp.zeros((), x.dtype))


def _quads(x_f, x_b):
    halves = lambda x: [x[n, :, s:s + RW_QUAD] for n in range(x.shape[0]) for s in range(0, x.shape[2], RW_QUAD)]
    return jnp.stack(halves(x_f) + halves(x_b), axis=0)


def _rw_scan_kernel(nc, ncb, ktf, rtf, khf, bhf, kbf, bbf, vf, wtf, ktb, rtb, khb, bhb, kbb, bbb, vb, wtb,
                    yf_ref, yb_ref, s_ref):
    i = pl.program_id(1)

    @pl.when(i == 0)
    def _():
        s_ref[...] = jnp.zeros_like(s_ref)

    kt, rt = _quads(ktf[...], ktb[...]), _quads(rtf[...], rtb[...])
    khat, bhat = _quads(khf[...], khb[...]), _quads(bhf[...], bhb[...])
    kbar, bbar = _quads(kbf[...], kbb[...]), _quads(bbf[...], bbb[...])
    v = _quads(vf[...], vb[...])
    wt = _quads(wtf[:, pl.ds(i % ncb, 1), :], wtb[:, pl.ds((nc - 1 - i) % ncb, 1), :])
    g, c, q = kt.shape
    ri = lax.broadcasted_iota(jnp.int32, (g, c, q), 1)
    ci = lax.broadcasted_iota(jnp.int32, (g, c, q), 2) % c
    rev = lax.broadcasted_iota(jnp.int32, (g, c, q), 0) >= g // 2
    ahead = jnp.where(rev, ri - ci, ci - ri)
    strict = ahead < 0
    causal = ahead <= 0
    eye = jnp.where(ahead == 0, 1.0, 0.0)
    same_head = (lax.broadcasted_iota(jnp.int32, (1, q, q), 1) // RW_HEAD
                 == lax.broadcasted_iota(jnp.int32, (1, q, q), 2) // RW_HEAD)
    bd = lambda x: _block_diag(x, same_head)

    lh = jnp.concatenate([kt, rt], axis=1)
    gk = _bmm_nt(lh, bd(khat))
    gb = _bmm_nt(lh, bd(bhat))
    a_k = jnp.where(strict, gk[:, :c], 0.0)
    b_k = jnp.where(causal, gk[:, c:], 0.0)
    a_b = jnp.where(strict, gb[:, :c], 0.0)
    b_b = jnp.where(causal, gb[:, c:], 0.0)
    p = -a_b
    tmat = eye + p
    p = _bmm(p.astype(BF16), bd(p.astype(BF16)))
    doublings = int(math.log2(c)) - 1
    for step in range(doublings):
        p_bd = bd(p.astype(BF16))
        if step == doublings - 1:
            tmat = tmat + _bmm(tmat.astype(BF16), p_bd)
        else:
            both = _bmm(jnp.concatenate([p, tmat], axis=1).astype(BF16), p_bd)
            tmat = tmat + both[:, c:]
            p = both[:, :c]
    s0 = s_ref[...]
    ks = _bmm_nt(lh, s0.astype(BF16))
    av = _bmm(jnp.concatenate([a_k, b_k], axis=1).astype(BF16), bd(v))
    ub = _bmm(tmat.astype(BF16), bd((ks[:, :c] + av[:, :c]).astype(BF16))).astype(BF16)
    y = ks[:, c:] + av[:, c:] - _bmm(b_b.astype(BF16), bd(ub))
    upd = _bmm_tn(jnp.concatenate([v, ub], axis=1), jnp.concatenate([kbar, -bbar], axis=1))
    s_ref[...] = s0 * wt + jnp.where(same_head, upd, 0.0)
    per_seq = yf_ref.shape[2] // q
    for n in range(yf_ref.shape[0]):
        yf_ref[n] = jnp.concatenate([y[n * per_seq + j] for j in range(per_seq)], axis=1)
        yb_ref[n] = jnp.concatenate([y[g // 2 + n * per_seq + j] for j in range(per_seq)], axis=1)


def _rw_scan(fwd_ops, bwd_ops, ncb):
    B, L, W = fwd_ops[0].shape
    c = min(RW_CHUNK, L)
    nc = L // c
    nb = RW_SEQS if B % RW_SEQS == 0 else 1
    fwd = lambda: pl.BlockSpec((nb, c, W), lambda b, i: (b, i, 0))
    bwd = lambda: pl.BlockSpec((nb, c, W), lambda b, i: (b, nc - 1 - i, 0))
    wt_f = pl.BlockSpec((nb, ncb, W), lambda b, i: (b, i // ncb, 0))
    wt_b = pl.BlockSpec((nb, ncb, W), lambda b, i: (b, (nc - 1 - i) // ncb, 0))
    return pl.pallas_call(
        functools.partial(_rw_scan_kernel, nc, ncb),
        grid=(B // nb, nc),
        in_specs=[fwd() for _ in range(7)] + [wt_f] + [bwd() for _ in range(7)] + [wt_b],
        out_specs=[fwd(), bwd()],
        out_shape=[jax.ShapeDtypeStruct((B, L, W), F32)] * 2,
        scratch_shapes=[pltpu.VMEM((2 * nb * W // RW_QUAD, RW_QUAD, RW_QUAD), F32)],
        compiler_params=_cparams(("parallel", "arbitrary")),
        name="rw_scan",
    )(*fwd_ops, *bwd_ops)


def _merge_kernel(x_ref, yh_ref, yf_ref, yb_ref, bonus_ref, g_ref, gates_ref, p_ref, gnw_ref, gnb_ref,
                  why_ref, wrw_ref, wo_ref, lnw_ref, lnb_ref, o_ref):
    p = p_ref[...]
    y = yf_ref[...] + yb_ref[...]
    mu = _segsum(y, p) * (1.0 / RW_HEAD)
    yc = y - mu
    var = _segsum(yc * yc, p) * (1.0 / RW_HEAD)
    yn = yc * lax.rsqrt(var + RW_GN_EPS) * gnw_ref[...] + gnb_ref[...]
    y_r = (yn + bonus_ref[...]) * g_ref[...]
    ph = _dot(_tiles_cat(yh_ref, lambda q: (0, q), yh_ref.shape[1]).astype(BF16), why_ref[...])
    pr = _dot(y_r.astype(BF16), wrw_ref[...])
    gates = jax.nn.sigmoid(gates_ref[...])
    m = gates[:, :D_MODEL] * ph + gates[:, D_MODEL:] * pr
    mix = _dot(m.astype(BF16), wo_ref[...])
    o_ref[...] = _layer_norm(DN_ALPHA * x_ref[...] + mix, lnw_ref[...], lnb_ref[...])


def _merge(x2, yh, yf, yb, bonus, g, gates, p_ones, gn_w, gn_b, w_hy_out, w_rw_out, w_o, ln_w, ln_b, tm=512):
    m, d = x2.shape
    _, tiles, L, _ = yh.shape
    tm = min(tm, L)
    per_seq = L // tm
    W = RW_WIDTH
    row = lambda width: pl.BlockSpec((tm, width), lambda i: (i, 0))
    full = lambda shape: pl.BlockSpec(shape, lambda i: tuple(0 for _ in shape))
    return pl.pallas_call(
        _merge_kernel,
        grid=(m // tm,),
        in_specs=[row(d), pl.BlockSpec((1, tiles, tm, LANES), lambda i: (i // per_seq, 0, i % per_seq, 0)),
                  row(W), row(W), row(W), row(W), row(GATE_COLS),
                  full((W, W)), full((1, W)), full((1, W)),
                  full((HY_WIDTH, d)), full((W, d)), full((d, d)), full((1, d)), full((1, d))],
        out_specs=row(d),
        out_shape=jax.ShapeDtypeStruct((m, d), F32),
        compiler_params=_cparams(("parallel",)),
        name="merge_ln1",
    )(x2, yh, yf, yb, bonus, g, gates, p_ones, gn_w.reshape(1, W), gn_b.reshape(1, W),
      w_hy_out.astype(BF16), w_rw_out.astype(BF16), w_o.astype(BF16), ln_w.reshape(1, d), ln_b.reshape(1, d))


def _ffn_kernel(x_ref, wg_ref, wu_ref, wd_ref, lnw_ref, lnb_ref, o_ref):
    x = x_ref[...]
    xb = x.astype(BF16)
    hidden = jax.nn.silu(_dot(xb, wg_ref[...])) * _dot(xb, wu_ref[...])
    ffn = _dot(hidden.astype(BF16), wd_ref[...])
    o_ref[...] = _layer_norm(DN_ALPHA * x + ffn, lnw_ref[...], lnb_ref[...])


def _ffn(x2, w_gate, w_up, w_down, ln_w, ln_b, tm=512):
    m, d = x2.shape
    tm = min(tm, m)
    fh = w_gate.shape[1]
    resident = lambda shape: pl.BlockSpec(shape, lambda i: (0, 0), pipeline_mode=pl.Buffered(1))
    return pl.pallas_call(
        _ffn_kernel,
        grid=(m // tm,),
        in_specs=[pl.BlockSpec((tm, d), lambda i: (i, 0)),
                  resident((d, fh)), resident((d, fh)), resident((fh, d)),
                  resident((1, d)), resident((1, d))],
        out_specs=pl.BlockSpec((tm, d), lambda i: (i, 0)),
        out_shape=jax.ShapeDtypeStruct((m, d), F32),
        compiler_params=_cparams(("parallel",)),
        name="ffn_ln2",
    )(x2, w_gate.astype(BF16), w_up.astype(BF16), w_down.astype(BF16), ln_w.reshape(1, d), ln_b.reshape(1, d))


def _layer(x, w_in, hy_conv_w, hy_conv_b, hy_filt_w1, hy_filt_b1, hy_filt_w2, hy_filt_b2,
           hy_filt_w3, hy_filt_b3, hy_filt_w4, hy_sin_freq, hy_skip, rw_mu, rw_w0, rw_w2,
           rw_a0, rw_a2, rw_g2, rw_k_k, rw_k_a, rw_r_k, rw_gn_w, rw_gn_b, w_hy_out, w_rw_out,
           w_o, ln1_w, ln1_b, ffn_w_gate, ffn_w_up, ffn_w_down, ln2_w, ln2_b):
    B, L, D = x.shape
    x2 = x.reshape(B * L, D)
    u_h, u_r, gates = _in_proj(x2, w_in.astype(BF16))
    y_h = _hyena(u_h.reshape(B, L, HY_COLS), hy_conv_w, hy_conv_b, hy_filt_w1, hy_filt_b1, hy_filt_w2,
                 hy_filt_b2, hy_filt_w3, hy_filt_b3, hy_filt_w4, hy_sin_freq, hy_skip)
    p_ones = _head_ones()
    fwd_ops, bwd_ops, g, bonus, ncb = _rw_prep(
        u_r.reshape(B, L, RW_COLS), rw_mu, rw_w0, rw_w2, rw_a0, rw_a2, rw_g2, rw_k_k, rw_k_a, rw_r_k, p_ones)
    yf, yb = _rw_scan(fwd_ops, bwd_ops, ncb)
    flat = lambda a: a.reshape(B * L, a.shape[-1])
    h = _merge(x2, y_h, flat(yf), flat(yb), flat(bonus), flat(g), gates, p_ones, rw_gn_w, rw_gn_b,
               w_hy_out, w_rw_out, w_o, ln1_w, ln1_b)
    out = _ffn(h, ffn_w_gate, ffn_w_up, ffn_w_down, ln2_w, ln2_b)
    return out.reshape(B, L, D)


def kernel(x, w_in, hy_conv_w, hy_conv_b, hy_filt_w1, hy_filt_b1, hy_filt_w2, hy_filt_b2, hy_filt_w3, hy_filt_b3, hy_filt_w4, hy_sin_freq, hy_skip, rw_mu, rw_w0, rw_w2, rw_a0, rw_a2, rw_g2, rw_k_k, rw_k_a, rw_r_k, rw_gn_w, rw_gn_b, w_hy_out, w_rw_out, w_o, ln1_w, ln1_b, ffn_w_gate, ffn_w_up, ffn_w_down, ln2_w, ln2_b):
    params = (w_in, hy_conv_w, hy_conv_b, hy_filt_w1, hy_filt_b1, hy_filt_w2, hy_filt_b2, hy_filt_w3,
              hy_filt_b3, hy_filt_w4, hy_sin_freq, hy_skip, rw_mu, rw_w0, rw_w2, rw_a0, rw_a2, rw_g2,
              rw_k_k, rw_k_a, rw_r_k, rw_gn_w, rw_gn_b, w_hy_out, w_rw_out, w_o, ln1_w, ln1_b,
              ffn_w_gate, ffn_w_up, ffn_w_down, ln2_w, ln2_b)
    for l in range(w_in.shape[0]):
        x = _layer(x, *[p[l] for p in params])
    return x
```

```python
import functools
import math

import jax
import jax.numpy as jnp
from jax import lax
from jax.experimental import pallas as pl
from jax.experimental.pallas import tpu as pltpu

F32 = jnp.float32
BF16 = jnp.bfloat16
HIGHEST = lax.Precision.HIGHEST

D_MODEL = 1024
HY_WIDTH = 512
HY_ORDER = 2
HY_BANDS = 16
HY_FILT_HIDDEN = 64
HY_FAST_DECAY = 0.3
HY_SLOW_DECAY = 1.5
HY_DECAY_TARGET = 1e-2
HY_MAX_DECAY = math.log(HY_DECAY_TARGET) / HY_FAST_DECAY
HY_MIN_DECAY = math.log(HY_DECAY_TARGET) / HY_SLOW_DECAY
HY_COLS = 3 * HY_WIDTH
RW_WIDTH = 512
RW_HEAD = 64
RW_HEADS = RW_WIDTH // RW_HEAD
RW_LORA = 64
RW_LORA_G = 128
RW_GN_EPS = 64e-5
RW_COLS = 3 * RW_WIDTH + 4 * RW_LORA + RW_LORA_G
GATE_COLS = 2 * D_MODEL
FFN_HIDDEN = ((8 * D_MODEL + 3 * 256 - 1) // (3 * 256)) * 256
DEPTH = 1
DN_ALPHA = (2.0 * DEPTH) ** 0.25
LN_EPS = 1e-5
RW_DECAY_SCALE = math.exp(-0.5)

LANES = 128
VMEM_LIMIT = 56 * 1024 * 1024

FFT_N1 = 32
FFT_F1 = FFT_N1 // 2 + 1
RW_CHUNK = 64


def _cparams(sem, vmem=VMEM_LIMIT):
    return pltpu.CompilerParams(dimension_semantics=sem, vmem_limit_bytes=vmem)


def _dot(a, b, precision=None):
    return jnp.dot(a, b, preferred_element_type=F32, precision=precision)


def _dot_nt(a, b):
    return lax.dot_general(a, b, (((1,), (1,)), ((), ())), preferred_element_type=F32)


def _dot_tn(a, b):
    return lax.dot_general(a, b, (((0,), (0,)), ((), ())), preferred_element_type=F32)


def _layer_norm(h, w, b):
    mu = jnp.mean(h, axis=-1, keepdims=True)
    c = h - mu
    var = jnp.mean(c * c, axis=-1, keepdims=True)
    return c * lax.rsqrt(var + LN_EPS) * w + b


def _segsum(x, p):
    hi = x.astype(BF16)
    lo = (x - hi.astype(F32)).astype(BF16)
    return _dot(hi, p) + _dot(lo, p)


def _split3(x):
    h1 = x.astype(BF16)
    r1 = x - h1.astype(F32)
    h2 = r1.astype(BF16)
    h3 = (r1 - h2.astype(F32)).astype(BF16)
    return h1, h2, h3


def _dot_parts(m, parts):
    out = _dot(m, parts[0])
    for part in parts[1:]:
        out = out + _dot(m, part)
    return out


HY_TILES = HY_WIDTH // LANES
HALO = 8


def _in_proj_kernel(per_seq, x_ref, xp_ref, xn_ref, w_ref, cw_ref, cb_ref, u3_ref, ur_ref, g_ref):
    pos = pl.program_id(0) % per_seq
    w_h = w_ref[:, :HY_COLS]
    xb = x_ref[...].astype(BF16)
    tm = xb.shape[0]
    ext = _dot(jnp.concatenate([xp_ref[...].astype(BF16), xb, xn_ref[...].astype(BF16)], axis=0), w_h)
    u = ext[HALO:HALO + tm]
    row = lax.broadcasted_iota(jnp.int32, (tm, 1), 0)
    prev = jnp.where((row == 0) & (pos == 0), 0.0, ext[HALO - 1:HALO - 1 + tm])
    nxt = jnp.where((row == tm - 1) & (pos == per_seq - 1), 0.0, ext[HALO + 1:HALO + 1 + tm])
    conv = cw_ref[0:1, :] * prev + cw_ref[1:2, :] * u + cw_ref[2:3, :] * nxt + cb_ref[...]
    for s in range(HY_COLS // HY_WIDTH):
        for q in range(HY_TILES):
            lo = s * HY_WIDTH + q * LANES
            u3_ref[s, 0, q] = conv[:, lo:lo + LANES]
    ur_ref[...] = _dot(xb, w_ref[:, HY_COLS:HY_COLS + RW_COLS])
    g_ref[...] = _dot(xb, w_ref[:, HY_COLS + RW_COLS:])


def _in_proj(x2, L, w_in_bf, conv_w, conv_b, tm=256):
    m, d = x2.shape
    n = w_in_bf.shape[1]
    tm = min(tm, L)
    per_seq = L // tm
    groups = tm // HALO
    resident = lambda shape: pl.BlockSpec(shape, lambda i: (0, 0), pipeline_mode=pl.Buffered(1))
    slabs = HY_COLS // HY_WIDTH
    return pl.pallas_call(
        functools.partial(_in_proj_kernel, per_seq),
        grid=(m // tm,),
        in_specs=[pl.BlockSpec((tm, d), lambda i: (i, 0)),
                  pl.BlockSpec((HALO, d), lambda i: (jnp.maximum(i * groups - 1, 0), 0)),
                  pl.BlockSpec((HALO, d), lambda i: (jnp.minimum((i + 1) * groups, m // HALO - 1), 0)),
                  resident((d, n)), resident((3, HY_COLS)), resident((1, HY_COLS))],
        out_specs=[pl.BlockSpec((slabs, 1, HY_TILES, tm, LANES), lambda i: (0, i // per_seq, 0, i % per_seq, 0)),
                   pl.BlockSpec((tm, RW_COLS), lambda i: (i, 0)),
                   pl.BlockSpec((tm, GATE_COLS), lambda i: (i, 0))],
        out_shape=[jax.ShapeDtypeStruct((slabs, m // L, HY_TILES, L, LANES), F32),
                   jax.ShapeDtypeStruct((m, RW_COLS), F32),
                   jax.ShapeDtypeStruct((m, GATE_COLS), F32)],
        compiler_params=_cparams(("parallel",)),
        name="in_proj",
    )(x2, x2, x2, w_in_bf, conv_w, conv_b.reshape(1, HY_COLS))


def _filter_kernel(L, tb, w1t_ref, w1c_ref, w1s_ref, b1_ref, w2_ref, b2_ref, w3_ref, b3_ref,
                   w4_ref, sf_ref, freq_ref, delta_ref, kc_ref):
    n = 2 * L
    i = pl.program_id(0)
    row = i * tb + lax.broadcasted_iota(jnp.int32, (tb, 1), 0)
    second = row >= L
    pos = jnp.where(second, n - row, row).astype(F32)
    t = pos / float(L - 1)
    ang = (2.0 * math.pi * pos / float(L)) * freq_ref[...]
    pre = t * w1t_ref[...] + _dot(jnp.cos(ang), w1c_ref[...], HIGHEST) \
        - _dot(jnp.sin(ang), w1s_ref[...], HIGHEST) + b1_ref[...]
    h = jnp.sin(sf_ref[0:1, :] * pre)
    h = jnp.sin(sf_ref[1:2, :] * (_dot(h, w2_ref[...], HIGHEST) + b2_ref[...]))
    h = jnp.sin(sf_ref[2:3, :] * (_dot(h, w3_ref[...], HIGHEST) + b3_ref[...]))
    h4 = _dot(h, w4_ref[...], HIGHEST)
    window = jnp.exp(-t * delta_ref[...])
    keep = jnp.where(row == L, 0.0, 1.0)
    first = jnp.where(row == 0, 1.0, 0.0)
    for o in range(HY_ORDER):
        base = o * 2 * HY_WIDTH
        fwd = h4[:, base:base + HY_WIDTH]
        bwd = h4[:, base + HY_WIDTH:base + 2 * HY_WIDTH]
        kc = window * (jnp.where(second, bwd, fwd) * keep + first * bwd)
        for q in range(HY_TILES):
            kc_ref[o, q] = kc[:, q * LANES:(q + 1) * LANES]


def _filters(L, fw1, fb1, fw2, fb2, fw3, fb3, fw4, sin_freq, tb=512):
    n = 2 * L
    tb = min(tb, n)
    freqs = jnp.linspace(1e-4, HY_BANDS - 1, HY_BANDS, dtype=F32).reshape(1, HY_BANDS)
    deltas = jnp.abs(jnp.linspace(HY_MIN_DECAY, HY_MAX_DECAY, HY_WIDTH, dtype=F32)).reshape(1, HY_WIDTH)
    hid = HY_FILT_HIDDEN
    full = lambda shape: pl.BlockSpec(shape, lambda i: tuple(0 for _ in shape))
    return pl.pallas_call(
        functools.partial(_filter_kernel, L, tb),
        grid=(n // tb,),
        in_specs=[full((1, hid)), full((HY_BANDS, hid)), full((HY_BANDS, hid)), full((1, hid)),
                  full((hid, hid)), full((1, hid)), full((hid, hid)), full((1, hid)),
                  full((hid, HY_ORDER * 2 * HY_WIDTH)), full((3, hid)),
                  full((1, HY_BANDS)), full((1, HY_WIDTH))],
        out_specs=pl.BlockSpec((HY_ORDER, HY_TILES, tb, LANES), lambda i: (0, 0, i, 0)),
        out_shape=jax.ShapeDtypeStruct((HY_ORDER, HY_TILES, n, LANES), F32),
        compiler_params=_cparams(("parallel",)),
        name="hy_filters",
    )(fw1[0:1], fw1[1:1 + HY_BANDS], fw1[1 + HY_BANDS:], fb1.reshape(1, hid), fw2, fb2.reshape(1, hid),
      fw3, fb3.reshape(1, hid), fw4, sin_freq, freqs, deltas)


def _fft_tables(n2):
    n = FFT_N1 * n2
    f1 = jnp.arange(FFT_F1, dtype=jnp.int32)[:, None, None]
    a = jnp.arange(n2, dtype=jnp.int32)[None, :, None]
    b = jnp.arange(n2, dtype=jnp.int32)[None, None, :]
    ph_f = ((b * (f1 + FFT_N1 * a)) % n).astype(F32) * (2.0 * math.pi / n)
    cr, ci = jnp.cos(ph_f), -jnp.sin(ph_f)
    m_fwd = jnp.concatenate([jnp.concatenate([cr, -ci], axis=2),
                             jnp.concatenate([ci, cr], axis=2)], axis=1).astype(BF16)
    ph_i = ((a * (f1 + FFT_N1 * b)) % n).astype(F32) * (2.0 * math.pi / n)
    herm = jnp.where((f1 == 0) | (f1 == FFT_N1 // 2), 1.0, 2.0) / n
    dr, di = herm * jnp.cos(ph_i), herm * jnp.sin(ph_i)
    m_inv = jnp.concatenate([jnp.concatenate([dr, -di], axis=2),
                             jnp.concatenate([di, dr], axis=2)], axis=1).astype(BF16)
    return m_fwd, m_inv


def _hi_lo(x):
    hi = x.astype(BF16)
    return hi, (x - hi.astype(F32)).astype(BF16)


def _slow_tables():
    k = (jnp.arange(FFT_F1, dtype=jnp.int32)[:, None] * jnp.arange(FFT_N1, dtype=jnp.int32)[None, :]) % FFT_N1
    ph = k.astype(F32) * (2.0 * math.pi / FFT_N1)
    fwd = jnp.stack([jnp.cos(ph), -jnp.sin(ph)], axis=1).reshape(2 * FFT_F1, FFT_N1)
    inv = fwd[:, :FFT_N1 // 2].T
    return fwd, inv


def _dot3(a_hi, a_lo, x):
    x_hi, x_lo = _hi_lo(x)
    return _dot(a_hi, x_hi) + _dot(a_lo, x_hi) + _dot(a_hi, x_lo)


SLOW_UNROLL = 4
ROW_GROUP = 16
SLOW_TILES = 2


def _kron_rows(m):
    return jnp.kron(m, jnp.eye(ROW_GROUP, dtype=m.dtype))


def _slow_dft_kernel(n2, s1_count, f_ref, z_ref, a_ref):
    f_hl = f_ref[...]
    half = f_hl.shape[0] // 2
    st = z_ref.shape[2]

    def body(g, carry):
        s2 = pl.multiple_of(g * ROW_GROUP, ROW_GROUP)
        zs = jnp.concatenate(
            [_tiles_cat(z_ref, lambda q: (0, 0, q, pl.ds(s1 * n2 + s2, ROW_GROUP)), st) for s1 in range(s1_count)],
            axis=0).astype(BF16)
        a = _dot(f_hl, zs)
        a = (a[:half] + a[half:]).astype(BF16)
        for f in range(half // ROW_GROUP):
            _tiles_put(a_ref, lambda q: (0, q, pl.ds(f * n2 + s2, ROW_GROUP)), a[f * ROW_GROUP:(f + 1) * ROW_GROUP])
        return carry

    lax.fori_loop(0, n2 // ROW_GROUP, body, 0, unroll=min(SLOW_UNROLL, n2 // ROW_GROUP))


def _tiles_cat(ref, index, ntiles):
    return jnp.concatenate([ref[index(q) + (slice(None),)] for q in range(ntiles)], axis=1)


def _tiles_put(ref, index, value):
    for q in range(value.shape[1] // LANES):
        ref[index(q) + (slice(None),)] = value[:, q * LANES:(q + 1) * LANES]


def _slow_dft(z5, sel, s1_count):
    _, B, tiles, T, _ = z5.shape
    n2 = T // s1_count
    fwd, _ = _slow_tables()
    rows = 2 * FFT_F1
    f_hl = jnp.concatenate([_kron_rows(t) for t in _hi_lo(fwd[:, :s1_count])], axis=0)
    st = SLOW_TILES
    return pl.pallas_call(
        functools.partial(_slow_dft_kernel, n2, s1_count),
        grid=(B, tiles // st),
        in_specs=[pl.BlockSpec(f_hl.shape, lambda b, j: (0, 0)),
                  pl.BlockSpec((1, 1, st, T, LANES), lambda b, j: (sel, b, j, 0, 0))],
        out_specs=pl.BlockSpec((1, st, rows * n2, LANES), lambda b, j: (b, j, 0, 0)),
        out_shape=jax.ShapeDtypeStruct((B, tiles, rows * n2, LANES), BF16),
        compiler_params=_cparams(("parallel", "parallel")),
        name="hy_slow_dft",
    )(f_hl, z5)


def _spectrum_kernel(n2, a_ref, mf_ref, hr_ref, hi_ref):
    x = _dot(mf_ref[0], _tiles_cat(a_ref, lambda q: (0, q), a_ref.shape[1]))
    hr_ref[0] = x[:n2]
    hi_ref[0] = x[n2:]


def _spectrum(kc, m_fwd):
    order, tiles, n, _ = kc.shape
    n2 = n // FFT_N1
    C = tiles * LANES
    a = _slow_dft(kc.reshape(1, order, tiles, n, LANES), 0, FFT_N1)
    spec = lambda: pl.BlockSpec((1, n2, C), lambda f, o: (o, f, 0))
    return pl.pallas_call(
        functools.partial(_spectrum_kernel, n2),
        grid=(FFT_F1, order),
        in_specs=[pl.BlockSpec((1, tiles, 2 * n2, LANES), lambda f, o: (o, 0, f, 0)),
                  pl.BlockSpec((1, 2 * n2, 2 * n2), lambda f, o: (f, 0, 0))],
        out_specs=[spec(), spec()],
        out_shape=[jax.ShapeDtypeStruct((order, FFT_F1 * n2, C), F32)] * 2,
        compiler_params=_cparams(("parallel", "parallel")),
        name="hy_spectrum",
    )(a, m_fwd)


def _conv_mid_kernel(n2, a_ref, hr_ref, hi_ref, mf_ref, mi_ref, b_ref):
    x = _dot(mf_ref[0], _tiles_cat(a_ref, lambda q: (0, q), a_ref.shape[1]))
    xr, xi = x[:n2], x[n2:]
    hr, hi = hr_ref[0], hi_ref[0]
    y = jnp.concatenate([xr * hr - xi * hi, xr * hi + xi * hr], axis=0).astype(BF16)
    _tiles_put(b_ref, lambda q: (0, q), _dot(mi_ref[0], y).astype(BF16))


def _conv_mid(a, order, hr, hi, m_fwd, m_inv):
    B, tiles, total, _ = a.shape
    rows = total // FFT_F1
    n2 = rows // 2
    W = tiles * LANES
    blk = lambda: pl.BlockSpec((1, tiles, rows, LANES), lambda f, b: (b, 0, f, 0))
    return pl.pallas_call(
        functools.partial(_conv_mid_kernel, n2),
        grid=(FFT_F1, B),
        in_specs=[blk(),
                  pl.BlockSpec((1, n2, W), lambda f, b: (order, f, 0)),
                  pl.BlockSpec((1, n2, W), lambda f, b: (order, f, 0)),
                  pl.BlockSpec((1, rows, rows), lambda f, b: (f, 0, 0)),
                  pl.BlockSpec((1, rows, rows), lambda f, b: (f, 0, 0))],
        out_specs=blk(),
        out_shape=jax.ShapeDtypeStruct((B, tiles, total, LANES), BF16),
        compiler_params=_cparams(("parallel", "parallel")),
        name=f"hy_conv_mid{order}",
    )(a, hr, hi, m_fwd, m_inv)


IDFT_ROWS = tuple(f for f in range(2 * FFT_F1) if f not in (1, 2 * FFT_F1 - 1))


def _slow_idft_kernel(n2, g_ref, b_ref, z_ref, gate_ref, skip_ref, o_ref):
    g_hl = g_ref[...]
    half = g_hl.shape[0] // 2
    skip = skip_ref[...]
    st = b_ref.shape[1]

    def body(g, carry):
        t2 = pl.multiple_of(g * ROW_GROUP, ROW_GROUP)
        bs = jnp.concatenate(
            [_tiles_cat(b_ref, lambda q: (0, q, pl.ds(f * n2 + t2, ROW_GROUP)), st) for f in IDFT_ROWS], axis=0)
        y2 = _dot(g_hl, bs)
        y = y2[:half] + y2[half:]
        for t1 in range(half // ROW_GROUP):
            tok = lambda q: (0, 0, q, pl.ds(t1 * n2 + t2, ROW_GROUP))
            z = _tiles_cat(z_ref, tok, st)
            gate = _tiles_cat(gate_ref, tok, st)
            _tiles_put(o_ref, tok, gate * (y[t1 * ROW_GROUP:(t1 + 1) * ROW_GROUP] + skip * z))
        return carry

    lax.fori_loop(0, n2 // ROW_GROUP, body, 0, unroll=min(SLOW_UNROLL, n2 // ROW_GROUP))


def _slow_idft(bm, z5, zsel, g5, gsel, skip_row):
    B, tiles, total, _ = bm.shape
    t1 = FFT_N1 // 2
    n2 = total // (2 * FFT_F1)
    T = t1 * n2
    _, inv = _slow_tables()
    g_hl = jnp.concatenate([_kron_rows(t) for t in _hi_lo(inv[:, jnp.array(IDFT_ROWS)])], axis=0)
    st = SLOW_TILES
    tok = lambda sel: pl.BlockSpec((1, 1, st, T, LANES), lambda b, j: (sel, b, j, 0, 0))
    return pl.pallas_call(
        functools.partial(_slow_idft_kernel, n2),
        grid=(B, tiles // st),
        in_specs=[pl.BlockSpec(g_hl.shape, lambda b, j: (0, 0)),
                  pl.BlockSpec((1, st, total, LANES), lambda b, j: (b, j, 0, 0)),
                  tok(zsel), tok(gsel),
                  pl.BlockSpec((1, st * LANES), lambda b, j: (0, j))],
        out_specs=tok(0),
        out_shape=jax.ShapeDtypeStruct((1, B, tiles, T, LANES), F32),
        compiler_params=_cparams(("parallel", "parallel")),
        name="hy_slow_idft",
    )(g_hl, bm, z5, g5, skip_row.reshape(1, tiles * LANES))


def _longconv(z5, zsel, g5, gsel, order, skip, hr, hi, m_fwd, m_inv):
    a = _slow_dft(z5, zsel, FFT_N1 // 2)
    bm = _conv_mid(a, order, hr, hi, m_fwd, m_inv)
    return _slow_idft(bm, z5, zsel, g5, gsel, skip[order])


def _hyena(u3, fw1, fb1, fw2, fb2, fw3, fb3, fw4, sin_freq, skip):
    L = u3.shape[3]
    kc = _filters(L, fw1, fb1, fw2, fb2, fw3, fb3, fw4, sin_freq)
    m_fwd, m_inv = _fft_tables(2 * L // FFT_N1)
    hr, hi = _spectrum(kc, m_fwd)
    z1 = _longconv(u3, 0, u3, 1, 0, skip, hr, hi, m_fwd, m_inv)
    return _longconv(z1, 0, u3, 2, 1, skip, hr, hi, m_fwd, m_inv)[0]


def _rw_prep_kernel(u_ref, up_ref, un_ref, mu_ref, w0_ref, w2f_ref, w2b_ref, a0_ref, a2f_ref, a2b_ref,
                    g2_ref, kk_ref, ka_ref, rk_ref, p_ref, trif_ref, trib_ref, sel_ref,
                    v_o, ktf_o, rtf_o, khf_o, bhf_o, kbf_o, bbf_o, wtf_o,
                    ktb_o, rtb_o, khb_o, bhb_o, kbb_o, bbb_o, wtb_o, g_o, bonus_o):
    j = pl.program_id(1)
    nj = pl.num_programs(1)
    u = u_ref[0]
    tb = u.shape[0]
    prow = jnp.where(j == 0, 0.0, up_ref[0, 7:8, :])
    nrow = jnp.where(j == nj - 1, 0.0, un_ref[0, 0:1, :])
    row = lax.broadcasted_iota(jnp.int32, (tb, 1), 0)
    prev = jnp.where(row == 0, prow, pltpu.roll(u, 1, 0))
    nxt = jnp.where(row == tb - 1, nrow, pltpu.roll(u, tb - 1, 0))
    mu = mu_ref[...]
    xs = (1.0 - mu) * u + (0.5 * mu) * (prev + nxt)
    W = RW_WIDTH
    r, k, v = xs[:, 0:W], xs[:, W:2 * W], xs[:, 2 * W:3 * W]
    wd = jnp.tanh(xs[:, 3 * W:3 * W + 2 * RW_LORA]).astype(BF16)
    ad = xs[:, 3 * W + 2 * RW_LORA:3 * W + 4 * RW_LORA].astype(BF16)
    gd = jax.nn.sigmoid(xs[:, 3 * W + 4 * RW_LORA:]).astype(BF16)
    p = p_ref[...]
    kkn = k * kk_ref[...]
    nrm = jnp.sqrt(_segsum(kkn * kkn, p))
    kk = kkn / jnp.maximum(nrm, 1e-12)
    ka = ka_ref[...]
    lw_f = -RW_DECAY_SCALE * jax.nn.sigmoid(w0_ref[0:1, :] + _dot(wd, w2f_ref[...]))
    lw_b = -RW_DECAY_SCALE * jax.nn.sigmoid(w0_ref[1:2, :] + _dot(wd, w2b_ref[...]))
    a_f = jax.nn.sigmoid(a0_ref[0:1, :] + _dot(ad, a2f_ref[...]))
    a_b = jax.nn.sigmoid(a0_ref[1:2, :] + _dot(ad, a2b_ref[...]))
    kd_f = k * (1.0 + (a_f - 1.0) * ka)
    kd_b = k * (1.0 + (a_b - 1.0) * ka)
    v_o[0] = v.astype(BF16)
    sel = sel_ref[...]
    ncb = sel.shape[0]
    c = tb // ncb

    def scan_operands(lw, kd, b, tri, outs):
        parts = _split3(lw)
        cum = _dot_parts(tri, parts)
        wt = jnp.exp(_dot_parts(sel, parts))
        wt_tok = jnp.concatenate([jnp.broadcast_to(wt[j:j + 1], (c, wt.shape[1])) for j in range(ncb)], axis=0)
        e_neg = jnp.exp(-cum)
        kh = kd * e_neg
        bh = b * e_neg
        kt_o, rt_o, kh_o, bh_o, kb_o, bb_o, wt_o = outs
        kt_o[0] = (kk * jnp.exp(cum - lw)).astype(BF16)
        rt_o[0] = (r * jnp.exp(cum)).astype(BF16)
        kh_o[0] = kh.astype(BF16)
        bh_o[0] = bh.astype(BF16)
        kb_o[0] = (kh * wt_tok).astype(BF16)
        bb_o[0] = (bh * wt_tok).astype(BF16)
        wt_o[0] = wt

    scan_operands(lw_f, kd_f, kk * a_f, trif_ref[...], (ktf_o, rtf_o, khf_o, bhf_o, kbf_o, bbf_o, wtf_o))
    scan_operands(lw_b, kd_b, kk * a_b, trib_ref[...], (ktb_o, rtb_o, khb_o, bhb_o, kbb_o, bbb_o, wtb_o))
    g_o[0] = _dot(gd, g2_ref[...])
    bonus_o[0] = _segsum(r * (kd_f + kd_b) * rk_ref[...], p) * v


def _head_ones():
    h = jnp.arange(RW_WIDTH, dtype=jnp.int32) // RW_HEAD
    return (h[:, None] == h[None, :]).astype(BF16)


def _chunk_matrices(tb, c):
    t = jnp.arange(tb, dtype=jnp.int32)
    same = (t[:, None] // c) == (t[None, :] // c)
    tri_f = (same & (t[None, :] <= t[:, None])).astype(BF16)
    tri_b = (same & (t[None, :] >= t[:, None])).astype(BF16)
    sel = (jnp.arange(tb // c, dtype=jnp.int32)[:, None] == (t[None, :] // c)).astype(BF16)
    return tri_f, tri_b, sel


def _rw_prep(u_r, mu, w0, w2, a0, a2, g2, k_k, k_a, r_k, p_ones, tb=512):
    B, L, C = u_r.shape
    tb = min(tb, L)
    W = RW_WIDTH
    c = min(RW_CHUNK, L)
    ncb = tb // c
    tri_f, tri_b, sel = _chunk_matrices(tb, c)
    zeros = jnp.zeros((RW_LORA, W), F32)
    w2f = jnp.concatenate([w2[0], zeros], axis=0).astype(BF16)
    w2b = jnp.concatenate([zeros, w2[1]], axis=0).astype(BF16)
    a2f = jnp.concatenate([a2[0], zeros], axis=0).astype(BF16)
    a2b = jnp.concatenate([zeros, a2[1]], axis=0).astype(BF16)
    full = lambda shape: pl.BlockSpec(shape, lambda b, j: tuple(0 for _ in shape))
    tok = lambda: pl.BlockSpec((1, tb, W), lambda b, j: (b, j, 0))
    wts = lambda: pl.BlockSpec((1, ncb, W), lambda b, j: (b, j, 0))
    bf_tok = jax.ShapeDtypeStruct((B, L, W), BF16)
    f32_tok = jax.ShapeDtypeStruct((B, L, W), F32)
    wt_shape = jax.ShapeDtypeStruct((B, L // c, W), F32)
    g8 = tb // 8
    outs = pl.pallas_call(
        _rw_prep_kernel,
        grid=(B, L // tb),
        in_specs=[pl.BlockSpec((1, tb, C), lambda b, j: (b, j, 0)),
                  pl.BlockSpec((1, 8, C), lambda b, j: (b, jnp.maximum(j * g8 - 1, 0), 0)),
                  pl.BlockSpec((1, 8, C), lambda b, j: (b, jnp.minimum((j + 1) * g8, L // 8 - 1), 0)),
                  full((1, C)), full((2, W)), full((2 * RW_LORA, W)), full((2 * RW_LORA, W)),
                  full((2, W)), full((2 * RW_LORA, W)), full((2 * RW_LORA, W)),
                  full((RW_LORA_G, W)), full((1, W)), full((1, W)), full((1, W)), full((W, W)),
                  full((tb, tb)), full((tb, tb)), full((ncb, tb))],
        out_specs=[tok()] + ([tok() for _ in range(6)] + [wts()]) * 2 + [tok(), tok()],
        out_shape=[bf_tok] + ([bf_tok] * 6 + [wt_shape]) * 2 + [f32_tok, f32_tok],
        compiler_params=_cparams(("parallel", "parallel")),
        name="rw_prep",
    )(u_r, u_r, u_r, mu.reshape(1, C), w0, w2f, w2b, a0, a2f, a2b, g2.astype(BF16),
      k_k.reshape(1, W), k_a.reshape(1, W), r_k.reshape(1, W), p_ones, tri_f, tri_b, sel)
    v, g, bonus = outs[0], outs[15], outs[16]
    fwd_ops = tuple(outs[1:7]) + (v, outs[7])
    bwd_ops = tuple(outs[8:14]) + (v, outs[14])
    return fwd_ops, bwd_ops, g, bonus, ncb


def _bmm(a, b):
    return lax.dot_general(a, b, (((2,), (1,)), ((0,), (0,))), preferred_element_type=F32)


def _bmm_nt(a, b):
    return lax.dot_general(a, b, (((2,), (2,)), ((0,), (0,))), preferred_element_type=F32)


def _bmm_tn(a, b):
    return lax.dot_general(a, b, (((1,), (1,)), ((0,), (0,))), preferred_element_type=F32)


RW_QUAD = 4 * RW_HEAD
RW_SEQS = 4


def _block_diag(x, same_head):
    reps = RW_QUAD // x.shape[1]
    return jnp.where(same_head, jnp.concatenate([x] * reps, axis=1), jnp.zeros((), x.dtype))


def _quads(x_f, x_b):
    halves = lambda x: [x[n, :, s:s + RW_QUAD] for n in range(x.shape[0]) for s in range(0, x.shape[2], RW_QUAD)]
    return jnp.stack(halves(x_f) + halves(x_b), axis=0)


def _rw_scan_kernel(nc, ncb, ktf, rtf, khf, bhf, kbf, bbf, vf, wtf, ktb, rtb, khb, bhb, kbb, bbb, vb, wtb,
                    yf_ref, yb_ref, s_ref):
    i = pl.program_id(1)

    @pl.when(i == 0)
    def _():
        s_ref[...] = jnp.zeros_like(s_ref)

    kt, rt = _quads(ktf[...], ktb[...]), _quads(rtf[...], rtb[...])
    khat, bhat = _quads(khf[...], khb[...]), _quads(bhf[...], bhb[...])
    kbar, bbar = _quads(kbf[...], kbb[...]), _quads(bbf[...], bbb[...])
    v = _quads(vf[...], vb[...])
    wt = _quads(wtf[:, pl.ds(i % ncb, 1), :], wtb[:, pl.ds((nc - 1 - i) % ncb, 1), :])
    g, c, q = kt.shape
    ri = lax.broadcasted_iota(jnp.int32, (g, c, q), 1)
    ci = lax.broadcasted_iota(jnp.int32, (g, c, q), 2) % c
    rev = lax.broadcasted_iota(jnp.int32, (g, c, q), 0) >= g // 2
    ahead = jnp.where(rev, ri - ci, ci - ri)
    strict = ahead < 0
    causal = ahead <= 0
    eye = jnp.where(ahead == 0, 1.0, 0.0)
    same_head = (lax.broadcasted_iota(jnp.int32, (1, q, q), 1) // RW_HEAD
                 == lax.broadcasted_iota(jnp.int32, (1, q, q), 2) // RW_HEAD)
    bd = lambda x: _block_diag(x, same_head)

    lh = jnp.concatenate([kt, rt], axis=1)
    gk = _bmm_nt(lh, bd(khat))
    gb = _bmm_nt(lh, bd(bhat))
    a_k = jnp.where(strict, gk[:, :c], 0.0)
    b_k = jnp.where(causal, gk[:, c:], 0.0)
    a_b = jnp.where(strict, gb[:, :c], 0.0)
    b_b = jnp.where(causal, gb[:, c:], 0.0)
    p = -a_b
    tmat = eye + p
    p = _bmm(p.astype(BF16), bd(p.astype(BF16)))
    doublings = int(math.log2(c)) - 1
    for step in range(doublings):
        p_bd = bd(p.astype(BF16))
        if step == doublings - 1:
            tmat = tmat + _bmm(tmat.astype(BF16), p_bd)
        else:
            both = _bmm(jnp.concatenate([p, tmat], axis=1).astype(BF16), p_bd)
            tmat = tmat + both[:, c:]
            p = both[:, :c]
    s0 = s_ref[...]
    ks = _bmm_nt(lh, s0.astype(BF16))
    av = _bmm(jnp.concatenate([a_k, b_k], axis=1).astype(BF16), bd(v))
    ub = _bmm(tmat.astype(BF16), bd((ks[:, :c] + av[:, :c]).astype(BF16))).astype(BF16)
    y = ks[:, c:] + av[:, c:] - _bmm(b_b.astype(BF16), bd(ub))
    upd = _bmm_tn(jnp.concatenate([v, ub], axis=1), jnp.concatenate([kbar, -bbar], axis=1))
    s_ref[...] = s0 * wt + jnp.where(same_head, upd, 0.0)
    per_seq = yf_ref.shape[2] // q
    for n in range(yf_ref.shape[0]):
        yf_ref[n] = jnp.concatenate([y[n * per_seq + j] for j in range(per_seq)], axis=1)
        yb_ref[n] = jnp.concatenate([y[g // 2 + n * per_seq + j] for j in range(per_seq)], axis=1)


def _rw_scan(fwd_ops, bwd_ops, ncb):
    B, L, W = fwd_ops[0].shape
    c = min(RW_CHUNK, L)
    nc = L // c
    nb = RW_SEQS if B % RW_SEQS == 0 else 1
    fwd = lambda: pl.BlockSpec((nb, c, W), lambda b, i: (b, i, 0))
    bwd = lambda: pl.BlockSpec((nb, c, W), lambda b, i: (b, nc - 1 - i, 0))
    wt_f = pl.BlockSpec((nb, ncb, W), lambda b, i: (b, i // ncb, 0))
    wt_b = pl.BlockSpec((nb, ncb, W), lambda b, i: (b, (nc - 1 - i) // ncb, 0))
    return pl.pallas_call(
        functools.partial(_rw_scan_kernel, nc, ncb),
        grid=(B // nb, nc),
        in_specs=[fwd() for _ in range(7)] + [wt_f] + [bwd() for _ in range(7)] + [wt_b],
        out_specs=[fwd(), bwd()],
        out_shape=[jax.ShapeDtypeStruct((B, L, W), F32)] * 2,
        scratch_shapes=[pltpu.VMEM((2 * nb * W // RW_QUAD, RW_QUAD, RW_QUAD), F32)],
        compiler_params=_cparams(("parallel", "arbitrary")),
        name="rw_scan",
    )(*fwd_ops, *bwd_ops)


def _merge_kernel(x_ref, yh_ref, yf_ref, yb_ref, bonus_ref, g_ref, gates_ref, p_ref, gnw_ref, gnb_ref,
                  why_ref, wrw_ref, wo_ref, lnw_ref, lnb_ref, o_ref):
    p = p_ref[...]
    y = yf_ref[...] + yb_ref[...]
    mu = _segsum(y, p) * (1.0 / RW_HEAD)
    yc = y - mu
    var = _segsum(yc * yc, p) * (1.0 / RW_HEAD)
    yn = yc * lax.rsqrt(var + RW_GN_EPS) * gnw_ref[...] + gnb_ref[...]
    y_r = (yn + bonus_ref[...]) * g_ref[...]
    ph = _dot(_tiles_cat(yh_ref, lambda q: (0, q), yh_ref.shape[1]).astype(BF16), why_ref[...])
    pr = _dot(y_r.astype(BF16), wrw_ref[...])
    gates = jax.nn.sigmoid(gates_ref[...])
    m = gates[:, :D_MODEL] * ph + gates[:, D_MODEL:] * pr
    mix = _dot(m.astype(BF16), wo_ref[...])
    o_ref[...] = _layer_norm(DN_ALPHA * x_ref[...] + mix, lnw_ref[...], lnb_ref[...])


def _merge(x2, yh, yf, yb, bonus, g, gates, p_ones, gn_w, gn_b, w_hy_out, w_rw_out, w_o, ln_w, ln_b, tm=512):
    m, d = x2.shape
    _, tiles, L, _ = yh.shape
    tm = min(tm, L)
    per_seq = L // tm
    W = RW_WIDTH
    row = lambda width: pl.BlockSpec((tm, width), lambda i: (i, 0))
    full = lambda shape: pl.BlockSpec(shape, lambda i: tuple(0 for _ in shape))
    return pl.pallas_call(
        _merge_kernel,
        grid=(m // tm,),
        in_specs=[row(d), pl.BlockSpec((1, tiles, tm, LANES), lambda i: (i // per_seq, 0, i % per_seq, 0)),
                  row(W), row(W), row(W), row(W), row(GATE_COLS),
                  full((W, W)), full((1, W)), full((1, W)),
                  full((HY_WIDTH, d)), full((W, d)), full((d, d)), full((1, d)), full((1, d))],
        out_specs=row(d),
        out_shape=jax.ShapeDtypeStruct((m, d), F32),
        compiler_params=_cparams(("parallel",)),
        name="merge_ln1",
    )(x2, yh, yf, yb, bonus, g, gates, p_ones, gn_w.reshape(1, W), gn_b.reshape(1, W),
      w_hy_out.astype(BF16), w_rw_out.astype(BF16), w_o.astype(BF16), ln_w.reshape(1, d), ln_b.reshape(1, d))


def _ffn_kernel(x_ref, wg_ref, wu_ref, wd_ref, lnw_ref, lnb_ref, o_ref):
    x = x_ref[...]
    xb = x.astype(BF16)
    hidden = jax.nn.silu(_dot(xb, wg_ref[...])) * _dot(xb, wu_ref[...])
    ffn = _dot(hidden.astype(BF16), wd_ref[...])
    o_ref[...] = _layer_norm(DN_ALPHA * x + ffn, lnw_ref[...], lnb_ref[...])


def _ffn(x2, w_gate, w_up, w_down, ln_w, ln_b, tm=512):
    m, d = x2.shape
    tm = min(tm, m)
    fh = w_gate.shape[1]
    resident = lambda shape: pl.BlockSpec(shape, lambda i: (0, 0), pipeline_mode=pl.Buffered(1))
    return pl.pallas_call(
        _ffn_kernel,
        grid=(m // tm,),
        in_specs=[pl.BlockSpec((tm, d), lambda i: (i, 0)),
                  resident((d, fh)), resident((d, fh)), resident((fh, d)),
                  resident((1, d)), resident((1, d))],
        out_specs=pl.BlockSpec((tm, d), lambda i: (i, 0)),
        out_shape=jax.ShapeDtypeStruct((m, d), F32),
        compiler_params=_cparams(("parallel",)),
        name="ffn_ln2",
    )(x2, w_gate.astype(BF16), w_up.astype(BF16), w_down.astype(BF16), ln_w.reshape(1, d), ln_b.reshape(1, d))


def _layer(x, w_in, hy_conv_w, hy_conv_b, hy_filt_w1, hy_filt_b1, hy_filt_w2, hy_filt_b2,
           hy_filt_w3, hy_filt_b3, hy_filt_w4, hy_sin_freq, hy_skip, rw_mu, rw_w0, rw_w2,
           rw_a0, rw_a2, rw_g2, rw_k_k, rw_k_a, rw_r_k, rw_gn_w, rw_gn_b, w_hy_out, w_rw_out,
           w_o, ln1_w, ln1_b, ffn_w_gate, ffn_w_up, ffn_w_down, ln2_w, ln2_b):
    B, L, D = x.shape
    x2 = x.reshape(B * L, D)
    u3, u_r, gates = _in_proj(x2, L, w_in.astype(BF16), hy_conv_w, hy_conv_b)
    y_h = _hyena(u3, hy_filt_w1, hy_filt_b1, hy_filt_w2, hy_filt_b2, hy_filt_w3, hy_filt_b3, hy_filt_w4,
                 hy_sin_freq, hy_skip)
    p_ones = _head_ones()
    fwd_ops, bwd_ops, g, bonus, ncb = _rw_prep(
        u_r.reshape(B, L, RW_COLS), rw_mu, rw_w0, rw_w2, rw_a0, rw_a2, rw_g2, rw_k_k, rw_k_a, rw_r_k, p_ones)
    yf, yb = _rw_scan(fwd_ops, bwd_ops, ncb)
    flat = lambda a: a.reshape(B * L, a.shape[-1])
    h = _merge(x2, y_h, flat(yf), flat(yb), flat(bonus), flat(g), gates, p_ones, rw_gn_w, rw_gn_b,
               w_hy_out, w_rw_out, w_o, ln1_w, ln1_b)
    out = _ffn(h, ffn_w_gate, ffn_w_up, ffn_w_down, ln2_w, ln2_b)
    return out.reshape(B, L, D)


def kernel(x, w_in, hy_conv_w, hy_conv_b, hy_filt_w1, hy_filt_b1, hy_filt_w2, hy_filt_b2, hy_filt_w3, hy_filt_b3, hy_filt_w4, hy_sin_freq, hy_skip, rw_mu, rw_w0, rw_w2, rw_a0, rw_a2, rw_g2, rw_k_k, rw_k_a, rw_r_k, rw_gn_w, rw_gn_b, w_hy_out, w_rw_out, w_o, ln1_w, ln1_b, ffn_w_gate, ffn_w_up, ffn_w_down, ln2_w, ln2_b):
    params = (w_in, hy_conv_w, hy_conv_b, hy_filt_w1, hy_filt_b1, hy_filt_w2, hy_filt_b2, hy_filt_w3,
              hy_filt_b3, hy_filt_w4, hy_sin_freq, hy_skip, rw_mu, rw_w0, rw_w2, rw_a0, rw_a2, rw_g2,
              rw_k_k, rw_k_a, rw_r_k, rw_gn_w, rw_gn_b, w_hy_out, w_rw_out, w_o, ln1_w, ln1_b,
              ffn_w_gate, ffn_w_up, ffn_w_down, ln2_w, ln2_b)
    for l in range(w_in.shape[0]):
        x = _layer(x, *[p[l] for p in params])
    return x
```

```python
import functools
import math

import jax
import jax.numpy as jnp
from jax import lax
from jax.experimental import pallas as pl
from jax.experimental.pallas import tpu as pltpu

F32 = jnp.float32
BF16 = jnp.bfloat16
HIGHEST = lax.Precision.HIGHEST

D_MODEL = 1024
HY_WIDTH = 512
HY_ORDER = 2
HY_BANDS = 16
HY_FILT_HIDDEN = 64
HY_FAST_DECAY = 0.3
HY_SLOW_DECAY = 1.5
HY_DECAY_TARGET = 1e-2
HY_MAX_DECAY = math.log(HY_DECAY_TARGET) / HY_FAST_DECAY
HY_MIN_DECAY = math.log(HY_DECAY_TARGET) / HY_SLOW_DECAY
HY_COLS = 3 * HY_WIDTH
RW_WIDTH = 512
RW_HEAD = 64
RW_HEADS = RW_WIDTH // RW_HEAD
RW_LORA = 64
RW_LORA_G = 128
RW_GN_EPS = 64e-5
RW_COLS = 3 * RW_WIDTH + 4 * RW_LORA + RW_LORA_G
GATE_COLS = 2 * D_MODEL
FFN_HIDDEN = ((8 * D_MODEL + 3 * 256 - 1) // (3 * 256)) * 256
DEPTH = 1
DN_ALPHA = (2.0 * DEPTH) ** 0.25
LN_EPS = 1e-5
RW_DECAY_SCALE = math.exp(-0.5)

LANES = 128
VMEM_LIMIT = 56 * 1024 * 1024

FFT_N1 = 32
FFT_F1 = FFT_N1 // 2 + 1
RW_CHUNK = 64


def _cparams(sem, vmem=VMEM_LIMIT):
    return pltpu.CompilerParams(dimension_semantics=sem, vmem_limit_bytes=vmem)


def _dot(a, b, precision=None):
    return jnp.dot(a, b, preferred_element_type=F32, precision=precision)


def _dot_nt(a, b):
    return lax.dot_general(a, b, (((1,), (1,)), ((), ())), preferred_element_type=F32)


def _dot_tn(a, b):
    return lax.dot_general(a, b, (((0,), (0,)), ((), ())), preferred_element_type=F32)


def _layer_norm(h, w, b):
    mu = jnp.mean(h, axis=-1, keepdims=True)
    c = h - mu
    var = jnp.mean(c * c, axis=-1, keepdims=True)
    return c * lax.rsqrt(var + LN_EPS) * w + b


def _hi_lo(x):
    hi = x.astype(BF16)
    return hi, (x - hi.astype(F32)).astype(BF16)


def _segsum(x, p, terms=2):
    return _dot_parts_t(_hi_lo(x)[:terms], p)


def _dot_parts_t(parts, m):
    out = _dot(parts[0], m)
    for part in parts[1:]:
        out = out + _dot(part, m)
    return out


def _dot_parts(m, parts):
    out = _dot(m, parts[0])
    for part in parts[1:]:
        out = out + _dot(m, part)
    return out


HY_TILES = HY_WIDTH // LANES
HALO = 8


def _in_proj_kernel(per_seq, x_ref, xp_ref, xn_ref, w_ref, cw_ref, cb_ref, u3_ref, ur_ref, g_ref):
    pos = pl.program_id(0) % per_seq
    w_h = w_ref[:, :HY_COLS]
    xb = x_ref[...].astype(BF16)
    tm = xb.shape[0]
    ext = _dot(jnp.concatenate([xp_ref[...].astype(BF16), xb, xn_ref[...].astype(BF16)], axis=0), w_h)
    u = ext[HALO:HALO + tm]
    row = lax.broadcasted_iota(jnp.int32, (tm, 1), 0)
    prev = jnp.where((row == 0) & (pos == 0), 0.0, ext[HALO - 1:HALO - 1 + tm])
    nxt = jnp.where((row == tm - 1) & (pos == per_seq - 1), 0.0, ext[HALO + 1:HALO + 1 + tm])
    conv = cw_ref[0:1, :] * prev + cw_ref[1:2, :] * u + cw_ref[2:3, :] * nxt + cb_ref[...]
    for s in range(HY_COLS // HY_WIDTH):
        for q in range(HY_TILES):
            lo = s * HY_WIDTH + q * LANES
            u3_ref[s, 0, q] = conv[:, lo:lo + LANES]
    ur_ref[...] = _dot(xb, w_ref[:, HY_COLS:HY_COLS + RW_COLS])
    g_ref[...] = _dot(xb, w_ref[:, HY_COLS + RW_COLS:])


def _in_proj(x2, L, w_in_bf, conv_w, conv_b, tm=512):
    m, d = x2.shape
    n = w_in_bf.shape[1]
    tm = min(tm, L)
    per_seq = L // tm
    groups = tm // HALO
    resident = lambda shape: pl.BlockSpec(shape, lambda i: (0, 0), pipeline_mode=pl.Buffered(1))
    slabs = HY_COLS // HY_WIDTH
    return pl.pallas_call(
        functools.partial(_in_proj_kernel, per_seq),
        grid=(m // tm,),
        in_specs=[pl.BlockSpec((tm, d), lambda i: (i, 0)),
                  pl.BlockSpec((HALO, d), lambda i: (jnp.maximum(i * groups - 1, 0), 0)),
                  pl.BlockSpec((HALO, d), lambda i: (jnp.minimum((i + 1) * groups, m // HALO - 1), 0)),
                  resident((d, n)), resident((3, HY_COLS)), resident((1, HY_COLS))],
        out_specs=[pl.BlockSpec((slabs, 1, HY_TILES, tm, LANES), lambda i: (0, i // per_seq, 0, i % per_seq, 0)),
                   pl.BlockSpec((tm, RW_COLS), lambda i: (i, 0)),
                   pl.BlockSpec((tm, GATE_COLS), lambda i: (i, 0))],
        out_shape=[jax.ShapeDtypeStruct((slabs, m // L, HY_TILES, L, LANES), F32),
                   jax.ShapeDtypeStruct((m, RW_COLS), F32),
                   jax.ShapeDtypeStruct((m, GATE_COLS), F32)],
        compiler_params=_cparams(("parallel",)),
        name="in_proj",
    )(x2, x2, x2, w_in_bf, conv_w, conv_b.reshape(1, HY_COLS))


def _filter_kernel(L, tb, w1t_ref, w1c_ref, w1s_ref, b1_ref, w2_ref, b2_ref, w3_ref, b3_ref,
                   w4_ref, sf_ref, freq_ref, delta_ref, kc_ref):
    n = 2 * L
    i = pl.program_id(0)
    row = i * tb + lax.broadcasted_iota(jnp.int32, (tb, 1), 0)
    second = row >= L
    pos = jnp.where(second, n - row, row).astype(F32)
    t = pos / float(L - 1)
    ang = (2.0 * math.pi * pos / float(L)) * freq_ref[...]
    pre = t * w1t_ref[...] + _dot(jnp.cos(ang), w1c_ref[...], HIGHEST) \
        - _dot(jnp.sin(ang), w1s_ref[...], HIGHEST) + b1_ref[...]
    h = jnp.sin(sf_ref[0:1, :] * pre)
    h = jnp.sin(sf_ref[1:2, :] * (_dot(h, w2_ref[...], HIGHEST) + b2_ref[...]))
    h = jnp.sin(sf_ref[2:3, :] * (_dot(h, w3_ref[...], HIGHEST) + b3_ref[...]))
    h4 = _dot(h, w4_ref[...], HIGHEST)
    window = jnp.exp(-t * delta_ref[...])
    keep = jnp.where(row == L, 0.0, 1.0)
    first = jnp.where(row == 0, 1.0, 0.0)
    for o in range(HY_ORDER):
        base = o * 2 * HY_WIDTH
        fwd = h4[:, base:base + HY_WIDTH]
        bwd = h4[:, base + HY_WIDTH:base + 2 * HY_WIDTH]
        kc = window * (jnp.where(second, bwd, fwd) * keep + first * bwd)
        for q in range(HY_TILES):
            kc_ref[o, q] = kc[:, q * LANES:(q + 1) * LANES]


def _filters(L, fw1, fb1, fw2, fb2, fw3, fb3, fw4, sin_freq, tb=512):
    n = 2 * L
    tb = min(tb, n)
    freqs = jnp.linspace(1e-4, HY_BANDS - 1, HY_BANDS, dtype=F32).reshape(1, HY_BANDS)
    deltas = jnp.abs(jnp.linspace(HY_MIN_DECAY, HY_MAX_DECAY, HY_WIDTH, dtype=F32)).reshape(1, HY_WIDTH)
    hid = HY_FILT_HIDDEN
    full = lambda shape: pl.BlockSpec(shape, lambda i: tuple(0 for _ in shape))
    return pl.pallas_call(
        functools.partial(_filter_kernel, L, tb),
        grid=(n // tb,),
        in_specs=[full((1, hid)), full((HY_BANDS, hid)), full((HY_BANDS, hid)), full((1, hid)),
                  full((hid, hid)), full((1, hid)), full((hid, hid)), full((1, hid)),
                  full((hid, HY_ORDER * 2 * HY_WIDTH)), full((3, hid)),
                  full((1, HY_BANDS)), full((1, HY_WIDTH))],
        out_specs=pl.BlockSpec((HY_ORDER, HY_TILES, tb, LANES), lambda i: (0, 0, i, 0)),
        out_shape=jax.ShapeDtypeStruct((HY_ORDER, HY_TILES, n, LANES), F32),
        compiler_params=_cparams(("parallel",)),
        name="hy_filters",
    )(fw1[0:1], fw1[1:1 + HY_BANDS], fw1[1 + HY_BANDS:], fb1.reshape(1, hid), fw2, fb2.reshape(1, hid),
      fw3, fb3.reshape(1, hid), fw4, sin_freq, freqs, deltas)


def _fft_tables(n2):
    n = FFT_N1 * n2
    f1 = jnp.arange(FFT_F1, dtype=jnp.int32)[:, None]
    a = jnp.arange(n2, dtype=jnp.int32)[:, None]
    b = jnp.arange(n2, dtype=jnp.int32)[None, :]
    ph_base = ((a * b) % n2).astype(F32) * (2.0 * math.pi / n2)
    ph_tw = (f1 * b).astype(F32) * (2.0 * math.pi / n)
    br, bi = jnp.cos(ph_base)[None], -jnp.sin(ph_base)[None]
    tr, ti = jnp.cos(ph_tw)[:, None, :], -jnp.sin(ph_tw)[:, None, :]
    cr, ci = br * tr - bi * ti, br * ti + bi * tr
    m_fwd = jnp.concatenate([jnp.concatenate([cr, -ci], axis=2),
                             jnp.concatenate([ci, cr], axis=2)], axis=1).astype(BF16)
    herm = (jnp.where((f1 == 0) | (f1 == FFT_N1 // 2), 1.0, 2.0) / n)[:, :, None]
    dr, di = herm * jnp.swapaxes(cr, 1, 2), -herm * jnp.swapaxes(ci, 1, 2)
    m_inv = jnp.concatenate([jnp.concatenate([dr, -di], axis=2),
                             jnp.concatenate([di, dr], axis=2)], axis=1).astype(BF16)
    return m_fwd, m_inv


def _slow_tables():
    k = (jnp.arange(FFT_F1, dtype=jnp.int32)[:, None] * jnp.arange(FFT_N1, dtype=jnp.int32)[None, :]) % FFT_N1
    ph = k.astype(F32) * (2.0 * math.pi / FFT_N1)
    fwd = jnp.stack([jnp.cos(ph), -jnp.sin(ph)], axis=1).reshape(2 * FFT_F1, FFT_N1)
    inv = fwd[:, :FFT_N1 // 2].T
    return fwd, inv


def _dot3(a_hi, a_lo, x):
    x_hi, x_lo = _hi_lo(x)
    return _dot(a_hi, x_hi) + _dot(a_lo, x_hi) + _dot(a_hi, x_lo)


SLOW_UNROLL = 4
ROW_GROUP = 16
SLOW_TILES = 2


def _kron_rows(m):
    return jnp.kron(m, jnp.eye(ROW_GROUP, dtype=m.dtype))


def _slow_dft_kernel(n2, s1_count, f_ref, z_ref, a_ref):
    f_hl = f_ref[...]
    half = f_hl.shape[0] // 2
    st = z_ref.shape[2]

    def body(g, carry):
        s2 = pl.multiple_of(g * ROW_GROUP, ROW_GROUP)
        zs = jnp.concatenate(
            [_tiles_cat(z_ref, lambda q: (0, 0, q, pl.ds(s1 * n2 + s2, ROW_GROUP)), st) for s1 in range(s1_count)],
            axis=0).astype(BF16)
        a = _dot(f_hl, zs)
        a = (a[:half] + a[half:]).astype(BF16)
        for f in range(half // ROW_GROUP):
            _tiles_put(a_ref, lambda q: (0, q, pl.ds(f * n2 + s2, ROW_GROUP)), a[f * ROW_GROUP:(f + 1) * ROW_GROUP])
        return carry

    lax.fori_loop(0, n2 // ROW_GROUP, body, 0, unroll=min(SLOW_UNROLL, n2 // ROW_GROUP))


def _tiles_cat(ref, index, ntiles):
    return jnp.concatenate([ref[index(q) + (slice(None),)] for q in range(ntiles)], axis=1)


def _tiles_put(ref, index, value):
    for q in range(value.shape[1] // LANES):
        ref[index(q) + (slice(None),)] = value[:, q * LANES:(q + 1) * LANES]


def _slow_dft(z5, sel, s1_count):
    _, B, tiles, T, _ = z5.shape
    n2 = T // s1_count
    fwd, _ = _slow_tables()
    rows = 2 * FFT_F1
    f_hl = jnp.concatenate([_kron_rows(t) for t in _hi_lo(fwd[:, :s1_count])], axis=0)
    st = SLOW_TILES
    return pl.pallas_call(
        functools.partial(_slow_dft_kernel, n2, s1_count),
        grid=(B, tiles // st),
        in_specs=[pl.BlockSpec(f_hl.shape, lambda b, j: (0, 0)),
                  pl.BlockSpec((1, 1, st, T, LANES), lambda b, j: (sel, b, j, 0, 0))],
        out_specs=pl.BlockSpec((1, st, rows * n2, LANES), lambda b, j: (b, j, 0, 0)),
        out_shape=jax.ShapeDtypeStruct((B, tiles, rows * n2, LANES), BF16),
        compiler_params=_cparams(("parallel", "parallel")),
        name="hy_slow_dft",
    )(f_hl, z5)


def _spectrum_kernel(n2, a_ref, mf_ref, hr_ref, hi_ref):
    x = _dot(mf_ref[0], _tiles_cat(a_ref, lambda q: (0, q), a_ref.shape[1]))
    hr_ref[0] = x[:n2]
    hi_ref[0] = x[n2:]


def _spectrum(kc, m_fwd):
    order, tiles, n, _ = kc.shape
    n2 = n // FFT_N1
    C = tiles * LANES
    a = _slow_dft(kc.reshape(1, order, tiles, n, LANES), 0, FFT_N1)
    spec = lambda: pl.BlockSpec((1, n2, C), lambda f, o: (o, f, 0))
    return pl.pallas_call(
        functools.partial(_spectrum_kernel, n2),
        grid=(FFT_F1, order),
        in_specs=[pl.BlockSpec((1, tiles, 2 * n2, LANES), lambda f, o: (o, 0, f, 0)),
                  pl.BlockSpec((1, 2 * n2, 2 * n2), lambda f, o: (f, 0, 0))],
        out_specs=[spec(), spec()],
        out_shape=[jax.ShapeDtypeStruct((order, FFT_F1 * n2, C), F32)] * 2,
        compiler_params=_cparams(("parallel", "parallel")),
        name="hy_spectrum",
    )(a, m_fwd)


MID_SEQS = 2


def _conv_mid_kernel(n2, a_ref, hr_ref, hi_ref, mf_ref, mi_ref, b_ref):
    hr, hi = hr_ref[0], hi_ref[0]
    for n in range(a_ref.shape[0]):
        x = _dot(mf_ref[0], _tiles_cat(a_ref, lambda q: (n, q), a_ref.shape[1]))
        xr, xi = x[:n2], x[n2:]
        y = jnp.concatenate([xr * hr - xi * hi, xr * hi + xi * hr], axis=0).astype(BF16)
        _tiles_put(b_ref, lambda q: (n, q), _dot(mi_ref[0], y).astype(BF16))


def _conv_mid(a, order, hr, hi, m_fwd, m_inv):
    B, tiles, total, _ = a.shape
    rows = total // FFT_F1
    n2 = rows // 2
    W = tiles * LANES
    nb = MID_SEQS if B % MID_SEQS == 0 else 1
    blk = lambda: pl.BlockSpec((nb, tiles, rows, LANES), lambda f, b: (b, 0, f, 0))
    return pl.pallas_call(
        functools.partial(_conv_mid_kernel, n2),
        grid=(FFT_F1, B // nb),
        in_specs=[blk(),
                  pl.BlockSpec((1, n2, W), lambda f, b: (order, f, 0)),
                  pl.BlockSpec((1, n2, W), lambda f, b: (order, f, 0)),
                  pl.BlockSpec((1, rows, rows), lambda f, b: (f, 0, 0)),
                  pl.BlockSpec((1, rows, rows), lambda f, b: (f, 0, 0))],
        out_specs=blk(),
        out_shape=jax.ShapeDtypeStruct((B, tiles, total, LANES), BF16),
        compiler_params=_cparams(("parallel", "parallel")),
        name=f"hy_conv_mid{order}",
    )(a, hr, hi, m_fwd, m_inv)


IDFT_ROWS = tuple(f for f in range(2 * FFT_F1) if f not in (1, 2 * FFT_F1 - 1))


def _slow_idft_kernel(n2, g_ref, b_ref, z_ref, gate_ref, skip_ref, o_ref):
    g_hl = g_ref[...]
    half = g_hl.shape[0] // 2
    skip = skip_ref[...]
    st = b_ref.shape[1]

    def body(g, carry):
        t2 = pl.multiple_of(g * ROW_GROUP, ROW_GROUP)
        bs = jnp.concatenate(
            [_tiles_cat(b_ref, lambda q: (0, q, pl.ds(f * n2 + t2, ROW_GROUP)), st) for f in IDFT_ROWS], axis=0)
        y2 = _dot(g_hl, bs)
        y = y2[:half] + y2[half:]
        for t1 in range(half // ROW_GROUP):
            tok = lambda q: (0, 0, q, pl.ds(t1 * n2 + t2, ROW_GROUP))
            z = _tiles_cat(z_ref, tok, st)
            gate = _tiles_cat(gate_ref, tok, st)
            _tiles_put(o_ref, tok, gate * (y[t1 * ROW_GROUP:(t1 + 1) * ROW_GROUP] + skip * z))
        return carry

    lax.fori_loop(0, n2 // ROW_GROUP, body, 0, unroll=min(SLOW_UNROLL, n2 // ROW_GROUP))


def _slow_idft(bm, z5, zsel, g5, gsel, skip_row):
    B, tiles, total, _ = bm.shape
    t1 = FFT_N1 // 2
    n2 = total // (2 * FFT_F1)
    T = t1 * n2
    _, inv = _slow_tables()
    g_hl = jnp.concatenate([_kron_rows(t) for t in _hi_lo(inv[:, jnp.array(IDFT_ROWS)])], axis=0)
    st = SLOW_TILES
    tok = lambda sel: pl.BlockSpec((1, 1, st, T, LANES), lambda b, j: (sel, b, j, 0, 0))
    return pl.pallas_call(
        functools.partial(_slow_idft_kernel, n2),
        grid=(B, tiles // st),
        in_specs=[pl.BlockSpec(g_hl.shape, lambda b, j: (0, 0)),
                  pl.BlockSpec((1, st, total, LANES), lambda b, j: (b, j, 0, 0)),
                  tok(zsel), tok(gsel),
                  pl.BlockSpec((1, st * LANES), lambda b, j: (0, j))],
        out_specs=tok(0),
        out_shape=jax.ShapeDtypeStruct((1, B, tiles, T, LANES), F32),
        compiler_params=_cparams(("parallel", "parallel")),
        name="hy_slow_idft",
    )(g_hl, bm, z5, g5, skip_row.reshape(1, tiles * LANES))


def _longconv(z5, zsel, g5, gsel, order, skip, hr, hi, m_fwd, m_inv):
    a = _slow_dft(z5, zsel, FFT_N1 // 2)
    bm = _conv_mid(a, order, hr, hi, m_fwd, m_inv)
    return _slow_idft(bm, z5, zsel, g5, gsel, skip[order])


def _hyena(u3, fw1, fb1, fw2, fb2, fw3, fb3, fw4, sin_freq, skip):
    L = u3.shape[3]
    kc = _filters(L, fw1, fb1, fw2, fb2, fw3, fb3, fw4, sin_freq)
    m_fwd, m_inv = _fft_tables(2 * L // FFT_N1)
    hr, hi = _spectrum(kc, m_fwd)
    z1 = _longconv(u3, 0, u3, 1, 0, skip, hr, hi, m_fwd, m_inv)
    return _longconv(z1, 0, u3, 2, 1, skip, hr, hi, m_fwd, m_inv)[0]


def _rw_prep_kernel(u_ref, up_ref, un_ref, mu_ref, w0_ref, w2f_ref, w2b_ref, a0_ref, a2f_ref, a2b_ref,
                    g2_ref, kk_ref, ka_ref, rk_ref, p_ref, trif_ref, trib_ref, sel_ref,
                    v_o, ktf_o, rtf_o, khf_o, bhf_o, kbf_o, bbf_o, wtf_o,
                    ktb_o, rtb_o, khb_o, bhb_o, kbb_o, bbb_o, wtb_o, g_o, bonus_o):
    j = pl.program_id(1)
    nj = pl.num_programs(1)
    u = u_ref[0]
    tb = u.shape[0]
    prow = jnp.where(j == 0, 0.0, up_ref[0, 7:8, :])
    nrow = jnp.where(j == nj - 1, 0.0, un_ref[0, 0:1, :])
    row = lax.broadcasted_iota(jnp.int32, (tb, 1), 0)
    prev = jnp.where(row == 0, prow, pltpu.roll(u, 1, 0))
    nxt = jnp.where(row == tb - 1, nrow, pltpu.roll(u, tb - 1, 0))
    mu = mu_ref[...]
    xs = (1.0 - mu) * u + (0.5 * mu) * (prev + nxt)
    W = RW_WIDTH
    r, k, v = xs[:, 0:W], xs[:, W:2 * W], xs[:, 2 * W:3 * W]
    wd = jnp.tanh(xs[:, 3 * W:3 * W + 2 * RW_LORA]).astype(BF16)
    ad = xs[:, 3 * W + 2 * RW_LORA:3 * W + 4 * RW_LORA].astype(BF16)
    gd = jax.nn.sigmoid(xs[:, 3 * W + 4 * RW_LORA:]).astype(BF16)
    p = p_ref[...]
    kkn = k * kk_ref[...]
    kk = kkn * lax.rsqrt(jnp.maximum(_segsum(kkn * kkn, p, terms=1), 1e-24))
    ka = ka_ref[...]
    lw_f = -RW_DECAY_SCALE * jax.nn.sigmoid(w0_ref[0:1, :] + _dot(wd, w2f_ref[...]))
    lw_b = -RW_DECAY_SCALE * jax.nn.sigmoid(w0_ref[1:2, :] + _dot(wd, w2b_ref[...]))
    a_f = jax.nn.sigmoid(a0_ref[0:1, :] + _dot(ad, a2f_ref[...]))
    a_b = jax.nn.sigmoid(a0_ref[1:2, :] + _dot(ad, a2b_ref[...]))
    kd_f = k * (1.0 + (a_f - 1.0) * ka)
    kd_b = k * (1.0 + (a_b - 1.0) * ka)
    v_o[0] = v.astype(BF16)
    sel = sel_ref[...]
    ncb = sel.shape[0]
    c = tb // ncb

    def scan_operands(lw, kd, b, tri, outs):
        parts = _hi_lo(lw)
        cum = _dot_parts(tri, parts)
        wt = jnp.exp(_dot_parts(sel, parts))
        wt_tok = jnp.concatenate([jnp.broadcast_to(wt[j:j + 1], (c, wt.shape[1])) for j in range(ncb)], axis=0)
        e_neg = jnp.exp(-cum)
        kh = kd * e_neg
        bh = b * e_neg
        kt_o, rt_o, kh_o, bh_o, kb_o, bb_o, wt_o = outs
        kt_o[0] = (kk * jnp.exp(cum - lw)).astype(BF16)
        rt_o[0] = (r * jnp.exp(cum)).astype(BF16)
        kh_o[0] = kh.astype(BF16)
        bh_o[0] = bh.astype(BF16)
        kb_o[0] = (kh * wt_tok).astype(BF16)
        bb_o[0] = (bh * wt_tok).astype(BF16)
        wt_o[0] = wt

    scan_operands(lw_f, kd_f, kk * a_f, trif_ref[...], (ktf_o, rtf_o, khf_o, bhf_o, kbf_o, bbf_o, wtf_o))
    scan_operands(lw_b, kd_b, kk * a_b, trib_ref[...], (ktb_o, rtb_o, khb_o, bhb_o, kbb_o, bbb_o, wtb_o))
    g_o[0] = _dot(gd, g2_ref[...])
    bonus_o[0] = _segsum(r * (kd_f + kd_b) * rk_ref[...], p) * v


def _head_ones():
    h = jnp.arange(RW_WIDTH, dtype=jnp.int32) // RW_HEAD
    return (h[:, None] == h[None, :]).astype(BF16)


def _chunk_matrices(tb, c):
    t = jnp.arange(tb, dtype=jnp.int32)
    same = (t[:, None] // c) == (t[None, :] // c)
    tri_f = (same & (t[None, :] <= t[:, None])).astype(BF16)
    tri_b = (same & (t[None, :] >= t[:, None])).astype(BF16)
    sel = (jnp.arange(tb // c, dtype=jnp.int32)[:, None] == (t[None, :] // c)).astype(BF16)
    return tri_f, tri_b, sel


def _rw_prep(u_r, mu, w0, w2, a0, a2, g2, k_k, k_a, r_k, p_ones, tb=512):
    B, L, C = u_r.shape
    tb = min(tb, L)
    W = RW_WIDTH
    c = min(RW_CHUNK, L)
    ncb = tb // c
    tri_f, tri_b, sel = _chunk_matrices(tb, c)
    zeros = jnp.zeros((RW_LORA, W), F32)
    w2f = jnp.concatenate([w2[0], zeros], axis=0).astype(BF16)
    w2b = jnp.concatenate([zeros, w2[1]], axis=0).astype(BF16)
    a2f = jnp.concatenate([a2[0], zeros], axis=0).astype(BF16)
    a2b = jnp.concatenate([zeros, a2[1]], axis=0).astype(BF16)
    full = lambda shape: pl.BlockSpec(shape, lambda b, j: tuple(0 for _ in shape))
    tok = lambda: pl.BlockSpec((1, tb, W), lambda b, j: (b, j, 0))
    wts = lambda: pl.BlockSpec((1, ncb, W), lambda b, j: (b, j, 0))
    bf_tok = jax.ShapeDtypeStruct((B, L, W), BF16)
    f32_tok = jax.ShapeDtypeStruct((B, L, W), F32)
    wt_shape = jax.ShapeDtypeStruct((B, L // c, W), F32)
    g8 = tb // 8
    outs = pl.pallas_call(
        _rw_prep_kernel,
        grid=(B, L // tb),
        in_specs=[pl.BlockSpec((1, tb, C), lambda b, j: (b, j, 0)),
                  pl.BlockSpec((1, 8, C), lambda b, j: (b, jnp.maximum(j * g8 - 1, 0), 0)),
                  pl.BlockSpec((1, 8, C), lambda b, j: (b, jnp.minimum((j + 1) * g8, L // 8 - 1), 0)),
                  full((1, C)), full((2, W)), full((2 * RW_LORA, W)), full((2 * RW_LORA, W)),
                  full((2, W)), full((2 * RW_LORA, W)), full((2 * RW_LORA, W)),
                  full((RW_LORA_G, W)), full((1, W)), full((1, W)), full((1, W)), full((W, W)),
                  full((tb, tb)), full((tb, tb)), full((ncb, tb))],
        out_specs=[tok()] + ([tok() for _ in range(6)] + [wts()]) * 2 + [tok(), tok()],
        out_shape=[bf_tok] + ([bf_tok] * 6 + [wt_shape]) * 2 + [f32_tok, f32_tok],
        compiler_params=_cparams(("parallel", "parallel")),
        name="rw_prep",
    )(u_r, u_r, u_r, mu.reshape(1, C), w0, w2f, w2b, a0, a2f, a2b, g2.astype(BF16),
      k_k.reshape(1, W), k_a.reshape(1, W), r_k.reshape(1, W), p_ones, tri_f, tri_b, sel)
    v, g, bonus = outs[0], outs[15], outs[16]
    fwd_ops = tuple(outs[1:7]) + (v, outs[7])
    bwd_ops = tuple(outs[8:14]) + (v, outs[14])
    return fwd_ops, bwd_ops, g, bonus, ncb


def _bmm(a, b):
    return lax.dot_general(a, b, (((2,), (1,)), ((0,), (0,))), preferred_element_type=F32)


def _bmm_nt(a, b):
    return lax.dot_general(a, b, (((2,), (2,)), ((0,), (0,))), preferred_element_type=F32)


def _bmm_tn(a, b):
    return lax.dot_general(a, b, (((1,), (1,)), ((0,), (0,))), preferred_element_type=F32)


RW_QUAD = 4 * RW_HEAD
RW_SEQS = 4


def _block_diag(x, same_head):
    reps = RW_QUAD // x.shape[1]
    return jnp.where(same_head, jnp.concatenate([x] * reps, axis=1), jnp.zeros((), x.dtype))


def _quads(x_f, x_b):
    halves = lambda x: [x[n, :, s:s + RW_QUAD] for n in range(x.shape[0]) for s in range(0, x.shape[2], RW_QUAD)]
    return jnp.stack(halves(x_f) + halves(x_b), axis=0)


def _rw_scan_kernel(nc, ncb, ktf, rtf, khf, bhf, kbf, bbf, vf, wtf, ktb, rtb, khb, bhb, kbb, bbb, vb, wtb,
                    yf_ref, yb_ref, s_ref):
    i = pl.program_id(1)

    @pl.when(i == 0)
    def _():
        s_ref[...] = jnp.zeros_like(s_ref)

    kt, rt = _quads(ktf[...], ktb[...]), _quads(rtf[...], rtb[...])
    khat, bhat = _quads(khf[...], khb[...]), _quads(bhf[...], bhb[...])
    kbar, bbar = _quads(kbf[...], kbb[...]), _quads(bbf[...], bbb[...])
    v = _quads(vf[...], vb[...])
    wt = _quads(wtf[:, pl.ds(i % ncb, 1), :], wtb[:, pl.ds((nc - 1 - i) % ncb, 1), :])
    g, c, q = kt.shape
    ri = lax.broadcasted_iota(jnp.int32, (g, c, q), 1)
    ci = lax.broadcasted_iota(jnp.int32, (g, c, q), 2) % c
    rev = lax.broadcasted_iota(jnp.int32, (g, c, q), 0) >= g // 2
    ahead = jnp.where(rev, ri - ci, ci - ri)
    strict = ahead < 0
    causal = ahead <= 0
    eye = jnp.where(ahead == 0, 1.0, 0.0)
    same_head = (lax.broadcasted_iota(jnp.int32, (1, q, q), 1) // RW_HEAD
                 == lax.broadcasted_iota(jnp.int32, (1, q, q), 2) // RW_HEAD)
    bd = lambda x: _block_diag(x, same_head)

    lh = jnp.concatenate([kt, rt], axis=1)
    gk = _bmm_nt(lh, bd(khat))
    gb = _bmm_nt(lh, bd(bhat))
    a_k = jnp.where(strict, gk[:, :c], 0.0)
    b_k = jnp.where(causal, gk[:, c:], 0.0)
    a_b = jnp.where(strict, gb[:, :c], 0.0)
    b_b = jnp.where(causal, gb[:, c:], 0.0)
    p = -a_b
    tmat = eye + p
    p = _bmm(p.astype(BF16), bd(p.astype(BF16)))
    doublings = int(math.log2(c)) - 1
    for step in range(doublings):
        p_bd = bd(p.astype(BF16))
        if step == doublings - 1:
            tmat = tmat + _bmm(tmat.astype(BF16), p_bd)
        else:
            both = _bmm(jnp.concatenate([p, tmat], axis=1).astype(BF16), p_bd)
            tmat = tmat + both[:, c:]
            p = both[:, :c]
    s0 = s_ref[...]
    ks = _bmm_nt(lh, s0.astype(BF16))
    av = _bmm(jnp.concatenate([a_k, b_k], axis=1).astype(BF16), bd(v))
    ub = _bmm(tmat.astype(BF16), bd((ks[:, :c] + av[:, :c]).astype(BF16))).astype(BF16)
    y = ks[:, c:] + av[:, c:] - _bmm(b_b.astype(BF16), bd(ub))
    upd = _bmm_tn(jnp.concatenate([v, ub], axis=1), jnp.concatenate([kbar, -bbar], axis=1))
    s_ref[...] = s0 * wt + jnp.where(same_head, upd, 0.0)
    per_seq = yf_ref.shape[2] // q
    for n in range(yf_ref.shape[0]):
        yf_ref[n] = jnp.concatenate([y[n * per_seq + j] for j in range(per_seq)], axis=1)
        yb_ref[n] = jnp.concatenate([y[g // 2 + n * per_seq + j] for j in range(per_seq)], axis=1)


def _rw_scan(fwd_ops, bwd_ops, ncb):
    B, L, W = fwd_ops[0].shape
    c = min(RW_CHUNK, L)
    nc = L // c
    nb = RW_SEQS if B % RW_SEQS == 0 else 1
    fwd = lambda: pl.BlockSpec((nb, c, W), lambda b, i: (b, i, 0))
    bwd = lambda: pl.BlockSpec((nb, c, W), lambda b, i: (b, nc - 1 - i, 0))
    wt_f = pl.BlockSpec((nb, ncb, W), lambda b, i: (b, i // ncb, 0))
    wt_b = pl.BlockSpec((nb, ncb, W), lambda b, i: (b, (nc - 1 - i) // ncb, 0))
    return pl.pallas_call(
        functools.partial(_rw_scan_kernel, nc, ncb),
        grid=(B // nb, nc),
        in_specs=[fwd() for _ in range(7)] + [wt_f] + [bwd() for _ in range(7)] + [wt_b],
        out_specs=[fwd(), bwd()],
        out_shape=[jax.ShapeDtypeStruct((B, L, W), F32)] * 2,
        scratch_shapes=[pltpu.VMEM((2 * nb * W // RW_QUAD, RW_QUAD, RW_QUAD), F32)],
        compiler_params=_cparams(("parallel", "arbitrary")),
        name="rw_scan",
    )(*fwd_ops, *bwd_ops)


def _merge_kernel(x_ref, yh_ref, yf_ref, yb_ref, bonus_ref, g_ref, gates_ref, p_ref, gnw_ref, gnb_ref,
                  why_ref, wrw_ref, wo_ref, lnw_ref, lnb_ref, o_ref):
    p = p_ref[...]
    y = yf_ref[...] + yb_ref[...]
    mu = _segsum(y, p) * (1.0 / RW_HEAD)
    yc = y - mu
    var = _segsum(yc * yc, p, terms=1) * (1.0 / RW_HEAD)
    yn = yc * lax.rsqrt(var + RW_GN_EPS) * gnw_ref[...] + gnb_ref[...]
    y_r = (yn + bonus_ref[...]) * g_ref[...]
    ph = _dot(_tiles_cat(yh_ref, lambda q: (0, q), yh_ref.shape[1]).astype(BF16), why_ref[...])
    pr = _dot(y_r.astype(BF16), wrw_ref[...])
    gates = jax.nn.sigmoid(gates_ref[...])
    m = gates[:, :D_MODEL] * ph + gates[:, D_MODEL:] * pr
    mix = _dot(m.astype(BF16), wo_ref[...])
    o_ref[...] = _layer_norm(DN_ALPHA * x_ref[...] + mix, lnw_ref[...], lnb_ref[...])


def _merge(x2, yh, yf, yb, bonus, g, gates, p_ones, gn_w, gn_b, w_hy_out, w_rw_out, w_o, ln_w, ln_b, tm=512):
    m, d = x2.shape
    _, tiles, L, _ = yh.shape
    tm = min(tm, L)
    per_seq = L // tm
    W = RW_WIDTH
    row = lambda width: pl.BlockSpec((tm, width), lambda i: (i, 0))
    full = lambda shape: pl.BlockSpec(shape, lambda i: tuple(0 for _ in shape))
    return pl.pallas_call(
        _merge_kernel,
        grid=(m // tm,),
        in_specs=[row(d), pl.BlockSpec((1, tiles, tm, LANES), lambda i: (i // per_seq, 0, i % per_seq, 0)),
                  row(W), row(W), row(W), row(W), row(GATE_COLS),
                  full((W, W)), full((1, W)), full((1, W)),
                  full((HY_WIDTH, d)), full((W, d)), full((d, d)), full((1, d)), full((1, d))],
        out_specs=row(d),
        out_shape=jax.ShapeDtypeStruct((m, d), F32),
        compiler_params=_cparams(("parallel",)),
        name="merge_ln1",
    )(x2, yh, yf, yb, bonus, g, gates, p_ones, gn_w.reshape(1, W), gn_b.reshape(1, W),
      w_hy_out.astype(BF16), w_rw_out.astype(BF16), w_o.astype(BF16), ln_w.reshape(1, d), ln_b.reshape(1, d))


def _ffn_kernel(x_ref, wg_ref, wu_ref, wd_ref, lnw_ref, lnb_ref, o_ref):
    x = x_ref[...]
    xb = x.astype(BF16)
    hidden = jax.nn.silu(_dot(xb, wg_ref[...])) * _dot(xb, wu_ref[...])
    ffn = _dot(hidden.astype(BF16), wd_ref[...])
    o_ref[...] = _layer_norm(DN_ALPHA * x + ffn, lnw_ref[...], lnb_ref[...])


def _ffn(x2, w_gate, w_up, w_down, ln_w, ln_b, tm=512):
    m, d = x2.shape
    tm = min(tm, m)
    fh = w_gate.shape[1]
    resident = lambda shape: pl.BlockSpec(shape, lambda i: (0, 0), pipeline_mode=pl.Buffered(1))
    return pl.pallas_call(
        _ffn_kernel,
        grid=(m // tm,),
        in_specs=[pl.BlockSpec((tm, d), lambda i: (i, 0)),
                  resident((d, fh)), resident((d, fh)), resident((fh, d)),
                  resident((1, d)), resident((1, d))],
        out_specs=pl.BlockSpec((tm, d), lambda i: (i, 0)),
        out_shape=jax.ShapeDtypeStruct((m, d), F32),
        compiler_params=_cparams(("parallel",)),
        name="ffn_ln2",
    )(x2, w_gate.astype(BF16), w_up.astype(BF16), w_down.astype(BF16), ln_w.reshape(1, d), ln_b.reshape(1, d))


def _layer(x, w_in, hy_conv_w, hy_conv_b, hy_filt_w1, hy_filt_b1, hy_filt_w2, hy_filt_b2,
           hy_filt_w3, hy_filt_b3, hy_filt_w4, hy_sin_freq, hy_skip, rw_mu, rw_w0, rw_w2,
           rw_a0, rw_a2, rw_g2, rw_k_k, rw_k_a, rw_r_k, rw_gn_w, rw_gn_b, w_hy_out, w_rw_out,
           w_o, ln1_w, ln1_b, ffn_w_gate, ffn_w_up, ffn_w_down, ln2_w, ln2_b):
    B, L, D = x.shape
    x2 = x.reshape(B * L, D)
    u3, u_r, gates = _in_proj(x2, L, w_in.astype(BF16), hy_conv_w, hy_conv_b)
    y_h = _hyena(u3, hy_filt_w1, hy_filt_b1, hy_filt_w2, hy_filt_b2, hy_filt_w3, hy_filt_b3, hy_filt_w4,
                 hy_sin_freq, hy_skip)
    p_ones = _head_ones()
    fwd_ops, bwd_ops, g, bonus, ncb = _rw_prep(
        u_r.reshape(B, L, RW_COLS), rw_mu, rw_w0, rw_w2, rw_a0, rw_a2, rw_g2, rw_k_k, rw_k_a, rw_r_k, p_ones)
    yf, yb = _rw_scan(fwd_ops, bwd_ops, ncb)
    flat = lambda a: a.reshape(B * L, a.shape[-1])
    h = _merge(x2, y_h, flat(yf), flat(yb), flat(bonus), flat(g), gates, p_ones, rw_gn_w, rw_gn_b,
               w_hy_out, w_rw_out, w_o, ln1_w, ln1_b)
    out = _ffn(h, ffn_w_gate, ffn_w_up, ffn_w_down, ln2_w, ln2_b)
    return out.reshape(B, L, D)


def kernel(x, w_in, hy_conv_w, hy_conv_b, hy_filt_w1, hy_filt_b1, hy_filt_w2, hy_filt_b2, hy_filt_w3, hy_filt_b3, hy_filt_w4, hy_sin_freq, hy_skip, rw_mu, rw_w0, rw_w2, rw_a0, rw_a2, rw_g2, rw_k_k, rw_k_a, rw_r_k, rw_gn_w, rw_gn_b, w_hy_out, w_rw_out, w_o, ln1_w, ln1_b, ffn_w_gate, ffn_w_up, ffn_w_down, ln2_w, ln2_b):
    params = (w_in, hy_conv_w, hy_conv_b, hy_filt_w1, hy_filt_b1, hy_filt_w2, hy_filt_b2, hy_filt_w3,
              hy_filt_b3, hy_filt_w4, hy_sin_freq, hy_skip, rw_mu, rw_w0, rw_w2, rw_a0, rw_a2, rw_g2,
              rw_k_k, rw_k_a, rw_r_k, rw_gn_w, rw_gn_b, w_hy_out, w_rw_out, w_o, ln1_w, ln1_b,
              ffn_w_gate, ffn_w_up, ffn_w_down, ln2_w, ln2_b)
    for l in range(w_in.shape[0]):
        x = _layer(x, *[p[l] for p in params])
    return x
```

```python
import functools
import math

import jax
import jax.numpy as jnp
from jax import lax
from jax.experimental import pallas as pl
from jax.experimental.pallas import tpu as pltpu

F32 = jnp.float32
BF16 = jnp.bfloat16
HIGHEST = lax.Precision.HIGHEST

D_MODEL = 1024
HY_WIDTH = 512
HY_ORDER = 2
HY_BANDS = 16
HY_FILT_HIDDEN = 64
HY_FAST_DECAY = 0.3
HY_SLOW_DECAY = 1.5
HY_DECAY_TARGET = 1e-2
HY_MAX_DECAY = math.log(HY_DECAY_TARGET) / HY_FAST_DECAY
HY_MIN_DECAY = math.log(HY_DECAY_TARGET) / HY_SLOW_DECAY
HY_COLS = 3 * HY_WIDTH
RW_WIDTH = 512
RW_HEAD = 64
RW_HEADS = RW_WIDTH // RW_HEAD
RW_LORA = 64
RW_LORA_G = 128
RW_GN_EPS = 64e-5
RW_COLS = 3 * RW_WIDTH + 4 * RW_LORA + RW_LORA_G
GATE_COLS = 2 * D_MODEL
FFN_HIDDEN = ((8 * D_MODEL + 3 * 256 - 1) // (3 * 256)) * 256
DEPTH = 1
DN_ALPHA = (2.0 * DEPTH) ** 0.25
LN_EPS = 1e-5
RW_DECAY_SCALE = math.exp(-0.5)

LANES = 128
VMEM_LIMIT = 56 * 1024 * 1024

FFT_N1 = 32
FFT_F1 = FFT_N1 // 2 + 1
RW_CHUNK = 64


def _cparams(sem, vmem=VMEM_LIMIT):
    return pltpu.CompilerParams(dimension_semantics=sem, vmem_limit_bytes=vmem)


def _dot(a, b, precision=None):
    return jnp.dot(a, b, preferred_element_type=F32, precision=precision)


def _dot_nt(a, b):
    return lax.dot_general(a, b, (((1,), (1,)), ((), ())), preferred_element_type=F32)


def _dot_tn(a, b):
    return lax.dot_general(a, b, (((0,), (0,)), ((), ())), preferred_element_type=F32)


def _layer_norm(h, w, b):
    mu = jnp.mean(h, axis=-1, keepdims=True)
    c = h - mu
    var = jnp.mean(c * c, axis=-1, keepdims=True)
    return c * lax.rsqrt(var + LN_EPS) * w + b


def _hi_lo(x):
    hi = x.astype(BF16)
    return hi, (x - hi.astype(F32)).astype(BF16)


def _segsum(x, p, terms=2):
    return _dot_parts_t(_hi_lo(x)[:terms], p)


def _dot_hl(a, b):
    a_hi, a_lo = _hi_lo(a)
    b_hi, b_lo = _hi_lo(b)
    return _dot(a_hi, b_hi) + _dot(a_lo, b_hi) + _dot(a_hi, b_lo)


def _dot_parts_t(parts, m):
    out = _dot(parts[0], m)
    for part in parts[1:]:
        out = out + _dot(part, m)
    return out


def _dot_parts(m, parts):
    out = _dot(m, parts[0])
    for part in parts[1:]:
        out = out + _dot(m, part)
    return out


HY_TILES = HY_WIDTH // LANES
HALO = 8


def _in_proj_kernel(per_seq, x_ref, xp_ref, xn_ref, w_ref, cw_ref, cb_ref, u3_ref, ur_ref, g_ref):
    pos = pl.program_id(0) % per_seq
    w_h = w_ref[:, :HY_COLS]
    xb = x_ref[...].astype(BF16)
    tm = xb.shape[0]
    ext = _dot(jnp.concatenate([xp_ref[...].astype(BF16), xb, xn_ref[...].astype(BF16)], axis=0), w_h)
    u = ext[HALO:HALO + tm]
    row = lax.broadcasted_iota(jnp.int32, (tm, 1), 0)
    prev = jnp.where((row == 0) & (pos == 0), 0.0, ext[HALO - 1:HALO - 1 + tm])
    nxt = jnp.where((row == tm - 1) & (pos == per_seq - 1), 0.0, ext[HALO + 1:HALO + 1 + tm])
    conv = cw_ref[0:1, :] * prev + cw_ref[1:2, :] * u + cw_ref[2:3, :] * nxt + cb_ref[...]
    for s in range(HY_COLS // HY_WIDTH):
        for q in range(HY_TILES):
            lo = s * HY_WIDTH + q * LANES
            u3_ref[s, 0, q] = conv[:, lo:lo + LANES]
    ur_ref[...] = _dot(xb, w_ref[:, HY_COLS:HY_COLS + RW_COLS])
    g_ref[...] = _dot(xb, w_ref[:, HY_COLS + RW_COLS:]).astype(BF16)


def _in_proj(x2, L, w_in_bf, conv_w, conv_b, tm=512):
    m, d = x2.shape
    n = w_in_bf.shape[1]
    tm = min(tm, L)
    per_seq = L // tm
    groups = tm // HALO
    resident = lambda shape: pl.BlockSpec(shape, lambda i: (0, 0), pipeline_mode=pl.Buffered(1))
    slabs = HY_COLS // HY_WIDTH
    return pl.pallas_call(
        functools.partial(_in_proj_kernel, per_seq),
        grid=(m // tm,),
        in_specs=[pl.BlockSpec((tm, d), lambda i: (i, 0)),
                  pl.BlockSpec((HALO, d), lambda i: (jnp.maximum(i * groups - 1, 0), 0)),
                  pl.BlockSpec((HALO, d), lambda i: (jnp.minimum((i + 1) * groups, m // HALO - 1), 0)),
                  resident((d, n)), resident((3, HY_COLS)), resident((1, HY_COLS))],
        out_specs=[pl.BlockSpec((slabs, 1, HY_TILES, tm, LANES), lambda i: (0, i // per_seq, 0, i % per_seq, 0)),
                   pl.BlockSpec((tm, RW_COLS), lambda i: (i, 0)),
                   pl.BlockSpec((tm, GATE_COLS), lambda i: (i, 0))],
        out_shape=[jax.ShapeDtypeStruct((slabs, m // L, HY_TILES, L, LANES), F32),
                   jax.ShapeDtypeStruct((m, RW_COLS), F32),
                   jax.ShapeDtypeStruct((m, GATE_COLS), BF16)],
        compiler_params=_cparams(("parallel",)),
        name="in_proj",
    )(x2, x2, x2, w_in_bf, conv_w, conv_b.reshape(1, HY_COLS))


def _filter_kernel(L, tb, w1t_ref, w1c_ref, w1s_ref, b1_ref, w2_ref, b2_ref, w3_ref, b3_ref,
                   w4_ref, sf_ref, freq_ref, delta_ref, kc_ref):
    n = 2 * L
    i = pl.program_id(0)
    row = i * tb + lax.broadcasted_iota(jnp.int32, (tb, 1), 0)
    second = row >= L
    pos = jnp.where(second, n - row, row).astype(F32)
    t = pos / float(L - 1)
    ang = (2.0 * math.pi * pos / float(L)) * freq_ref[...]
    pre = t * w1t_ref[...] + _dot(jnp.cos(ang), w1c_ref[...], HIGHEST) \
        - _dot(jnp.sin(ang), w1s_ref[...], HIGHEST) + b1_ref[...]
    h = jnp.sin(sf_ref[0:1, :] * pre)
    h = jnp.sin(sf_ref[1:2, :] * (_dot(h, w2_ref[...], HIGHEST) + b2_ref[...]))
    h = jnp.sin(sf_ref[2:3, :] * (_dot(h, w3_ref[...], HIGHEST) + b3_ref[...]))
    h4 = _dot_hl(h, w4_ref[...])
    window = jnp.exp(-t * delta_ref[...])
    keep = jnp.where(row == L, 0.0, 1.0)
    first = jnp.where(row == 0, 1.0, 0.0)
    for o in range(HY_ORDER):
        base = o * 2 * HY_WIDTH
        fwd = h4[:, base:base + HY_WIDTH]
        bwd = h4[:, base + HY_WIDTH:base + 2 * HY_WIDTH]
        kc = window * (jnp.where(second, bwd, fwd) * keep + first * bwd)
        for q in range(HY_TILES):
            kc_ref[o, q] = kc[:, q * LANES:(q + 1) * LANES]


def _filters(L, fw1, fb1, fw2, fb2, fw3, fb3, fw4, sin_freq, tb=512):
    n = 2 * L
    tb = min(tb, n)
    freqs = jnp.linspace(1e-4, HY_BANDS - 1, HY_BANDS, dtype=F32).reshape(1, HY_BANDS)
    deltas = jnp.abs(jnp.linspace(HY_MIN_DECAY, HY_MAX_DECAY, HY_WIDTH, dtype=F32)).reshape(1, HY_WIDTH)
    hid = HY_FILT_HIDDEN
    full = lambda shape: pl.BlockSpec(shape, lambda i: tuple(0 for _ in shape))
    return pl.pallas_call(
        functools.partial(_filter_kernel, L, tb),
        grid=(n // tb,),
        in_specs=[full((1, hid)), full((HY_BANDS, hid)), full((HY_BANDS, hid)), full((1, hid)),
                  full((hid, hid)), full((1, hid)), full((hid, hid)), full((1, hid)),
                  full((hid, HY_ORDER * 2 * HY_WIDTH)), full((3, hid)),
                  full((1, HY_BANDS)), full((1, HY_WIDTH))],
        out_specs=pl.BlockSpec((HY_ORDER, HY_TILES, tb, LANES), lambda i: (0, 0, i, 0)),
        out_shape=jax.ShapeDtypeStruct((HY_ORDER, HY_TILES, n, LANES), F32),
        compiler_params=_cparams(("parallel",)),
        name="hy_filters",
    )(fw1[0:1], fw1[1:1 + HY_BANDS], fw1[1 + HY_BANDS:], fb1.reshape(1, hid), fw2, fb2.reshape(1, hid),
      fw3, fb3.reshape(1, hid), fw4, sin_freq, freqs, deltas)


def _fft_tables(n2):
    n = FFT_N1 * n2
    f1 = jnp.arange(FFT_F1, dtype=jnp.int32)[:, None]
    a = jnp.arange(n2, dtype=jnp.int32)[:, None]
    b = jnp.arange(n2, dtype=jnp.int32)[None, :]
    ph_base = ((a * b) % n2).astype(F32) * (2.0 * math.pi / n2)
    ph_tw = (f1 * b).astype(F32) * (2.0 * math.pi / n)
    br, bi = jnp.cos(ph_base)[None], -jnp.sin(ph_base)[None]
    tr, ti = jnp.cos(ph_tw)[:, None, :], -jnp.sin(ph_tw)[:, None, :]
    cr, ci = br * tr - bi * ti, br * ti + bi * tr
    m_fwd = jnp.concatenate([jnp.concatenate([cr, -ci], axis=2),
                             jnp.concatenate([ci, cr], axis=2)], axis=1).astype(BF16)
    herm = (jnp.where((f1 == 0) | (f1 == FFT_N1 // 2), 1.0, 2.0) / n)[:, :, None]
    dr, di = herm * jnp.swapaxes(cr, 1, 2), -herm * jnp.swapaxes(ci, 1, 2)
    m_inv = jnp.concatenate([jnp.concatenate([dr, -di], axis=2),
                             jnp.concatenate([di, dr], axis=2)], axis=1).astype(BF16)
    return m_fwd, m_inv


def _slow_tables():
    k = (jnp.arange(FFT_F1, dtype=jnp.int32)[:, None] * jnp.arange(FFT_N1, dtype=jnp.int32)[None, :]) % FFT_N1
    ph = k.astype(F32) * (2.0 * math.pi / FFT_N1)
    fwd = jnp.stack([jnp.cos(ph), -jnp.sin(ph)], axis=1).reshape(2 * FFT_F1, FFT_N1)
    inv = fwd[:, :FFT_N1 // 2].T
    return fwd, inv


SLOW_UNROLL = 4
ROW_GROUP = 16
SLOW_TILES = 2


def _kron_rows(m):
    return jnp.kron(m, jnp.eye(ROW_GROUP, dtype=m.dtype))


def _slow_dft_kernel(n2, s1_count, f_ref, z_ref, a_ref):
    f_hl = f_ref[...]
    half = f_hl.shape[0] // 2
    st = z_ref.shape[2]

    def body(g, carry):
        s2 = pl.multiple_of(g * ROW_GROUP, ROW_GROUP)
        zs = jnp.concatenate(
            [_tiles_cat(z_ref, lambda q: (0, 0, q, pl.ds(s1 * n2 + s2, ROW_GROUP)), st) for s1 in range(s1_count)],
            axis=0).astype(BF16)
        a = _dot(f_hl, zs)
        a = (a[:half] + a[half:]).astype(BF16)
        for f in range(half // ROW_GROUP):
            _tiles_put(a_ref, lambda q: (0, q, pl.ds(f * n2 + s2, ROW_GROUP)), a[f * ROW_GROUP:(f + 1) * ROW_GROUP])
        return carry

    lax.fori_loop(0, n2 // ROW_GROUP, body, 0, unroll=min(SLOW_UNROLL, n2 // ROW_GROUP))


def _tiles_cat(ref, index, ntiles):
    return jnp.concatenate([ref[index(q) + (slice(None),)] for q in range(ntiles)], axis=1)


def _tiles_put(ref, index, value):
    for q in range(value.shape[1] // LANES):
        ref[index(q) + (slice(None),)] = value[:, q * LANES:(q + 1) * LANES]


def _slow_dft(z5, sel, s1_count):
    _, B, tiles, T, _ = z5.shape
    n2 = T // s1_count
    fwd, _ = _slow_tables()
    rows = 2 * FFT_F1
    f_hl = jnp.concatenate([_kron_rows(t) for t in _hi_lo(fwd[:, :s1_count])], axis=0)
    st = SLOW_TILES
    return pl.pallas_call(
        functools.partial(_slow_dft_kernel, n2, s1_count),
        grid=(B, tiles // st),
        in_specs=[pl.BlockSpec(f_hl.shape, lambda b, j: (0, 0)),
                  pl.BlockSpec((1, 1, st, T, LANES), lambda b, j: (sel, b, j, 0, 0))],
        out_specs=pl.BlockSpec((1, st, rows * n2, LANES), lambda b, j: (b, j, 0, 0)),
        out_shape=jax.ShapeDtypeStruct((B, tiles, rows * n2, LANES), BF16),
        compiler_params=_cparams(("parallel", "parallel")),
        name="hy_slow_dft",
    )(f_hl, z5)


def _spectrum_kernel(n2, a_ref, mf_ref, hr_ref, hi_ref):
    x = _dot(mf_ref[0], _tiles_cat(a_ref, lambda q: (0, q), a_ref.shape[1]))
    hr_ref[0] = x[:n2]
    hi_ref[0] = x[n2:]


def _spectrum(kc, m_fwd):
    order, tiles, n, _ = kc.shape
    n2 = n // FFT_N1
    C = tiles * LANES
    a = _slow_dft(kc.reshape(1, order, tiles, n, LANES), 0, FFT_N1)
    spec = lambda: pl.BlockSpec((1, n2, C), lambda f, o: (o, f, 0))
    return pl.pallas_call(
        functools.partial(_spectrum_kernel, n2),
        grid=(FFT_F1, order),
        in_specs=[pl.BlockSpec((1, tiles, 2 * n2, LANES), lambda f, o: (o, 0, f, 0)),
                  pl.BlockSpec((1, 2 * n2, 2 * n2), lambda f, o: (f, 0, 0))],
        out_specs=[spec(), spec()],
        out_shape=[jax.ShapeDtypeStruct((order, FFT_F1 * n2, C), F32)] * 2,
        compiler_params=_cparams(("parallel", "parallel")),
        name="hy_spectrum",
    )(a, m_fwd)


MID_SEQS = 2


def _conv_mid_kernel(n2, a_ref, hr_ref, hi_ref, mf_ref, mi_ref, b_ref):
    hr, hi = hr_ref[0], hi_ref[0]
    for n in range(a_ref.shape[0]):
        x = _dot(mf_ref[0], _tiles_cat(a_ref, lambda q: (n, q), a_ref.shape[1]))
        xr, xi = x[:n2], x[n2:]
        y = jnp.concatenate([xr * hr - xi * hi, xr * hi + xi * hr], axis=0).astype(BF16)
        _tiles_put(b_ref, lambda q: (n, q), _dot(mi_ref[0], y).astype(BF16))


def _conv_mid(a, order, hr, hi, m_fwd, m_inv):
    B, tiles, total, _ = a.shape
    rows = total // FFT_F1
    n2 = rows // 2
    W = tiles * LANES
    nb = MID_SEQS if B % MID_SEQS == 0 else 1
    blk = lambda: pl.BlockSpec((nb, tiles, rows, LANES), lambda f, b: (b, 0, f, 0))
    return pl.pallas_call(
        functools.partial(_conv_mid_kernel, n2),
        grid=(FFT_F1, B // nb),
        in_specs=[blk(),
                  pl.BlockSpec((1, n2, W), lambda f, b: (order, f, 0)),
                  pl.BlockSpec((1, n2, W), lambda f, b: (order, f, 0)),
                  pl.BlockSpec((1, rows, rows), lambda f, b: (f, 0, 0)),
                  pl.BlockSpec((1, rows, rows), lambda f, b: (f, 0, 0))],
        out_specs=blk(),
        out_shape=jax.ShapeDtypeStruct((B, tiles, total, LANES), BF16),
        compiler_params=_cparams(("parallel", "parallel")),
        name=f"hy_conv_mid{order}",
    )(a, hr, hi, m_fwd, m_inv)


IDFT_ROWS = tuple(f for f in range(2 * FFT_F1) if f not in (1, 2 * FFT_F1 - 1))


def _slow_idft_kernel(n2, g_ref, b_ref, z_ref, gate_ref, skip_ref, o_ref):
    g_hl = g_ref[...]
    half = g_hl.shape[0] // 2
    skip = skip_ref[...]
    st = b_ref.shape[1]

    def body(g, carry):
        t2 = pl.multiple_of(g * ROW_GROUP, ROW_GROUP)
        bs = jnp.concatenate(
            [_tiles_cat(b_ref, lambda q: (0, q, pl.ds(f * n2 + t2, ROW_GROUP)), st) for f in IDFT_ROWS], axis=0)
        y2 = _dot(g_hl, bs)
        y = y2[:half] + y2[half:]
        for t1 in range(half // ROW_GROUP):
            tok = lambda q: (0, 0, q, pl.ds(t1 * n2 + t2, ROW_GROUP))
            z = _tiles_cat(z_ref, tok, st)
            gate = _tiles_cat(gate_ref, tok, st)
            out = gate * (y[t1 * ROW_GROUP:(t1 + 1) * ROW_GROUP] + skip * z)
            _tiles_put(o_ref, tok, out.astype(o_ref.dtype))
        return carry

    lax.fori_loop(0, n2 // ROW_GROUP, body, 0, unroll=min(SLOW_UNROLL, n2 // ROW_GROUP))


def _slow_idft(bm, z5, zsel, g5, gsel, skip_row, out_dtype):
    B, tiles, total, _ = bm.shape
    t1 = FFT_N1 // 2
    n2 = total // (2 * FFT_F1)
    T = t1 * n2
    _, inv = _slow_tables()
    g_hl = jnp.concatenate([_kron_rows(t) for t in _hi_lo(inv[:, jnp.array(IDFT_ROWS)])], axis=0)
    st = SLOW_TILES
    tok = lambda sel: pl.BlockSpec((1, 1, st, T, LANES), lambda b, j: (sel, b, j, 0, 0))
    return pl.pallas_call(
        functools.partial(_slow_idft_kernel, n2),
        grid=(B, tiles // st),
        in_specs=[pl.BlockSpec(g_hl.shape, lambda b, j: (0, 0)),
                  pl.BlockSpec((1, st, total, LANES), lambda b, j: (b, j, 0, 0)),
                  tok(zsel), tok(gsel),
                  pl.BlockSpec((1, st * LANES), lambda b, j: (0, j))],
        out_specs=tok(0),
        out_shape=jax.ShapeDtypeStruct((1, B, tiles, T, LANES), out_dtype),
        compiler_params=_cparams(("parallel", "parallel")),
        name="hy_slow_idft",
    )(g_hl, bm, z5, g5, skip_row.reshape(1, tiles * LANES))


def _longconv(z5, zsel, g5, gsel, order, skip, hr, hi, m_fwd, m_inv, out_dtype):
    a = _slow_dft(z5, zsel, FFT_N1 // 2)
    bm = _conv_mid(a, order, hr, hi, m_fwd, m_inv)
    return _slow_idft(bm, z5, zsel, g5, gsel, skip[order], out_dtype)


def _hyena(u3, fw1, fb1, fw2, fb2, fw3, fb3, fw4, sin_freq, skip):
    L = u3.shape[3]
    kc = _filters(L, fw1, fb1, fw2, fb2, fw3, fb3, fw4, sin_freq)
    m_fwd, m_inv = _fft_tables(2 * L // FFT_N1)
    hr, hi = _spectrum(kc, m_fwd)
    z1 = _longconv(u3, 0, u3, 1, 0, skip, hr, hi, m_fwd, m_inv, F32)
    return _longconv(z1, 0, u3, 2, 1, skip, hr, hi, m_fwd, m_inv, BF16)[0]


def _rw_prep_kernel(u_ref, up_ref, un_ref, mu_ref, w0_ref, w2f_ref, w2b_ref, a0_ref, a2f_ref, a2b_ref,
                    g2_ref, kk_ref, ka_ref, rk_ref, p_ref, trif_ref, trib_ref, sel_ref,
                    v_o, ktf_o, rtf_o, khf_o, bhf_o, kbf_o, bbf_o, wtf_o,
                    ktb_o, rtb_o, khb_o, bhb_o, kbb_o, bbb_o, wtb_o, g_o, bonus_o):
    j = pl.program_id(1)
    nj = pl.num_programs(1)
    u = u_ref[0]
    tb = u.shape[0]
    prow = jnp.where(j == 0, 0.0, up_ref[0, 7:8, :])
    nrow = jnp.where(j == nj - 1, 0.0, un_ref[0, 0:1, :])
    row = lax.broadcasted_iota(jnp.int32, (tb, 1), 0)
    prev = jnp.where(row == 0, prow, pltpu.roll(u, 1, 0))
    nxt = jnp.where(row == tb - 1, nrow, pltpu.roll(u, tb - 1, 0))
    mu = mu_ref[...]
    xs = (1.0 - mu) * u + (0.5 * mu) * (prev + nxt)
    W = RW_WIDTH
    r, k, v = xs[:, 0:W], xs[:, W:2 * W], xs[:, 2 * W:3 * W]
    wd = jnp.tanh(xs[:, 3 * W:3 * W + 2 * RW_LORA]).astype(BF16)
    ad = xs[:, 3 * W + 2 * RW_LORA:3 * W + 4 * RW_LORA].astype(BF16)
    gd = jax.nn.sigmoid(xs[:, 3 * W + 4 * RW_LORA:]).astype(BF16)
    p = p_ref[...]
    kkn = k * kk_ref[...]
    kk = kkn * lax.rsqrt(jnp.maximum(_segsum(kkn * kkn, p, terms=1), 1e-24))
    ka = ka_ref[...]
    lw_f = -RW_DECAY_SCALE * jax.nn.sigmoid(w0_ref[0:1, :] + _dot(wd, w2f_ref[...]))
    lw_b = -RW_DECAY_SCALE * jax.nn.sigmoid(w0_ref[1:2, :] + _dot(wd, w2b_ref[...]))
    a_f = jax.nn.sigmoid(a0_ref[0:1, :] + _dot(ad, a2f_ref[...]))
    a_b = jax.nn.sigmoid(a0_ref[1:2, :] + _dot(ad, a2b_ref[...]))
    kd_f = k * (1.0 + (a_f - 1.0) * ka)
    kd_b = k * (1.0 + (a_b - 1.0) * ka)
    v_o[0] = v.astype(BF16)
    sel = sel_ref[...]
    ncb = sel.shape[0]
    c = tb // ncb

    def scan_operands(lw, kd, b, tri, outs):
        parts = _hi_lo(lw)
        cum = _dot_parts(tri, parts)
        wt = jnp.exp(_dot_parts(sel, parts))
        wt_tok = jnp.concatenate([jnp.broadcast_to(wt[j:j + 1], (c, wt.shape[1])) for j in range(ncb)], axis=0)
        e_neg = jnp.exp(-cum)
        kh = kd * e_neg
        bh = b * e_neg
        kt_o, rt_o, kh_o, bh_o, kb_o, bb_o, wt_o = outs
        kt_o[0] = (kk * jnp.exp(cum - lw)).astype(BF16)
        rt_o[0] = (r * jnp.exp(cum)).astype(BF16)
        kh_o[0] = kh.astype(BF16)
        bh_o[0] = bh.astype(BF16)
        kb_o[0] = (kh * wt_tok).astype(BF16)
        bb_o[0] = (bh * wt_tok).astype(BF16)
        wt_o[0] = wt

    scan_operands(lw_f, kd_f, kk * a_f, trif_ref[...], (ktf_o, rtf_o, khf_o, bhf_o, kbf_o, bbf_o, wtf_o))
    scan_operands(lw_b, kd_b, kk * a_b, trib_ref[...], (ktb_o, rtb_o, khb_o, bhb_o, kbb_o, bbb_o, wtb_o))
    g_o[0] = _dot(gd, g2_ref[...]).astype(BF16)
    bonus_o[0] = (_segsum(r * (kd_f + kd_b) * rk_ref[...], p) * v).astype(BF16)


def _head_ones():
    h = jnp.arange(RW_WIDTH, dtype=jnp.int32) // RW_HEAD
    return (h[:, None] == h[None, :]).astype(BF16)


def _chunk_matrices(tb, c):
    t = jnp.arange(tb, dtype=jnp.int32)
    same = (t[:, None] // c) == (t[None, :] // c)
    tri_f = (same & (t[None, :] <= t[:, None])).astype(BF16)
    tri_b = (same & (t[None, :] >= t[:, None])).astype(BF16)
    sel = (jnp.arange(tb // c, dtype=jnp.int32)[:, None] == (t[None, :] // c)).astype(BF16)
    return tri_f, tri_b, sel


def _rw_prep(u_r, mu, w0, w2, a0, a2, g2, k_k, k_a, r_k, p_ones, tb=512):
    B, L, C = u_r.shape
    tb = min(tb, L)
    W = RW_WIDTH
    c = min(RW_CHUNK, L)
    ncb = tb // c
    tri_f, tri_b, sel = _chunk_matrices(tb, c)
    zeros = jnp.zeros((RW_LORA, W), F32)
    w2f = jnp.concatenate([w2[0], zeros], axis=0).astype(BF16)
    w2b = jnp.concatenate([zeros, w2[1]], axis=0).astype(BF16)
    a2f = jnp.concatenate([a2[0], zeros], axis=0).astype(BF16)
    a2b = jnp.concatenate([zeros, a2[1]], axis=0).astype(BF16)
    full = lambda shape: pl.BlockSpec(shape, lambda b, j: tuple(0 for _ in shape))
    tok = lambda: pl.BlockSpec((1, tb, W), lambda b, j: (b, j, 0))
    wts = lambda: pl.BlockSpec((1, ncb, W), lambda b, j: (b, j, 0))
    bf_tok = jax.ShapeDtypeStruct((B, L, W), BF16)
    wt_shape = jax.ShapeDtypeStruct((B, L // c, W), F32)
    g8 = tb // 8
    outs = pl.pallas_call(
        _rw_prep_kernel,
        grid=(B, L // tb),
        in_specs=[pl.BlockSpec((1, tb, C), lambda b, j: (b, j, 0)),
                  pl.BlockSpec((1, 8, C), lambda b, j: (b, jnp.maximum(j * g8 - 1, 0), 0)),
                  pl.BlockSpec((1, 8, C), lambda b, j: (b, jnp.minimum((j + 1) * g8, L // 8 - 1), 0)),
                  full((1, C)), full((2, W)), full((2 * RW_LORA, W)), full((2 * RW_LORA, W)),
                  full((2, W)), full((2 * RW_LORA, W)), full((2 * RW_LORA, W)),
                  full((RW_LORA_G, W)), full((1, W)), full((1, W)), full((1, W)), full((W, W)),
                  full((tb, tb)), full((tb, tb)), full((ncb, tb))],
        out_specs=[tok()] + ([tok() for _ in range(6)] + [wts()]) * 2 + [tok(), tok()],
        out_shape=[bf_tok] + ([bf_tok] * 6 + [wt_shape]) * 2 + [bf_tok, bf_tok],
        compiler_params=_cparams(("parallel", "parallel")),
        name="rw_prep",
    )(u_r, u_r, u_r, mu.reshape(1, C), w0, w2f, w2b, a0, a2f, a2b, g2.astype(BF16),
      k_k.reshape(1, W), k_a.reshape(1, W), r_k.reshape(1, W), p_ones, tri_f, tri_b, sel)
    v, g, bonus = outs[0], outs[15], outs[16]
    fwd_ops = tuple(outs[1:7]) + (v, outs[7])
    bwd_ops = tuple(outs[8:14]) + (v, outs[14])
    return fwd_ops, bwd_ops, g, bonus, ncb


def _bmm(a, b):
    return lax.dot_general(a, b, (((2,), (1,)), ((0,), (0,))), preferred_element_type=F32)


def _bmm_nt(a, b):
    return lax.dot_general(a, b, (((2,), (2,)), ((0,), (0,))), preferred_element_type=F32)


def _bmm_tn(a, b):
    return lax.dot_general(a, b, (((1,), (1,)), ((0,), (0,))), preferred_element_type=F32)


RW_QUAD = 4 * RW_HEAD
RW_SEQS = 4


def _block_diag(x, same_head):
    reps = RW_QUAD // x.shape[1]
    return jnp.where(same_head, jnp.concatenate([x] * reps, axis=1), jnp.zeros((), x.dtype))


def _quads(x_f, x_b):
    halves = lambda x: [x[n, :, s:s + RW_QUAD] for n in range(x.shape[0]) for s in range(0, x.shape[2], RW_QUAD)]
    return jnp.stack(halves(x_f) + halves(x_b), axis=0)


def _rw_scan_kernel(nc, ncb, ktf, rtf, khf, bhf, kbf, bbf, vf, wtf, ktb, rtb, khb, bhb, kbb, bbb, vb, wtb,
                    yf_ref, yb_ref, s_ref):
    i = pl.program_id(1)

    @pl.when(i == 0)
    def _():
        s_ref[...] = jnp.zeros_like(s_ref)

    kt, rt = _quads(ktf[...], ktb[...]), _quads(rtf[...], rtb[...])
    khat, bhat = _quads(khf[...], khb[...]), _quads(bhf[...], bhb[...])
    kbar, bbar = _quads(kbf[...], kbb[...]), _quads(bbf[...], bbb[...])
    v = _quads(vf[...], vb[...])
    wt = _quads(wtf[:, pl.ds(i % ncb, 1), :], wtb[:, pl.ds((nc - 1 - i) % ncb, 1), :])
    g, c, q = kt.shape
    ri = lax.broadcasted_iota(jnp.int32, (g, c, q), 1)
    ci = lax.broadcasted_iota(jnp.int32, (g, c, q), 2) % c
    rev = lax.broadcasted_iota(jnp.int32, (g, c, q), 0) >= g // 2
    ahead = jnp.where(rev, ri - ci, ci - ri)
    strict = ahead < 0
    causal = ahead <= 0
    eye = jnp.where(ahead == 0, 1.0, 0.0)
    same_head = (lax.broadcasted_iota(jnp.int32, (1, q, q), 1) // RW_HEAD
                 == lax.broadcasted_iota(jnp.int32, (1, q, q), 2) // RW_HEAD)
    bd = lambda x: _block_diag(x, same_head)

    lh = jnp.concatenate([kt, rt], axis=1)
    gk = _bmm_nt(lh, bd(khat))
    gb = _bmm_nt(lh, bd(bhat))
    a_k = jnp.where(strict, gk[:, :c], 0.0)
    b_k = jnp.where(causal, gk[:, c:], 0.0)
    a_b = jnp.where(strict, gb[:, :c], 0.0)
    b_b = jnp.where(causal, gb[:, c:], 0.0)
    p = -a_b
    tmat = eye + p
    p = _bmm(p.astype(BF16), bd(p.astype(BF16)))
    doublings = int(math.log2(c)) - 1
    for step in range(doublings):
        p_bd = bd(p.astype(BF16))
        if step == doublings - 1:
            tmat = tmat + _bmm(tmat.astype(BF16), p_bd)
        else:
            both = _bmm(jnp.concatenate([p, tmat], axis=1).astype(BF16), p_bd)
            tmat = tmat + both[:, c:]
            p = both[:, :c]
    s0 = s_ref[...]
    ks = _bmm_nt(lh, s0.astype(BF16))
    av = _bmm(jnp.concatenate([a_k, b_k], axis=1).astype(BF16), bd(v))
    ub = _bmm(tmat.astype(BF16), bd((ks[:, :c] + av[:, :c]).astype(BF16))).astype(BF16)
    y = ks[:, c:] + av[:, c:] - _bmm(b_b.astype(BF16), bd(ub))
    upd = _bmm_tn(jnp.concatenate([v, ub], axis=1), jnp.concatenate([kbar, -bbar], axis=1))
    s_ref[...] = s0 * wt + jnp.where(same_head, upd, 0.0)
    per_seq = yf_ref.shape[2] // q
    for n in range(yf_ref.shape[0]):
        yf_ref[n] = jnp.concatenate([y[n * per_seq + j] for j in range(per_seq)], axis=1).astype(BF16)
        yb_ref[n] = jnp.concatenate([y[g // 2 + n * per_seq + j] for j in range(per_seq)], axis=1).astype(BF16)


def _rw_scan(fwd_ops, bwd_ops, ncb):
    B, L, W = fwd_ops[0].shape
    c = min(RW_CHUNK, L)
    nc = L // c
    nb = RW_SEQS if B % RW_SEQS == 0 else 1
    fwd = lambda: pl.BlockSpec((nb, c, W), lambda b, i: (b, i, 0))
    bwd = lambda: pl.BlockSpec((nb, c, W), lambda b, i: (b, nc - 1 - i, 0))
    wt_f = pl.BlockSpec((nb, ncb, W), lambda b, i: (b, i // ncb, 0))
    wt_b = pl.BlockSpec((nb, ncb, W), lambda b, i: (b, (nc - 1 - i) // ncb, 0))
    return pl.pallas_call(
        functools.partial(_rw_scan_kernel, nc, ncb),
        grid=(B // nb, nc),
        in_specs=[fwd() for _ in range(7)] + [wt_f] + [bwd() for _ in range(7)] + [wt_b],
        out_specs=[fwd(), bwd()],
        out_shape=[jax.ShapeDtypeStruct((B, L, W), BF16)] * 2,
        scratch_shapes=[pltpu.VMEM((2 * nb * W // RW_QUAD, RW_QUAD, RW_QUAD), F32)],
        compiler_params=_cparams(("parallel", "arbitrary")),
        name="rw_scan",
    )(*fwd_ops, *bwd_ops)


def _merge_kernel(x_ref, yh_ref, yf_ref, yb_ref, bonus_ref, g_ref, gates_ref, p_ref, gnw_ref, gnb_ref,
                  why_ref, wrw_ref, wo_ref, lnw_ref, lnb_ref, o_ref):
    p = p_ref[...]
    y = yf_ref[...].astype(F32) + yb_ref[...].astype(F32)
    mu = _segsum(y, p) * (1.0 / RW_HEAD)
    yc = y - mu
    var = _segsum(yc * yc, p, terms=1) * (1.0 / RW_HEAD)
    yn = yc * lax.rsqrt(var + RW_GN_EPS) * gnw_ref[...] + gnb_ref[...]
    y_r = (yn + bonus_ref[...].astype(F32)) * g_ref[...].astype(F32)
    ph = _dot(_tiles_cat(yh_ref, lambda q: (0, q), yh_ref.shape[1]).astype(BF16), why_ref[...])
    pr = _dot(y_r.astype(BF16), wrw_ref[...])
    gates = jax.nn.sigmoid(gates_ref[...].astype(F32))
    m = gates[:, :D_MODEL] * ph + gates[:, D_MODEL:] * pr
    mix = _dot(m.astype(BF16), wo_ref[...])
    o_ref[...] = _layer_norm(DN_ALPHA * x_ref[...] + mix, lnw_ref[...], lnb_ref[...])


def _merge(x2, yh, yf, yb, bonus, g, gates, p_ones, gn_w, gn_b, w_hy_out, w_rw_out, w_o, ln_w, ln_b, tm=512):
    m, d = x2.shape
    _, tiles, L, _ = yh.shape
    tm = min(tm, L)
    per_seq = L // tm
    W = RW_WIDTH
    row = lambda width: pl.BlockSpec((tm, width), lambda i: (i, 0))
    full = lambda shape: pl.BlockSpec(shape, lambda i: tuple(0 for _ in shape))
    return pl.pallas_call(
        _merge_kernel,
        grid=(m // tm,),
        in_specs=[row(d), pl.BlockSpec((1, tiles, tm, LANES), lambda i: (i // per_seq, 0, i % per_seq, 0)),
                  row(W), row(W), row(W), row(W), row(GATE_COLS),
                  full((W, W)), full((1, W)), full((1, W)),
                  full((HY_WIDTH, d)), full((W, d)), full((d, d)), full((1, d)), full((1, d))],
        out_specs=row(d),
        out_shape=jax.ShapeDtypeStruct((m, d), F32),
        compiler_params=_cparams(("parallel",)),
        name="merge_ln1",
    )(x2, yh, yf, yb, bonus, g, gates, p_ones, gn_w.reshape(1, W), gn_b.reshape(1, W),
      w_hy_out.astype(BF16), w_rw_out.astype(BF16), w_o.astype(BF16), ln_w.reshape(1, d), ln_b.reshape(1, d))


def _ffn_kernel(x_ref, wg_ref, wu_ref, wd_ref, lnw_ref, lnb_ref, o_ref):
    x = x_ref[...]
    xb = x.astype(BF16)
    hidden = jax.nn.silu(_dot(xb, wg_ref[...])) * _dot(xb, wu_ref[...])
    ffn = _dot(hidden.astype(BF16), wd_ref[...])
    o_ref[...] = _layer_norm(DN_ALPHA * x + ffn, lnw_ref[...], lnb_ref[...])


def _ffn(x2, w_gate, w_up, w_down, ln_w, ln_b, tm=512):
    m, d = x2.shape
    tm = min(tm, m)
    fh = w_gate.shape[1]
    resident = lambda shape: pl.BlockSpec(shape, lambda i: (0, 0), pipeline_mode=pl.Buffered(1))
    return pl.pallas_call(
        _ffn_kernel,
        grid=(m // tm,),
        in_specs=[pl.BlockSpec((tm, d), lambda i: (i, 0)),
                  resident((d, fh)), resident((d, fh)), resident((fh, d)),
                  resident((1, d)), resident((1, d))],
        out_specs=pl.BlockSpec((tm, d), lambda i: (i, 0)),
        out_shape=jax.ShapeDtypeStruct((m, d), F32),
        compiler_params=_cparams(("parallel",)),
        name="ffn_ln2",
    )(x2, w_gate.astype(BF16), w_up.astype(BF16), w_down.astype(BF16), ln_w.reshape(1, d), ln_b.reshape(1, d))


def _layer(x, w_in, hy_conv_w, hy_conv_b, hy_filt_w1, hy_filt_b1, hy_filt_w2, hy_filt_b2,
           hy_filt_w3, hy_filt_b3, hy_filt_w4, hy_sin_freq, hy_skip, rw_mu, rw_w0, rw_w2,
           rw_a0, rw_a2, rw_g2, rw_k_k, rw_k_a, rw_r_k, rw_gn_w, rw_gn_b, w_hy_out, w_rw_out,
           w_o, ln1_w, ln1_b, ffn_w_gate, ffn_w_up, ffn_w_down, ln2_w, ln2_b):
    B, L, D = x.shape
    x2 = x.reshape(B * L, D)
    u3, u_r, gates = _in_proj(x2, L, w_in.astype(BF16), hy_conv_w, hy_conv_b)
    y_h = _hyena(u3, hy_filt_w1, hy_filt_b1, hy_filt_w2, hy_filt_b2, hy_filt_w3, hy_filt_b3, hy_filt_w4,
                 hy_sin_freq, hy_skip)
    p_ones = _head_ones()
    fwd_ops, bwd_ops, g, bonus, ncb = _rw_prep(
        u_r.reshape(B, L, RW_COLS), rw_mu, rw_w0, rw_w2, rw_a0, rw_a2, rw_g2, rw_k_k, rw_k_a, rw_r_k, p_ones)
    yf, yb = _rw_scan(fwd_ops, bwd_ops, ncb)
    flat = lambda a: a.reshape(B * L, a.shape[-1])
    h = _merge(x2, y_h, flat(yf), flat(yb), flat(bonus), flat(g), gates, p_ones, rw_gn_w, rw_gn_b,
               w_hy_out, w_rw_out, w_o, ln1_w, ln1_b)
    out = _ffn(h, ffn_w_gate, ffn_w_up, ffn_w_down, ln2_w, ln2_b)
    return out.reshape(B, L, D)


def kernel(x, w_in, hy_conv_w, hy_conv_b, hy_filt_w1, hy_filt_b1, hy_filt_w2, hy_filt_b2, hy_filt_w3, hy_filt_b3, hy_filt_w4, hy_sin_freq, hy_skip, rw_mu, rw_w0, rw_w2, rw_a0, rw_a2, rw_g2, rw_k_k, rw_k_a, rw_r_k, rw_gn_w, rw_gn_b, w_hy_out, w_rw_out, w_o, ln1_w, ln1_b, ffn_w_gate, ffn_w_up, ffn_w_down, ln2_w, ln2_b):
    params = (w_in, hy_conv_w, hy_conv_b, hy_filt_w1, hy_filt_b1, hy_filt_w2, hy_filt_b2, hy_filt_w3,
              hy_filt_b3, hy_filt_w4, hy_sin_freq, hy_skip, rw_mu, rw_w0, rw_w2, rw_a0, rw_a2, rw_g2,
              rw_k_k, rw_k_a, rw_r_k, rw_gn_w, rw_gn_b, w_hy_out, w_rw_out, w_o, ln1_w, ln1_b,
              ffn_w_gate, ffn_w_up, ffn_w_down, ln2_w, ln2_b)
    for l in range(w_in.shape[0]):
        x = _layer(x, *[p[l] for p in params])
    return x
```

```python
import functools
import math

import numpy as np
import jax
import jax.numpy as jnp
from jax import lax
from jax.experimental import pallas as pl
from jax.experimental.pallas import tpu as pltpu

F32 = jnp.float32
BF16 = jnp.bfloat16
HIGHEST = lax.Precision.HIGHEST

D_MODEL = 1024
HY_WIDTH = 512
HY_ORDER = 2
HY_BANDS = 16
HY_FILT_HIDDEN = 64
HY_FAST_DECAY = 0.3
HY_SLOW_DECAY = 1.5
HY_DECAY_TARGET = 1e-2
HY_MAX_DECAY = math.log(HY_DECAY_TARGET) / HY_FAST_DECAY
HY_MIN_DECAY = math.log(HY_DECAY_TARGET) / HY_SLOW_DECAY
HY_COLS = 3 * HY_WIDTH
RW_WIDTH = 512
RW_HEAD = 64
RW_HEADS = RW_WIDTH // RW_HEAD
RW_LORA = 64
RW_LORA_G = 128
RW_GN_EPS = 64e-5
RW_COLS = 3 * RW_WIDTH + 4 * RW_LORA + RW_LORA_G
GATE_COLS = 2 * D_MODEL
FFN_HIDDEN = ((8 * D_MODEL + 3 * 256 - 1) // (3 * 256)) * 256
DEPTH = 1
DN_ALPHA = (2.0 * DEPTH) ** 0.25
LN_EPS = 1e-5
RW_DECAY_SCALE = math.exp(-0.5)

LANES = 128
VMEM_LIMIT = 56 * 1024 * 1024

FFT_N1 = 32
FFT_F1 = FFT_N1 // 2 + 1
RW_CHUNK = 64


def _cparams(sem, vmem=VMEM_LIMIT):
    return pltpu.CompilerParams(dimension_semantics=sem, vmem_limit_bytes=vmem)


def _dot(a, b, precision=None):
    return jnp.dot(a, b, preferred_element_type=F32, precision=precision)


def _dot_nt(a, b):
    return lax.dot_general(a, b, (((1,), (1,)), ((), ())), preferred_element_type=F32)


def _dot_tn(a, b):
    return lax.dot_general(a, b, (((0,), (0,)), ((), ())), preferred_element_type=F32)


def _layer_norm(h, w, b):
    mu = jnp.mean(h, axis=-1, keepdims=True)
    c = h - mu
    var = jnp.mean(c * c, axis=-1, keepdims=True)
    return c * lax.rsqrt(var + LN_EPS) * w + b


def _hi_lo(x):
    hi = x.astype(BF16)
    return hi, (x - hi.astype(F32)).astype(BF16)


def _segsum(x, p, terms=2):
    return _dot_parts_t(_hi_lo(x)[:terms], p)


def _dot_hl(a, b):
    a_hi, a_lo = _hi_lo(a)
    b_hi, b_lo = _hi_lo(b)
    return _dot(a_hi, b_hi) + _dot(a_lo, b_hi) + _dot(a_hi, b_lo)


def _dot_parts_t(parts, m):
    out = _dot(parts[0], m)
    for part in parts[1:]:
        out = out + _dot(part, m)
    return out


def _dot_parts(m, parts):
    out = _dot(m, parts[0])
    for part in parts[1:]:
        out = out + _dot(m, part)
    return out


HY_TILES = HY_WIDTH // LANES
HALO = 8


def _in_proj_kernel(per_seq, x_ref, xp_ref, xn_ref, w_ref, cw_ref, cb_ref, u3_ref, ur_ref, g_ref):
    pos = pl.program_id(0) % per_seq
    w_h = w_ref[:, :HY_COLS]
    xb = x_ref[...].astype(BF16)
    tm = xb.shape[0]
    ext = _dot(jnp.concatenate([xp_ref[...].astype(BF16), xb, xn_ref[...].astype(BF16)], axis=0), w_h)
    u = ext[HALO:HALO + tm]
    row = lax.broadcasted_iota(jnp.int32, (tm, 1), 0)
    prev = jnp.where((row == 0) & (pos == 0), 0.0, ext[HALO - 1:HALO - 1 + tm])
    nxt = jnp.where((row == tm - 1) & (pos == per_seq - 1), 0.0, ext[HALO + 1:HALO + 1 + tm])
    conv = cw_ref[0:1, :] * prev + cw_ref[1:2, :] * u + cw_ref[2:3, :] * nxt + cb_ref[...]
    for s in range(HY_COLS // HY_WIDTH):
        for q in range(HY_TILES):
            lo = s * HY_WIDTH + q * LANES
            u3_ref[s, 0, q] = conv[:, lo:lo + LANES].astype(BF16)
    ur_ref[...] = _dot(xb, w_ref[:, HY_COLS:HY_COLS + RW_COLS])
    g_ref[...] = _dot(xb, w_ref[:, HY_COLS + RW_COLS:]).astype(BF16)


def _in_proj(x2, L, w_in_bf, conv_w, conv_b, tm=512):
    m, d = x2.shape
    n = w_in_bf.shape[1]
    tm = min(tm, L)
    per_seq = L // tm
    groups = tm // HALO
    resident = lambda shape: pl.BlockSpec(shape, lambda i: (0, 0), pipeline_mode=pl.Buffered(1))
    slabs = HY_COLS // HY_WIDTH
    return pl.pallas_call(
        functools.partial(_in_proj_kernel, per_seq),
        grid=(m // tm,),
        in_specs=[pl.BlockSpec((tm, d), lambda i: (i, 0)),
                  pl.BlockSpec((HALO, d), lambda i: (jnp.maximum(i * groups - 1, 0), 0)),
                  pl.BlockSpec((HALO, d), lambda i: (jnp.minimum((i + 1) * groups, m // HALO - 1), 0)),
                  resident((d, n)), resident((3, HY_COLS)), resident((1, HY_COLS))],
        out_specs=[pl.BlockSpec((slabs, 1, HY_TILES, tm, LANES), lambda i: (0, i // per_seq, 0, i % per_seq, 0)),
                   pl.BlockSpec((tm, RW_COLS), lambda i: (i, 0)),
                   pl.BlockSpec((tm, GATE_COLS), lambda i: (i, 0))],
        out_shape=[jax.ShapeDtypeStruct((slabs, m // L, HY_TILES, L, LANES), BF16),
                   jax.ShapeDtypeStruct((m, RW_COLS), F32),
                   jax.ShapeDtypeStruct((m, GATE_COLS), BF16)],
        compiler_params=_cparams(("parallel",)),
        name="in_proj",
    )(x2, x2, x2, w_in_bf, conv_w, conv_b.reshape(1, HY_COLS))


def _filter_kernel(L, tb, w1t_ref, w1c_ref, w1s_ref, b1_ref, w2_ref, b2_ref, w3_ref, b3_ref,
                   w4_ref, sf_ref, freq_ref, delta_ref, kc_ref):
    n = 2 * L
    i = pl.program_id(0)
    row = i * tb + lax.broadcasted_iota(jnp.int32, (tb, 1), 0)
    second = row >= L
    pos = jnp.where(second, n - row, row).astype(F32)
    t = pos / float(L - 1)
    ang = (2.0 * math.pi * pos / float(L)) * freq_ref[...]
    pre = t * w1t_ref[...] + _dot(jnp.cos(ang), w1c_ref[...], HIGHEST) \
        - _dot(jnp.sin(ang), w1s_ref[...], HIGHEST) + b1_ref[...]
    h = jnp.sin(sf_ref[0:1, :] * pre)
    h = jnp.sin(sf_ref[1:2, :] * (_dot(h, w2_ref[...], HIGHEST) + b2_ref[...]))
    h = jnp.sin(sf_ref[2:3, :] * (_dot(h, w3_ref[...], HIGHEST) + b3_ref[...]))
    h4 = _dot_hl(h, w4_ref[...])
    window = jnp.exp(-t * delta_ref[...])
    keep = jnp.where(row == L, 0.0, 1.0)
    first = jnp.where(row == 0, 1.0, 0.0)
    for o in range(HY_ORDER):
        base = o * 2 * HY_WIDTH
        fwd = h4[:, base:base + HY_WIDTH]
        bwd = h4[:, base + HY_WIDTH:base + 2 * HY_WIDTH]
        kc = window * (jnp.where(second, bwd, fwd) * keep + first * bwd)
        for q in range(HY_TILES):
            kc_ref[o, q] = kc[:, q * LANES:(q + 1) * LANES]


def _filters(L, fw1, fb1, fw2, fb2, fw3, fb3, fw4, sin_freq, tb=512):
    n = 2 * L
    tb = min(tb, n)
    freqs = jnp.linspace(1e-4, HY_BANDS - 1, HY_BANDS, dtype=F32).reshape(1, HY_BANDS)
    deltas = jnp.abs(jnp.linspace(HY_MIN_DECAY, HY_MAX_DECAY, HY_WIDTH, dtype=F32)).reshape(1, HY_WIDTH)
    hid = HY_FILT_HIDDEN
    full = lambda shape: pl.BlockSpec(shape, lambda i: tuple(0 for _ in shape))
    return pl.pallas_call(
        functools.partial(_filter_kernel, L, tb),
        grid=(n // tb,),
        in_specs=[full((1, hid)), full((HY_BANDS, hid)), full((HY_BANDS, hid)), full((1, hid)),
                  full((hid, hid)), full((1, hid)), full((hid, hid)), full((1, hid)),
                  full((hid, HY_ORDER * 2 * HY_WIDTH)), full((3, hid)),
                  full((1, HY_BANDS)), full((1, HY_WIDTH))],
        out_specs=pl.BlockSpec((HY_ORDER, HY_TILES, tb, LANES), lambda i: (0, 0, i, 0)),
        out_shape=jax.ShapeDtypeStruct((HY_ORDER, HY_TILES, n, LANES), F32),
        compiler_params=_cparams(("parallel",)),
        name="hy_filters",
    )(fw1[0:1], fw1[1:1 + HY_BANDS], fw1[1 + HY_BANDS:], fb1.reshape(1, hid), fw2, fb2.reshape(1, hid),
      fw3, fb3.reshape(1, hid), fw4, sin_freq, freqs, deltas)


def _fft_tables(n2):
    n = FFT_N1 * n2
    f1 = np.arange(FFT_F1)[:, None, None]
    a = np.arange(n2)[None, :, None]
    b = np.arange(n2)[None, None, :]
    ph = ((b * (f1 + FFT_N1 * a)) % n) * (2.0 * math.pi / n)
    cr, ci = np.cos(ph), -np.sin(ph)
    m_fwd = np.concatenate([np.concatenate([cr, -ci], axis=2), np.concatenate([ci, cr], axis=2)], axis=1)
    herm = np.where((f1 == 0) | (f1 == FFT_N1 // 2), 1.0, 2.0) / n
    dr, di = herm * cr.transpose(0, 2, 1), -herm * ci.transpose(0, 2, 1)
    m_inv = np.concatenate([np.concatenate([dr, -di], axis=2), np.concatenate([di, dr], axis=2)], axis=1)
    return jnp.asarray(m_fwd, F32).astype(BF16), jnp.asarray(m_inv, F32).astype(BF16)


def _slow_tables():
    k = (np.arange(FFT_F1)[:, None] * np.arange(FFT_N1)[None, :]) % FFT_N1
    ph = k * (2.0 * math.pi / FFT_N1)
    fwd = np.stack([np.cos(ph), -np.sin(ph)], axis=1).reshape(2 * FFT_F1, FFT_N1)
    inv = fwd[:, :FFT_N1 // 2].T
    return fwd, inv


SLOW_UNROLL = 4
ROW_GROUP = 16
SLOW_TILES = 2


def _kron_hi_lo(m):
    return jnp.concatenate(_hi_lo(jnp.asarray(np.kron(m, np.eye(ROW_GROUP)), F32)), axis=0)


def _slow_dft_kernel(n2, s1_count, f_ref, z_ref, a_ref):
    f_hl = f_ref[...]
    half = f_hl.shape[0] // 2
    st = z_ref.shape[2]

    def body(g, carry):
        s2 = pl.multiple_of(g * ROW_GROUP, ROW_GROUP)
        zs = jnp.concatenate(
            [_tiles_cat(z_ref, lambda q: (0, 0, q, pl.ds(s1 * n2 + s2, ROW_GROUP)), st) for s1 in range(s1_count)],
            axis=0).astype(BF16)
        a = _dot(f_hl, zs)
        a = (a[:half] + a[half:]).astype(BF16)
        for f in range(half // ROW_GROUP):
            _tiles_put(a_ref, lambda q: (0, q, pl.ds(f * n2 + s2, ROW_GROUP)), a[f * ROW_GROUP:(f + 1) * ROW_GROUP])
        return carry

    lax.fori_loop(0, n2 // ROW_GROUP, body, 0, unroll=min(SLOW_UNROLL, n2 // ROW_GROUP))


def _tiles_cat(ref, index, ntiles):
    return jnp.concatenate([ref[index(q) + (slice(None),)] for q in range(ntiles)], axis=1)


def _tiles_put(ref, index, value):
    for q in range(value.shape[1] // LANES):
        ref[index(q) + (slice(None),)] = value[:, q * LANES:(q + 1) * LANES]


def _slow_dft(z5, sel, s1_count):
    _, B, tiles, T, _ = z5.shape
    n2 = T // s1_count
    fwd, _ = _slow_tables()
    rows = 2 * FFT_F1
    f_hl = _kron_hi_lo(fwd[:, :s1_count])
    st = SLOW_TILES
    return pl.pallas_call(
        functools.partial(_slow_dft_kernel, n2, s1_count),
        grid=(B, tiles // st),
        in_specs=[pl.BlockSpec(f_hl.shape, lambda b, j: (0, 0)),
                  pl.BlockSpec((1, 1, st, T, LANES), lambda b, j: (sel, b, j, 0, 0))],
        out_specs=pl.BlockSpec((1, st, rows * n2, LANES), lambda b, j: (b, j, 0, 0)),
        out_shape=jax.ShapeDtypeStruct((B, tiles, rows * n2, LANES), BF16),
        compiler_params=_cparams(("parallel", "parallel")),
        name="hy_slow_dft",
    )(f_hl, z5)


def _spectrum_kernel(n2, a_ref, mf_ref, hr_ref, hi_ref):
    x = _dot(mf_ref[0], _tiles_cat(a_ref, lambda q: (0, q), a_ref.shape[1]))
    hr_ref[0] = x[:n2]
    hi_ref[0] = x[n2:]


def _spectrum(kc, m_fwd):
    order, tiles, n, _ = kc.shape
    n2 = n // FFT_N1
    C = tiles * LANES
    a = _slow_dft(kc.reshape(1, order, tiles, n, LANES), 0, FFT_N1)
    spec = lambda: pl.BlockSpec((1, n2, C), lambda f, o: (o, f, 0))
    return pl.pallas_call(
        functools.partial(_spectrum_kernel, n2),
        grid=(FFT_F1, order),
        in_specs=[pl.BlockSpec((1, tiles, 2 * n2, LANES), lambda f, o: (o, 0, f, 0)),
                  pl.BlockSpec((1, 2 * n2, 2 * n2), lambda f, o: (f, 0, 0))],
        out_specs=[spec(), spec()],
        out_shape=[jax.ShapeDtypeStruct((order, FFT_F1 * n2, C), F32)] * 2,
        compiler_params=_cparams(("parallel", "parallel")),
        name="hy_spectrum",
    )(a, m_fwd)


MID_SEQS = 2


def _conv_mid_kernel(n2, a_ref, hr_ref, hi_ref, mf_ref, mi_ref, b_ref):
    hr, hi = hr_ref[0], hi_ref[0]
    for n in range(a_ref.shape[0]):
        x = _dot(mf_ref[0], _tiles_cat(a_ref, lambda q: (n, q), a_ref.shape[1]))
        xr, xi = x[:n2], x[n2:]
        y = jnp.concatenate([xr * hr - xi * hi, xr * hi + xi * hr], axis=0).astype(BF16)
        _tiles_put(b_ref, lambda q: (n, q), _dot(mi_ref[0], y).astype(BF16))


def _conv_mid(a, order, hr, hi, m_fwd, m_inv):
    B, tiles, total, _ = a.shape
    rows = total // FFT_F1
    n2 = rows // 2
    W = tiles * LANES
    nb = MID_SEQS if B % MID_SEQS == 0 else 1
    blk = lambda: pl.BlockSpec((nb, tiles, rows, LANES), lambda f, b: (b, 0, f, 0))
    return pl.pallas_call(
        functools.partial(_conv_mid_kernel, n2),
        grid=(FFT_F1, B // nb),
        in_specs=[blk(),
                  pl.BlockSpec((1, n2, W), lambda f, b: (order, f, 0)),
                  pl.BlockSpec((1, n2, W), lambda f, b: (order, f, 0)),
                  pl.BlockSpec((1, rows, rows), lambda f, b: (f, 0, 0)),
                  pl.BlockSpec((1, rows, rows), lambda f, b: (f, 0, 0))],
        out_specs=blk(),
        out_shape=jax.ShapeDtypeStruct((B, tiles, total, LANES), BF16),
        compiler_params=_cparams(("parallel", "parallel")),
        name=f"hy_conv_mid{order}",
    )(a, hr, hi, m_fwd, m_inv)


IDFT_ROWS = tuple(f for f in range(2 * FFT_F1) if f not in (1, 2 * FFT_F1 - 1))


def _slow_idft_kernel(n2, g_ref, b_ref, z_ref, gate_ref, skip_ref, o_ref):
    g_hl = g_ref[...]
    half = g_hl.shape[0] // 2
    skip = skip_ref[...]
    st = b_ref.shape[1]

    def body(g, carry):
        t2 = pl.multiple_of(g * ROW_GROUP, ROW_GROUP)
        bs = jnp.concatenate(
            [_tiles_cat(b_ref, lambda q: (0, q, pl.ds(f * n2 + t2, ROW_GROUP)), st) for f in IDFT_ROWS], axis=0)
        y2 = _dot(g_hl, bs)
        y = y2[:half] + y2[half:]
        for t1 in range(half // ROW_GROUP):
            tok = lambda q: (0, 0, q, pl.ds(t1 * n2 + t2, ROW_GROUP))
            z = _tiles_cat(z_ref, tok, st)
            gate = _tiles_cat(gate_ref, tok, st)
            out = gate * (y[t1 * ROW_GROUP:(t1 + 1) * ROW_GROUP] + skip * z)
            _tiles_put(o_ref, tok, out.astype(o_ref.dtype))
        return carry

    lax.fori_loop(0, n2 // ROW_GROUP, body, 0, unroll=min(SLOW_UNROLL, n2 // ROW_GROUP))


def _slow_idft(bm, z5, zsel, g5, gsel, skip_row, out_dtype):
    B, tiles, total, _ = bm.shape
    t1 = FFT_N1 // 2
    n2 = total // (2 * FFT_F1)
    T = t1 * n2
    _, inv = _slow_tables()
    g_hl = _kron_hi_lo(inv[:, list(IDFT_ROWS)])
    st = SLOW_TILES
    tok = lambda sel: pl.BlockSpec((1, 1, st, T, LANES), lambda b, j: (sel, b, j, 0, 0))
    return pl.pallas_call(
        functools.partial(_slow_idft_kernel, n2),
        grid=(B, tiles // st),
        in_specs=[pl.BlockSpec(g_hl.shape, lambda b, j: (0, 0)),
                  pl.BlockSpec((1, st, total, LANES), lambda b, j: (b, j, 0, 0)),
                  tok(zsel), tok(gsel),
                  pl.BlockSpec((1, st * LANES), lambda b, j: (0, j))],
        out_specs=tok(0),
        out_shape=jax.ShapeDtypeStruct((1, B, tiles, T, LANES), out_dtype),
        compiler_params=_cparams(("parallel", "parallel")),
        name="hy_slow_idft",
    )(g_hl, bm, z5, g5, skip_row.reshape(1, tiles * LANES))


def _longconv(z5, zsel, g5, gsel, order, skip, hr, hi, m_fwd, m_inv, out_dtype):
    a = _slow_dft(z5, zsel, FFT_N1 // 2)
    bm = _conv_mid(a, order, hr, hi, m_fwd, m_inv)
    return _slow_idft(bm, z5, zsel, g5, gsel, skip[order], out_dtype)


def _hyena(u3, fw1, fb1, fw2, fb2, fw3, fb3, fw4, sin_freq, skip):
    L = u3.shape[3]
    kc = _filters(L, fw1, fb1, fw2, fb2, fw3, fb3, fw4, sin_freq)
    m_fwd, m_inv = _fft_tables(2 * L // FFT_N1)
    hr, hi = _spectrum(kc, m_fwd)
    z1 = _longconv(u3, 0, u3, 1, 0, skip, hr, hi, m_fwd, m_inv, BF16)
    return _longconv(z1, 0, u3, 2, 1, skip, hr, hi, m_fwd, m_inv, BF16)[0]


def _rw_prep_kernel(u_ref, up_ref, un_ref, mu_ref, w0_ref, w2f_ref, w2b_ref, a0_ref, a2f_ref, a2b_ref,
                    g2_ref, kk_ref, ka_ref, rk_ref, p_ref, trif_ref, trib_ref, sel_ref,
                    v_o, ktf_o, rtf_o, khf_o, bhf_o, kbf_o, bbf_o, wtf_o,
                    ktb_o, rtb_o, khb_o, bhb_o, kbb_o, bbb_o, wtb_o, g_o, bonus_o):
    j = pl.program_id(1)
    nj = pl.num_programs(1)
    u = u_ref[0]
    tb = u.shape[0]
    prow = jnp.where(j == 0, 0.0, up_ref[0, 7:8, :])
    nrow = jnp.where(j == nj - 1, 0.0, un_ref[0, 0:1, :])
    row = lax.broadcasted_iota(jnp.int32, (tb, 1), 0)
    prev = jnp.where(row == 0, prow, pltpu.roll(u, 1, 0))
    nxt = jnp.where(row == tb - 1, nrow, pltpu.roll(u, tb - 1, 0))
    mu = mu_ref[...]
    xs = (1.0 - mu) * u + (0.5 * mu) * (prev + nxt)
    W = RW_WIDTH
    r, k, v = xs[:, 0:W], xs[:, W:2 * W], xs[:, 2 * W:3 * W]
    wd = jnp.tanh(xs[:, 3 * W:3 * W + 2 * RW_LORA]).astype(BF16)
    ad = xs[:, 3 * W + 2 * RW_LORA:3 * W + 4 * RW_LORA].astype(BF16)
    gd = jax.nn.sigmoid(xs[:, 3 * W + 4 * RW_LORA:]).astype(BF16)
    p = p_ref[...]
    kkn = k * kk_ref[...]
    kk = kkn * lax.rsqrt(jnp.maximum(_segsum(kkn * kkn, p, terms=1), 1e-24))
    ka = ka_ref[...]
    lw_f = -RW_DECAY_SCALE * jax.nn.sigmoid(w0_ref[0:1, :] + _dot(wd, w2f_ref[...]))
    lw_b = -RW_DECAY_SCALE * jax.nn.sigmoid(w0_ref[1:2, :] + _dot(wd, w2b_ref[...]))
    a_f = jax.nn.sigmoid(a0_ref[0:1, :] + _dot(ad, a2f_ref[...]))
    a_b = jax.nn.sigmoid(a0_ref[1:2, :] + _dot(ad, a2b_ref[...]))
    kd_f = k * (1.0 + (a_f - 1.0) * ka)
    kd_b = k * (1.0 + (a_b - 1.0) * ka)
    v_o[0] = v.astype(BF16)
    sel = sel_ref[...]
    ncb = sel.shape[0]
    c = tb // ncb

    def scan_operands(lw, kd, b, tri, outs):
        parts = _hi_lo(lw)
        cum = _dot_parts(tri, parts)
        wt = jnp.exp(_dot_parts(sel, parts))
        wt_tok = jnp.concatenate([jnp.broadcast_to(wt[j:j + 1], (c, wt.shape[1])) for j in range(ncb)], axis=0)
        e_neg = jnp.exp(-cum)
        kh = kd * e_neg
        bh = b * e_neg
        kt_o, rt_o, kh_o, bh_o, kb_o, bb_o, wt_o = outs
        kt_o[0] = (kk * jnp.exp(cum - lw)).astype(BF16)
        rt_o[0] = (r * jnp.exp(cum)).astype(BF16)
        kh_o[0] = kh.astype(BF16)
        bh_o[0] = bh.astype(BF16)
        kb_o[0] = (kh * wt_tok).astype(BF16)
        bb_o[0] = (bh * wt_tok).astype(BF16)
        wt_o[0] = wt

    scan_operands(lw_f, kd_f, kk * a_f, trif_ref[...], (ktf_o, rtf_o, khf_o, bhf_o, kbf_o, bbf_o, wtf_o))
    scan_operands(lw_b, kd_b, kk * a_b, trib_ref[...], (ktb_o, rtb_o, khb_o, bhb_o, kbb_o, bbb_o, wtb_o))
    g_o[0] = _dot(gd, g2_ref[...]).astype(BF16)
    bonus_o[0] = (_segsum(r * (kd_f + kd_b) * rk_ref[...], p) * v).astype(BF16)


def _head_ones():
    h = np.arange(RW_WIDTH) // RW_HEAD
    return (h[:, None] == h[None, :]).astype(BF16)


def _chunk_matrices(tb, c):
    t = np.arange(tb)
    same = (t[:, None] // c) == (t[None, :] // c)
    tri_f = (same & (t[None, :] <= t[:, None])).astype(BF16)
    tri_b = (same & (t[None, :] >= t[:, None])).astype(BF16)
    sel = (np.arange(tb // c)[:, None] == (t[None, :] // c)).astype(BF16)
    return tri_f, tri_b, sel


def _rw_prep(u_r, mu, w0, w2, a0, a2, g2, k_k, k_a, r_k, p_ones, tb=512):
    B, L, C = u_r.shape
    tb = min(tb, L)
    W = RW_WIDTH
    c = min(RW_CHUNK, L)
    ncb = tb // c
    tri_f, tri_b, sel = _chunk_matrices(tb, c)
    zeros = jnp.zeros((RW_LORA, W), F32)
    w2f = jnp.concatenate([w2[0], zeros], axis=0).astype(BF16)
    w2b = jnp.concatenate([zeros, w2[1]], axis=0).astype(BF16)
    a2f = jnp.concatenate([a2[0], zeros], axis=0).astype(BF16)
    a2b = jnp.concatenate([zeros, a2[1]], axis=0).astype(BF16)
    full = lambda shape: pl.BlockSpec(shape, lambda b, j: tuple(0 for _ in shape))
    tok = lambda: pl.BlockSpec((1, tb, W), lambda b, j: (b, j, 0))
    wts = lambda: pl.BlockSpec((1, ncb, W), lambda b, j: (b, j, 0))
    bf_tok = jax.ShapeDtypeStruct((B, L, W), BF16)
    wt_shape = jax.ShapeDtypeStruct((B, L // c, W), F32)
    g8 = tb // 8
    outs = pl.pallas_call(
        _rw_prep_kernel,
        grid=(B, L // tb),
        in_specs=[pl.BlockSpec((1, tb, C), lambda b, j: (b, j, 0)),
                  pl.BlockSpec((1, 8, C), lambda b, j: (b, jnp.maximum(j * g8 - 1, 0), 0)),
                  pl.BlockSpec((1, 8, C), lambda b, j: (b, jnp.minimum((j + 1) * g8, L // 8 - 1), 0)),
                  full((1, C)), full((2, W)), full((2 * RW_LORA, W)), full((2 * RW_LORA, W)),
                  full((2, W)), full((2 * RW_LORA, W)), full((2 * RW_LORA, W)),
                  full((RW_LORA_G, W)), full((1, W)), full((1, W)), full((1, W)), full((W, W)),
                  full((tb, tb)), full((tb, tb)), full((ncb, tb))],
        out_specs=[tok()] + ([tok() for _ in range(6)] + [wts()]) * 2 + [tok(), tok()],
        out_shape=[bf_tok] + ([bf_tok] * 6 + [wt_shape]) * 2 + [bf_tok, bf_tok],
        compiler_params=_cparams(("parallel", "parallel")),
        name="rw_prep",
    )(u_r, u_r, u_r, mu.reshape(1, C), w0, w2f, w2b, a0, a2f, a2b, g2.astype(BF16),
      k_k.reshape(1, W), k_a.reshape(1, W), r_k.reshape(1, W), p_ones, tri_f, tri_b, sel)
    v, g, bonus = outs[0], outs[15], outs[16]
    fwd_ops = tuple(outs[1:7]) + (v, outs[7])
    bwd_ops = tuple(outs[8:14]) + (v, outs[14])
    return fwd_ops, bwd_ops, g, bonus, ncb


def _bmm(a, b):
    return lax.dot_general(a, b, (((2,), (1,)), ((0,), (0,))), preferred_element_type=F32)


def _bmm_nt(a, b):
    return lax.dot_general(a, b, (((2,), (2,)), ((0,), (0,))), preferred_element_type=F32)


def _bmm_tn(a, b):
    return lax.dot_general(a, b, (((1,), (1,)), ((0,), (0,))), preferred_element_type=F32)


RW_QUAD = 4 * RW_HEAD
RW_SEQS = 4


def _block_diag(x, same_head):
    reps = RW_QUAD // x.shape[1]
    return jnp.where(same_head, jnp.concatenate([x] * reps, axis=1), jnp.zeros((), x.dtype))


def _quads(x_f, x_b):
    halves = lambda x: [x[n, :, s:s + RW_QUAD] for n in range(x.shape[0]) for s in range(0, x.shape[2], RW_QUAD)]
    return jnp.stack(halves(x_f) + halves(x_b), axis=0)


def _rw_scan_kernel(nc, ncb, ktf, rtf, khf, bhf, kbf, bbf, vf, wtf, ktb, rtb, khb, bhb, kbb, bbb, vb, wtb,
                    yf_ref, yb_ref, s_ref):
    i = pl.program_id(1)

    @pl.when(i == 0)
    def _():
        s_ref[...] = jnp.zeros_like(s_ref)

    kt, rt = _quads(ktf[...], ktb[...]), _quads(rtf[...], rtb[...])
    khat, bhat = _quads(khf[...], khb[...]), _quads(bhf[...], bhb[...])
    kbar, bbar = _quads(kbf[...], kbb[...]), _quads(bbf[...], bbb[...])
    v = _quads(vf[...], vb[...])
    wt = _quads(wtf[:, pl.ds(i % ncb, 1), :], wtb[:, pl.ds((nc - 1 - i) % ncb, 1), :])
    g, c, q = kt.shape
    ri = lax.broadcasted_iota(jnp.int32, (g, c, q), 1)
    ci = lax.broadcasted_iota(jnp.int32, (g, c, q), 2) % c
    rev = lax.broadcasted_iota(jnp.int32, (g, c, q), 0) >= g // 2
    ahead = jnp.where(rev, ri - ci, ci - ri)
    strict = ahead < 0
    causal = ahead <= 0
    eye = jnp.where(ahead == 0, 1.0, 0.0)
    same_head = (lax.broadcasted_iota(jnp.int32, (1, q, q), 1) // RW_HEAD
                 == lax.broadcasted_iota(jnp.int32, (1, q, q), 2) // RW_HEAD)
    bd = lambda x: _block_diag(x, same_head)

    lh = jnp.concatenate([kt, rt], axis=1)
    gk = _bmm_nt(lh, bd(khat))
    gb = _bmm_nt(lh, bd(bhat))
    a_k = jnp.where(strict, gk[:, :c], 0.0)
    b_k = jnp.where(causal, gk[:, c:], 0.0)
    a_b = jnp.where(strict, gb[:, :c], 0.0)
    b_b = jnp.where(causal, gb[:, c:], 0.0)
    p = -a_b
    tmat = eye + p
    p = _bmm(p.astype(BF16), bd(p.astype(BF16)))
    doublings = int(math.log2(c)) - 1
    for step in range(doublings):
        p_bd = bd(p.astype(BF16))
        if step == doublings - 1:
            tmat = tmat + _bmm(tmat.astype(BF16), p_bd)
        else:
            both = _bmm(jnp.concatenate([p, tmat], axis=1).astype(BF16), p_bd)
            tmat = tmat + both[:, c:]
            p = both[:, :c]
    s0 = s_ref[...]
    ks = _bmm_nt(lh, s0.astype(BF16))
    av = _bmm(jnp.concatenate([a_k, b_k], axis=1).astype(BF16), bd(v))
    ub = _bmm(tmat.astype(BF16), bd((ks[:, :c] + av[:, :c]).astype(BF16))).astype(BF16)
    y = ks[:, c:] + av[:, c:] - _bmm(b_b.astype(BF16), bd(ub))
    upd = _bmm_tn(jnp.concatenate([v, ub], axis=1), jnp.concatenate([kbar, -bbar], axis=1))
    s_ref[...] = s0 * wt + jnp.where(same_head, upd, 0.0)
    per_seq = yf_ref.shape[2] // q
    for n in range(yf_ref.shape[0]):
        yf_ref[n] = jnp.concatenate([y[n * per_seq + j] for j in range(per_seq)], axis=1).astype(BF16)
        yb_ref[n] = jnp.concatenate([y[g // 2 + n * per_seq + j] for j in range(per_seq)], axis=1).astype(BF16)


def _rw_scan(fwd_ops, bwd_ops, ncb):
    B, L, W = fwd_ops[0].shape
    c = min(RW_CHUNK, L)
    nc = L // c
    nb = RW_SEQS if B % RW_SEQS == 0 else 1
    fwd = lambda: pl.BlockSpec((nb, c, W), lambda b, i: (b, i, 0))
    bwd = lambda: pl.BlockSpec((nb, c, W), lambda b, i: (b, nc - 1 - i, 0))
    wt_f = pl.BlockSpec((nb, ncb, W), lambda b, i: (b, i // ncb, 0))
    wt_b = pl.BlockSpec((nb, ncb, W), lambda b, i: (b, (nc - 1 - i) // ncb, 0))
    return pl.pallas_call(
        functools.partial(_rw_scan_kernel, nc, ncb),
        grid=(B // nb, nc),
        in_specs=[fwd() for _ in range(7)] + [wt_f] + [bwd() for _ in range(7)] + [wt_b],
        out_specs=[fwd(), bwd()],
        out_shape=[jax.ShapeDtypeStruct((B, L, W), BF16)] * 2,
        scratch_shapes=[pltpu.VMEM((2 * nb * W // RW_QUAD, RW_QUAD, RW_QUAD), F32)],
        compiler_params=_cparams(("parallel", "arbitrary")),
        name="rw_scan",
    )(*fwd_ops, *bwd_ops)


def _merge_kernel(x_ref, yh_ref, yf_ref, yb_ref, bonus_ref, g_ref, gates_ref, p_ref, gnw_ref, gnb_ref,
                  why_ref, wrw_ref, wo_ref, lnw_ref, lnb_ref, o_ref):
    p = p_ref[...]
    y = yf_ref[...].astype(F32) + yb_ref[...].astype(F32)
    mu = _segsum(y, p) * (1.0 / RW_HEAD)
    yc = y - mu
    var = _segsum(yc * yc, p, terms=1) * (1.0 / RW_HEAD)
    yn = yc * lax.rsqrt(var + RW_GN_EPS) * gnw_ref[...] + gnb_ref[...]
    y_r = (yn + bonus_ref[...].astype(F32)) * g_ref[...].astype(F32)
    ph = _dot(_tiles_cat(yh_ref, lambda q: (0, q), yh_ref.shape[1]).astype(BF16), why_ref[...])
    pr = _dot(y_r.astype(BF16), wrw_ref[...])
    gates = jax.nn.sigmoid(gates_ref[...].astype(F32))
    m = gates[:, :D_MODEL] * ph + gates[:, D_MODEL:] * pr
    mix = _dot(m.astype(BF16), wo_ref[...])
    o_ref[...] = _layer_norm(DN_ALPHA * x_ref[...] + mix, lnw_ref[...], lnb_ref[...])


def _merge(x2, yh, yf, yb, bonus, g, gates, p_ones, gn_w, gn_b, w_hy_out, w_rw_out, w_o, ln_w, ln_b, tm=512):
    m, d = x2.shape
    _, tiles, L, _ = yh.shape
    tm = min(tm, L)
    per_seq = L // tm
    W = RW_WIDTH
    row = lambda width: pl.BlockSpec((tm, width), lambda i: (i, 0))
    full = lambda shape: pl.BlockSpec(shape, lambda i: tuple(0 for _ in shape))
    return pl.pallas_call(
        _merge_kernel,
        grid=(m // tm,),
        in_specs=[row(d), pl.BlockSpec((1, tiles, tm, LANES), lambda i: (i // per_seq, 0, i % per_seq, 0)),
                  row(W), row(W), row(W), row(W), row(GATE_COLS),
                  full((W, W)), full((1, W)), full((1, W)),
                  full((HY_WIDTH, d)), full((W, d)), full((d, d)), full((1, d)), full((1, d))],
        out_specs=row(d),
        out_shape=jax.ShapeDtypeStruct((m, d), F32),
        compiler_params=_cparams(("parallel",)),
        name="merge_ln1",
    )(x2, yh, yf, yb, bonus, g, gates, p_ones, gn_w.reshape(1, W), gn_b.reshape(1, W),
      w_hy_out.astype(BF16), w_rw_out.astype(BF16), w_o.astype(BF16), ln_w.reshape(1, d), ln_b.reshape(1, d))


def _ffn_kernel(x_ref, wg_ref, wu_ref, wd_ref, lnw_ref, lnb_ref, o_ref):
    x = x_ref[...]
    xb = x.astype(BF16)
    hidden = jax.nn.silu(_dot(xb, wg_ref[...])) * _dot(xb, wu_ref[...])
    ffn = _dot(hidden.astype(BF16), wd_ref[...])
    o_ref[...] = _layer_norm(DN_ALPHA * x + ffn, lnw_ref[...], lnb_ref[...])


def _ffn(x2, w_gate, w_up, w_down, ln_w, ln_b, tm=1024):
    m, d = x2.shape
    tm = min(tm, m)
    fh = w_gate.shape[1]
    resident = lambda shape: pl.BlockSpec(shape, lambda i: (0, 0), pipeline_mode=pl.Buffered(1))
    return pl.pallas_call(
        _ffn_kernel,
        grid=(m // tm,),
        in_specs=[pl.BlockSpec((tm, d), lambda i: (i, 0)),
                  resident((d, fh)), resident((d, fh)), resident((fh, d)),
                  resident((1, d)), resident((1, d))],
        out_specs=pl.BlockSpec((tm, d), lambda i: (i, 0)),
        out_shape=jax.ShapeDtypeStruct((m, d), F32),
        compiler_params=_cparams(("parallel",)),
        name="ffn_ln2",
    )(x2, w_gate.astype(BF16), w_up.astype(BF16), w_down.astype(BF16), ln_w.reshape(1, d), ln_b.reshape(1, d))


def _layer(x, w_in, hy_conv_w, hy_conv_b, hy_filt_w1, hy_filt_b1, hy_filt_w2, hy_filt_b2,
           hy_filt_w3, hy_filt_b3, hy_filt_w4, hy_sin_freq, hy_skip, rw_mu, rw_w0, rw_w2,
           rw_a0, rw_a2, rw_g2, rw_k_k, rw_k_a, rw_r_k, rw_gn_w, rw_gn_b, w_hy_out, w_rw_out,
           w_o, ln1_w, ln1_b, ffn_w_gate, ffn_w_up, ffn_w_down, ln2_w, ln2_b):
    B, L, D = x.shape
    x2 = x.reshape(B * L, D)
    u3, u_r, gates = _in_proj(x2, L, w_in.astype(BF16), hy_conv_w, hy_conv_b)
    y_h = _hyena(u3, hy_filt_w1, hy_filt_b1, hy_filt_w2, hy_filt_b2, hy_filt_w3, hy_filt_b3, hy_filt_w4,
                 hy_sin_freq, hy_skip)
    p_ones = _head_ones()
    fwd_ops, bwd_ops, g, bonus, ncb = _rw_prep(
        u_r.reshape(B, L, RW_COLS), rw_mu, rw_w0, rw_w2, rw_a0, rw_a2, rw_g2, rw_k_k, rw_k_a, rw_r_k, p_ones)
    yf, yb = _rw_scan(fwd_ops, bwd_ops, ncb)
    flat = lambda a: a.reshape(B * L, a.shape[-1])
    h = _merge(x2, y_h, flat(yf), flat(yb), flat(bonus), flat(g), gates, p_ones, rw_gn_w, rw_gn_b,
               w_hy_out, w_rw_out, w_o, ln1_w, ln1_b)
    out = _ffn(h, ffn_w_gate, ffn_w_up, ffn_w_down, ln2_w, ln2_b)
    return out.reshape(B, L, D)


def kernel(x, w_in, hy_conv_w, hy_conv_b, hy_filt_w1, hy_filt_b1, hy_filt_w2, hy_filt_b2, hy_filt_w3, hy_filt_b3, hy_filt_w4, hy_sin_freq, hy_skip, rw_mu, rw_w0, rw_w2, rw_a0, rw_a2, rw_g2, rw_k_k, rw_k_a, rw_r_k, rw_gn_w, rw_gn_b, w_hy_out, w_rw_out, w_o, ln1_w, ln1_b, ffn_w_gate, ffn_w_up, ffn_w_down, ln2_w, ln2_b):
    params = (w_in, hy_conv_w, hy_conv_b, hy_filt_w1, hy_filt_b1, hy_filt_w2, hy_filt_b2, hy_filt_w3,
              hy_filt_b3, hy_filt_w4, hy_sin_freq, hy_skip, rw_mu, rw_w0, rw_w2, rw_a0, rw_a2, rw_g2,
              rw_k_k, rw_k_a, rw_r_k, rw_gn_w, rw_gn_b, w_hy_out, w_rw_out, w_o, ln1_w, ln1_b,
              ffn_w_gate, ffn_w_up, ffn_w_down, ln2_w, ln2_b)
    for l in range(w_in.shape[0]):
        x = _layer(x, *[p[l] for p in params])
    return x
```

```python
import functools
import math

import numpy as np
import jax
import jax.numpy as jnp
from jax import lax
from jax.experimental import pallas as pl
from jax.experimental.pallas import tpu as pltpu

F32 = jnp.float32
BF16 = jnp.bfloat16
HIGHEST = lax.Precision.HIGHEST

D_MODEL = 1024
HY_WIDTH = 512
HY_ORDER = 2
HY_BANDS = 16
HY_FILT_HIDDEN = 64
HY_FAST_DECAY = 0.3
HY_SLOW_DECAY = 1.5
HY_DECAY_TARGET = 1e-2
HY_MAX_DECAY = math.log(HY_DECAY_TARGET) / HY_FAST_DECAY
HY_MIN_DECAY = math.log(HY_DECAY_TARGET) / HY_SLOW_DECAY
HY_COLS = 3 * HY_WIDTH
RW_WIDTH = 512
RW_HEAD = 64
RW_HEADS = RW_WIDTH // RW_HEAD
RW_LORA = 64
RW_LORA_G = 128
RW_GN_EPS = 64e-5
RW_COLS = 3 * RW_WIDTH + 4 * RW_LORA + RW_LORA_G
GATE_COLS = 2 * D_MODEL
FFN_HIDDEN = ((8 * D_MODEL + 3 * 256 - 1) // (3 * 256)) * 256
DEPTH = 1
DN_ALPHA = (2.0 * DEPTH) ** 0.25
LN_EPS = 1e-5
RW_DECAY_SCALE = math.exp(-0.5)

LANES = 128
VMEM_LIMIT = 56 * 1024 * 1024

FFT_N1 = 32
FFT_F1 = FFT_N1 // 2 + 1
RW_CHUNK = 64


def _cparams(sem, vmem=VMEM_LIMIT):
    return pltpu.CompilerParams(dimension_semantics=sem, vmem_limit_bytes=vmem)


def _dot(a, b, precision=None):
    return jnp.dot(a, b, preferred_element_type=F32, precision=precision)


def _dot_nt(a, b):
    return lax.dot_general(a, b, (((1,), (1,)), ((), ())), preferred_element_type=F32)


def _dot_tn(a, b):
    return lax.dot_general(a, b, (((0,), (0,)), ((), ())), preferred_element_type=F32)


def _layer_norm(h, w, b):
    mu = jnp.mean(h, axis=-1, keepdims=True)
    c = h - mu
    var = jnp.mean(c * c, axis=-1, keepdims=True)
    return c * lax.rsqrt(var + LN_EPS) * w + b


def _hi_lo(x):
    hi = x.astype(BF16)
    return hi, (x - hi.astype(F32)).astype(BF16)


def _segsum(x, p, terms=2):
    q = p.shape[0]
    parts = _hi_lo(x)[:terms]
    return jnp.concatenate([_dot_parts_t([t[:, s:s + q] for t in parts], p) for s in range(0, x.shape[1], q)], axis=1)


def _dot_hl(a, b):
    a_hi, a_lo = _hi_lo(a)
    b_hi, b_lo = _hi_lo(b)
    return _dot(a_hi, b_hi) + _dot(a_lo, b_hi) + _dot(a_hi, b_lo)


def _dot_parts_t(parts, m):
    out = _dot(parts[0], m)
    for part in parts[1:]:
        out = out + _dot(part, m)
    return out


def _dot_parts(m, parts):
    out = _dot(m, parts[0])
    for part in parts[1:]:
        out = out + _dot(m, part)
    return out


HY_TILES = HY_WIDTH // LANES
HALO = 8


def _in_proj_kernel(per_seq, x_ref, xp_ref, xn_ref, w_ref, cw_ref, cb_ref, u3_ref, ur_ref, g_ref):
    pos = pl.program_id(0) % per_seq
    w_h = w_ref[:, :HY_COLS]
    xb = x_ref[...].astype(BF16)
    tm = xb.shape[0]
    ext = _dot(jnp.concatenate([xp_ref[...].astype(BF16), xb, xn_ref[...].astype(BF16)], axis=0), w_h)
    u = ext[HALO:HALO + tm]
    row = lax.broadcasted_iota(jnp.int32, (tm, 1), 0)
    prev = jnp.where((row == 0) & (pos == 0), 0.0, ext[HALO - 1:HALO - 1 + tm])
    nxt = jnp.where((row == tm - 1) & (pos == per_seq - 1), 0.0, ext[HALO + 1:HALO + 1 + tm])
    conv = cw_ref[0:1, :] * prev + cw_ref[1:2, :] * u + cw_ref[2:3, :] * nxt + cb_ref[...]
    for s in range(HY_COLS // HY_WIDTH):
        for q in range(HY_TILES):
            lo = s * HY_WIDTH + q * LANES
            u3_ref[s, 0, q] = conv[:, lo:lo + LANES].astype(BF16)
    ur_ref[...] = _dot(xb, w_ref[:, HY_COLS:HY_COLS + RW_COLS])
    g_ref[...] = _dot(xb, w_ref[:, HY_COLS + RW_COLS:]).astype(BF16)


def _in_proj(x2, L, w_in_bf, conv_w, conv_b, tm=512):
    m, d = x2.shape
    n = w_in_bf.shape[1]
    tm = min(tm, L)
    per_seq = L // tm
    groups = tm // HALO
    resident = lambda shape: pl.BlockSpec(shape, lambda i: (0, 0), pipeline_mode=pl.Buffered(1))
    slabs = HY_COLS // HY_WIDTH
    return pl.pallas_call(
        functools.partial(_in_proj_kernel, per_seq),
        grid=(m // tm,),
        in_specs=[pl.BlockSpec((tm, d), lambda i: (i, 0)),
                  pl.BlockSpec((HALO, d), lambda i: (jnp.maximum(i * groups - 1, 0), 0)),
                  pl.BlockSpec((HALO, d), lambda i: (jnp.minimum((i + 1) * groups, m // HALO - 1), 0)),
                  resident((d, n)), resident((3, HY_COLS)), resident((1, HY_COLS))],
        out_specs=[pl.BlockSpec((slabs, 1, HY_TILES, tm, LANES), lambda i: (0, i // per_seq, 0, i % per_seq, 0)),
                   pl.BlockSpec((tm, RW_COLS), lambda i: (i, 0)),
                   pl.BlockSpec((tm, GATE_COLS), lambda i: (i, 0))],
        out_shape=[jax.ShapeDtypeStruct((slabs, m // L, HY_TILES, L, LANES), BF16),
                   jax.ShapeDtypeStruct((m, RW_COLS), F32),
                   jax.ShapeDtypeStruct((m, GATE_COLS), BF16)],
        compiler_params=_cparams(("parallel",)),
        name="in_proj",
    )(x2, x2, x2, w_in_bf, conv_w, conv_b.reshape(1, HY_COLS))


def _filter_kernel(L, tb, w1t_ref, w1c_ref, w1s_ref, b1_ref, w2_ref, b2_ref, w3_ref, b3_ref,
                   w4_ref, sf_ref, freq_ref, delta_ref, kc_ref):
    n = 2 * L
    i = pl.program_id(0)
    row = i * tb + lax.broadcasted_iota(jnp.int32, (tb, 1), 0)
    second = row >= L
    pos = jnp.where(second, n - row, row).astype(F32)
    t = pos / float(L - 1)
    ang = (2.0 * math.pi * pos / float(L)) * freq_ref[...]
    pre = t * w1t_ref[...] + _dot(jnp.cos(ang), w1c_ref[...], HIGHEST) \
        - _dot(jnp.sin(ang), w1s_ref[...], HIGHEST) + b1_ref[...]
    h = jnp.sin(sf_ref[0:1, :] * pre)
    h = jnp.sin(sf_ref[1:2, :] * (_dot(h, w2_ref[...], HIGHEST) + b2_ref[...]))
    h = jnp.sin(sf_ref[2:3, :] * (_dot(h, w3_ref[...], HIGHEST) + b3_ref[...]))
    h4 = _dot_hl(h, w4_ref[...])
    window = jnp.exp(-t * delta_ref[...])
    keep = jnp.where(row == L, 0.0, 1.0)
    first = jnp.where(row == 0, 1.0, 0.0)
    for o in range(HY_ORDER):
        base = o * 2 * HY_WIDTH
        fwd = h4[:, base:base + HY_WIDTH]
        bwd = h4[:, base + HY_WIDTH:base + 2 * HY_WIDTH]
        kc = window * (jnp.where(second, bwd, fwd) * keep + first * bwd)
        for q in range(HY_TILES):
            kc_ref[o, q] = kc[:, q * LANES:(q + 1) * LANES]


def _filters(L, fw1, fb1, fw2, fb2, fw3, fb3, fw4, sin_freq, tb=512):
    n = 2 * L
    tb = min(tb, n)
    freqs = jnp.linspace(1e-4, HY_BANDS - 1, HY_BANDS, dtype=F32).reshape(1, HY_BANDS)
    deltas = jnp.abs(jnp.linspace(HY_MIN_DECAY, HY_MAX_DECAY, HY_WIDTH, dtype=F32)).reshape(1, HY_WIDTH)
    hid = HY_FILT_HIDDEN
    full = lambda shape: pl.BlockSpec(shape, lambda i: tuple(0 for _ in shape))
    return pl.pallas_call(
        functools.partial(_filter_kernel, L, tb),
        grid=(n // tb,),
        in_specs=[full((1, hid)), full((HY_BANDS, hid)), full((HY_BANDS, hid)), full((1, hid)),
                  full((hid, hid)), full((1, hid)), full((hid, hid)), full((1, hid)),
                  full((hid, HY_ORDER * 2 * HY_WIDTH)), full((3, hid)),
                  full((1, HY_BANDS)), full((1, HY_WIDTH))],
        out_specs=pl.BlockSpec((HY_ORDER, HY_TILES, tb, LANES), lambda i: (0, 0, i, 0)),
        out_shape=jax.ShapeDtypeStruct((HY_ORDER, HY_TILES, n, LANES), F32),
        compiler_params=_cparams(("parallel",)),
        name="hy_filters",
    )(fw1[0:1], fw1[1:1 + HY_BANDS], fw1[1 + HY_BANDS:], fb1.reshape(1, hid), fw2, fb2.reshape(1, hid),
      fw3, fb3.reshape(1, hid), fw4, sin_freq, freqs, deltas)


def _fft_tables(n2):
    n = FFT_N1 * n2
    f1 = np.arange(FFT_F1)[:, None, None]
    a = np.arange(n2)[None, :, None]
    b = np.arange(n2)[None, None, :]
    ph = ((b * (f1 + FFT_N1 * a)) % n) * (2.0 * math.pi / n)
    cr, ci = np.cos(ph), -np.sin(ph)
    m_fwd = np.concatenate([np.concatenate([cr, -ci], axis=2), np.concatenate([ci, cr], axis=2)], axis=1)
    herm = np.where((f1 == 0) | (f1 == FFT_N1 // 2), 1.0, 2.0) / n
    dr, di = herm * cr.transpose(0, 2, 1), -herm * ci.transpose(0, 2, 1)
    m_inv = np.concatenate([np.concatenate([dr, -di], axis=2), np.concatenate([di, dr], axis=2)], axis=1)
    return jnp.asarray(m_fwd, F32).astype(BF16), jnp.asarray(m_inv, F32).astype(BF16)


def _slow_tables():
    k = (np.arange(FFT_F1)[:, None] * np.arange(FFT_N1)[None, :]) % FFT_N1
    ph = k * (2.0 * math.pi / FFT_N1)
    fwd = np.stack([np.cos(ph), -np.sin(ph)], axis=1).reshape(2 * FFT_F1, FFT_N1)
    inv = fwd[:, :FFT_N1 // 2].T
    return fwd, inv


SLOW_UNROLL = 4
ROW_GROUP = 16
SLOW_TILES = 2


def _kron_hi_lo(m):
    return jnp.concatenate(_hi_lo(jnp.asarray(np.kron(m, np.eye(ROW_GROUP)), F32)), axis=0)


def _slow_dft_kernel(n2, s1_count, f_ref, z_ref, a_ref):
    f_hl = f_ref[...]
    half = f_hl.shape[0] // 2
    st = z_ref.shape[2]

    def body(g, carry):
        s2 = pl.multiple_of(g * ROW_GROUP, ROW_GROUP)
        zs = jnp.concatenate(
            [_tiles_cat(z_ref, lambda q: (0, 0, q, pl.ds(s1 * n2 + s2, ROW_GROUP)), st) for s1 in range(s1_count)],
            axis=0).astype(BF16)
        a = _dot(f_hl, zs)
        a = (a[:half] + a[half:]).astype(BF16)
        for f in range(half // ROW_GROUP):
            _tiles_put(a_ref, lambda q: (0, q, pl.ds(f * n2 + s2, ROW_GROUP)), a[f * ROW_GROUP:(f + 1) * ROW_GROUP])
        return carry

    lax.fori_loop(0, n2 // ROW_GROUP, body, 0, unroll=min(SLOW_UNROLL, n2 // ROW_GROUP))


def _tiles_cat(ref, index, ntiles):
    return jnp.concatenate([ref[index(q) + (slice(None),)] for q in range(ntiles)], axis=1)


def _tiles_put(ref, index, value):
    for q in range(value.shape[1] // LANES):
        ref[index(q) + (slice(None),)] = value[:, q * LANES:(q + 1) * LANES]


def _slow_dft(z5, sel, s1_count):
    _, B, tiles, T, _ = z5.shape
    n2 = T // s1_count
    fwd, _ = _slow_tables()
    rows = 2 * FFT_F1
    f_hl = _kron_hi_lo(fwd[:, :s1_count])
    st = SLOW_TILES
    return pl.pallas_call(
        functools.partial(_slow_dft_kernel, n2, s1_count),
        grid=(B, tiles // st),
        in_specs=[pl.BlockSpec(f_hl.shape, lambda b, j: (0, 0)),
                  pl.BlockSpec((1, 1, st, T, LANES), lambda b, j: (sel, b, j, 0, 0))],
        out_specs=pl.BlockSpec((1, st, rows * n2, LANES), lambda b, j: (b, j, 0, 0)),
        out_shape=jax.ShapeDtypeStruct((B, tiles, rows * n2, LANES), BF16),
        compiler_params=_cparams(("parallel", "parallel")),
        name="hy_slow_dft",
    )(f_hl, z5)


def _spectrum_kernel(n2, a_ref, mf_ref, hr_ref, hi_ref):
    x = _dot(mf_ref[0], _tiles_cat(a_ref, lambda q: (0, q), a_ref.shape[1]))
    hr_ref[0] = x[:n2]
    hi_ref[0] = x[n2:]


def _spectrum(kc, m_fwd):
    order, tiles, n, _ = kc.shape
    n2 = n // FFT_N1
    C = tiles * LANES
    a = _slow_dft(kc.reshape(1, order, tiles, n, LANES), 0, FFT_N1)
    spec = lambda: pl.BlockSpec((1, n2, C), lambda f, o: (o, f, 0))
    return pl.pallas_call(
        functools.partial(_spectrum_kernel, n2),
        grid=(FFT_F1, order),
        in_specs=[pl.BlockSpec((1, tiles, 2 * n2, LANES), lambda f, o: (o, 0, f, 0)),
                  pl.BlockSpec((1, 2 * n2, 2 * n2), lambda f, o: (f, 0, 0))],
        out_specs=[spec(), spec()],
        out_shape=[jax.ShapeDtypeStruct((order, FFT_F1 * n2, C), F32)] * 2,
        compiler_params=_cparams(("parallel", "parallel")),
        name="hy_spectrum",
    )(a, m_fwd)


MID_SEQS = 2


def _conv_mid_kernel(n2, a_ref, hr_ref, hi_ref, mf_ref, mi_ref, b_ref):
    hr, hi = hr_ref[0], hi_ref[0]
    for n in range(a_ref.shape[0]):
        x = _dot(mf_ref[0], _tiles_cat(a_ref, lambda q: (n, q), a_ref.shape[1]))
        xr, xi = x[:n2], x[n2:]
        y = jnp.concatenate([xr * hr - xi * hi, xr * hi + xi * hr], axis=0).astype(BF16)
        _tiles_put(b_ref, lambda q: (n, q), _dot(mi_ref[0], y).astype(BF16))


def _conv_mid(a, order, hr, hi, m_fwd, m_inv):
    B, tiles, total, _ = a.shape
    rows = total // FFT_F1
    n2 = rows // 2
    W = tiles * LANES
    nb = MID_SEQS if B % MID_SEQS == 0 else 1
    blk = lambda: pl.BlockSpec((nb, tiles, rows, LANES), lambda f, b: (b, 0, f, 0))
    return pl.pallas_call(
        functools.partial(_conv_mid_kernel, n2),
        grid=(FFT_F1, B // nb),
        in_specs=[blk(),
                  pl.BlockSpec((1, n2, W), lambda f, b: (order, f, 0)),
                  pl.BlockSpec((1, n2, W), lambda f, b: (order, f, 0)),
                  pl.BlockSpec((1, rows, rows), lambda f, b: (f, 0, 0)),
                  pl.BlockSpec((1, rows, rows), lambda f, b: (f, 0, 0))],
        out_specs=blk(),
        out_shape=jax.ShapeDtypeStruct((B, tiles, total, LANES), BF16),
        compiler_params=_cparams(("parallel", "parallel")),
        name=f"hy_conv_mid{order}",
    )(a, hr, hi, m_fwd, m_inv)


IDFT_ROWS = tuple(f for f in range(2 * FFT_F1) if f not in (1, 2 * FFT_F1 - 1))


def _slow_idft_kernel(n2, g_ref, b_ref, z_ref, gate_ref, skip_ref, o_ref):
    g_hl = g_ref[...]
    half = g_hl.shape[0] // 2
    skip = skip_ref[...]
    st = b_ref.shape[1]

    def body(g, carry):
        t2 = pl.multiple_of(g * ROW_GROUP, ROW_GROUP)
        bs = jnp.concatenate(
            [_tiles_cat(b_ref, lambda q: (0, q, pl.ds(f * n2 + t2, ROW_GROUP)), st) for f in IDFT_ROWS], axis=0)
        y2 = _dot(g_hl, bs)
        y = y2[:half] + y2[half:]
        for t1 in range(half // ROW_GROUP):
            tok = lambda q: (0, 0, q, pl.ds(t1 * n2 + t2, ROW_GROUP))
            z = _tiles_cat(z_ref, tok, st)
            gate = _tiles_cat(gate_ref, tok, st)
            out = gate * (y[t1 * ROW_GROUP:(t1 + 1) * ROW_GROUP] + skip * z)
            _tiles_put(o_ref, tok, out.astype(o_ref.dtype))
        return carry

    lax.fori_loop(0, n2 // ROW_GROUP, body, 0, unroll=min(SLOW_UNROLL, n2 // ROW_GROUP))


def _slow_idft(bm, z5, zsel, g5, gsel, skip_row, out_dtype):
    B, tiles, total, _ = bm.shape
    t1 = FFT_N1 // 2
    n2 = total // (2 * FFT_F1)
    T = t1 * n2
    _, inv = _slow_tables()
    g_hl = _kron_hi_lo(inv[:, list(IDFT_ROWS)])
    st = SLOW_TILES
    tok = lambda sel: pl.BlockSpec((1, 1, st, T, LANES), lambda b, j: (sel, b, j, 0, 0))
    return pl.pallas_call(
        functools.partial(_slow_idft_kernel, n2),
        grid=(B, tiles // st),
        in_specs=[pl.BlockSpec(g_hl.shape, lambda b, j: (0, 0)),
                  pl.BlockSpec((1, st, total, LANES), lambda b, j: (b, j, 0, 0)),
                  tok(zsel), tok(gsel),
                  pl.BlockSpec((1, st * LANES), lambda b, j: (0, j))],
        out_specs=tok(0),
        out_shape=jax.ShapeDtypeStruct((1, B, tiles, T, LANES), out_dtype),
        compiler_params=_cparams(("parallel", "parallel")),
        name="hy_slow_idft",
    )(g_hl, bm, z5, g5, skip_row.reshape(1, tiles * LANES))


def _longconv(z5, zsel, g5, gsel, order, skip, hr, hi, m_fwd, m_inv, out_dtype):
    a = _slow_dft(z5, zsel, FFT_N1 // 2)
    bm = _conv_mid(a, order, hr, hi, m_fwd, m_inv)
    return _slow_idft(bm, z5, zsel, g5, gsel, skip[order], out_dtype)


def _hyena(u3, fw1, fb1, fw2, fb2, fw3, fb3, fw4, sin_freq, skip):
    L = u3.shape[3]
    kc = _filters(L, fw1, fb1, fw2, fb2, fw3, fb3, fw4, sin_freq)
    m_fwd, m_inv = _fft_tables(2 * L // FFT_N1)
    hr, hi = _spectrum(kc, m_fwd)
    z1 = _longconv(u3, 0, u3, 1, 0, skip, hr, hi, m_fwd, m_inv, BF16)
    return _longconv(z1, 0, u3, 2, 1, skip, hr, hi, m_fwd, m_inv, BF16)[0]


def _rw_prep_kernel(u_ref, up_ref, un_ref, mu_ref, w0_ref, w2f_ref, w2b_ref, a0_ref, a2f_ref, a2b_ref,
                    g2_ref, kk_ref, ka_ref, rk_ref, p_ref, trif_ref, trib_ref, sel_ref,
                    v_o, ktf_o, rtf_o, khf_o, bhf_o, kbf_o, bbf_o, wtf_o,
                    ktb_o, rtb_o, khb_o, bhb_o, kbb_o, bbb_o, wtb_o, g_o, bonus_o):
    j = pl.program_id(1)
    nj = pl.num_programs(1)
    u = u_ref[0]
    tb = u.shape[0]
    prow = jnp.where(j == 0, 0.0, up_ref[0, 7:8, :])
    nrow = jnp.where(j == nj - 1, 0.0, un_ref[0, 0:1, :])
    row = lax.broadcasted_iota(jnp.int32, (tb, 1), 0)
    prev = jnp.where(row == 0, prow, pltpu.roll(u, 1, 0))
    nxt = jnp.where(row == tb - 1, nrow, pltpu.roll(u, tb - 1, 0))
    mu = mu_ref[...]
    xs = (1.0 - mu) * u + (0.5 * mu) * (prev + nxt)
    W = RW_WIDTH
    r, k, v = xs[:, 0:W], xs[:, W:2 * W], xs[:, 2 * W:3 * W]
    wd = jnp.tanh(xs[:, 3 * W:3 * W + 2 * RW_LORA]).astype(BF16)
    ad = xs[:, 3 * W + 2 * RW_LORA:3 * W + 4 * RW_LORA].astype(BF16)
    gd = jax.nn.sigmoid(xs[:, 3 * W + 4 * RW_LORA:]).astype(BF16)
    p = p_ref[...]
    kkn = k * kk_ref[...]
    kk = kkn * lax.rsqrt(jnp.maximum(_segsum(kkn * kkn, p, terms=1), 1e-24))
    ka = ka_ref[...]
    lw_f = -RW_DECAY_SCALE * jax.nn.sigmoid(w0_ref[0:1, :] + _dot(wd, w2f_ref[...]))
    lw_b = -RW_DECAY_SCALE * jax.nn.sigmoid(w0_ref[1:2, :] + _dot(wd, w2b_ref[...]))
    a_f = jax.nn.sigmoid(a0_ref[0:1, :] + _dot(ad, a2f_ref[...]))
    a_b = jax.nn.sigmoid(a0_ref[1:2, :] + _dot(ad, a2b_ref[...]))
    kd_f = k * (1.0 + (a_f - 1.0) * ka)
    kd_b = k * (1.0 + (a_b - 1.0) * ka)
    v_o[0] = v.astype(BF16)
    sel = sel_ref[...]
    ncb = sel.shape[0]
    c = tb // ncb

    def scan_operands(lw, kd, b, tri, outs):
        parts = _hi_lo(lw)
        cum = _dot_parts(tri, parts)
        wt = jnp.exp(_dot_parts(sel, parts))
        wt_tok = jnp.concatenate([jnp.broadcast_to(wt[j:j + 1], (c, wt.shape[1])) for j in range(ncb)], axis=0)
        e_neg = jnp.exp(-cum)
        kh = kd * e_neg
        bh = b * e_neg
        kt_o, rt_o, kh_o, bh_o, kb_o, bb_o, wt_o = outs
        kt_o[0] = (kk * jnp.exp(cum - lw)).astype(BF16)
        rt_o[0] = (r * jnp.exp(cum)).astype(BF16)
        kh_o[0] = kh.astype(BF16)
        bh_o[0] = bh.astype(BF16)
        kb_o[0] = (kh * wt_tok).astype(BF16)
        bb_o[0] = (bh * wt_tok).astype(BF16)
        wt_o[0] = wt

    scan_operands(lw_f, kd_f, kk * a_f, trif_ref[...], (ktf_o, rtf_o, khf_o, bhf_o, kbf_o, bbf_o, wtf_o))
    scan_operands(lw_b, kd_b, kk * a_b, trib_ref[...], (ktb_o, rtb_o, khb_o, bhb_o, kbb_o, bbb_o, wtb_o))
    g_o[0] = _dot(gd, g2_ref[...]).astype(BF16)
    bonus_o[0] = (_segsum(r * (kd_f + kd_b) * rk_ref[...], p) * v).astype(BF16)


def _head_ones():
    h = np.arange(RW_QUAD) // RW_HEAD
    return (h[:, None] == h[None, :]).astype(BF16)


def _chunk_matrices(tb, c):
    t = np.arange(tb)
    same = (t[:, None] // c) == (t[None, :] // c)
    tri_f = (same & (t[None, :] <= t[:, None])).astype(BF16)
    tri_b = (same & (t[None, :] >= t[:, None])).astype(BF16)
    sel = (np.arange(tb // c)[:, None] == (t[None, :] // c)).astype(BF16)
    return tri_f, tri_b, sel


def _rw_prep(u_r, mu, w0, w2, a0, a2, g2, k_k, k_a, r_k, p_ones, tb=512):
    B, L, C = u_r.shape
    tb = min(tb, L)
    W = RW_WIDTH
    c = min(RW_CHUNK, L)
    ncb = tb // c
    tri_f, tri_b, sel = _chunk_matrices(tb, c)
    zeros = jnp.zeros((RW_LORA, W), F32)
    w2f = jnp.concatenate([w2[0], zeros], axis=0).astype(BF16)
    w2b = jnp.concatenate([zeros, w2[1]], axis=0).astype(BF16)
    a2f = jnp.concatenate([a2[0], zeros], axis=0).astype(BF16)
    a2b = jnp.concatenate([zeros, a2[1]], axis=0).astype(BF16)
    full = lambda shape: pl.BlockSpec(shape, lambda b, j: tuple(0 for _ in shape))
    tok = lambda: pl.BlockSpec((1, tb, W), lambda b, j: (b, j, 0))
    wts = lambda: pl.BlockSpec((1, ncb, W), lambda b, j: (b, j, 0))
    bf_tok = jax.ShapeDtypeStruct((B, L, W), BF16)
    wt_shape = jax.ShapeDtypeStruct((B, L // c, W), F32)
    g8 = tb // 8
    outs = pl.pallas_call(
        _rw_prep_kernel,
        grid=(B, L // tb),
        in_specs=[pl.BlockSpec((1, tb, C), lambda b, j: (b, j, 0)),
                  pl.BlockSpec((1, 8, C), lambda b, j: (b, jnp.maximum(j * g8 - 1, 0), 0)),
                  pl.BlockSpec((1, 8, C), lambda b, j: (b, jnp.minimum((j + 1) * g8, L // 8 - 1), 0)),
                  full((1, C)), full((2, W)), full((2 * RW_LORA, W)), full((2 * RW_LORA, W)),
                  full((2, W)), full((2 * RW_LORA, W)), full((2 * RW_LORA, W)),
                  full((RW_LORA_G, W)), full((1, W)), full((1, W)), full((1, W)), full(p_ones.shape),
                  full((tb, tb)), full((tb, tb)), full((ncb, tb))],
        out_specs=[tok()] + ([tok() for _ in range(6)] + [wts()]) * 2 + [tok(), tok()],
        out_shape=[bf_tok] + ([bf_tok] * 6 + [wt_shape]) * 2 + [bf_tok, bf_tok],
        compiler_params=_cparams(("parallel", "parallel")),
        name="rw_prep",
    )(u_r, u_r, u_r, mu.reshape(1, C), w0, w2f, w2b, a0, a2f, a2b, g2.astype(BF16),
      k_k.reshape(1, W), k_a.reshape(1, W), r_k.reshape(1, W), p_ones, tri_f, tri_b, sel)
    v, g, bonus = outs[0], outs[15], outs[16]
    fwd_ops = tuple(outs[1:7]) + (v, outs[7])
    bwd_ops = tuple(outs[8:14]) + (v, outs[14])
    return fwd_ops, bwd_ops, g, bonus, ncb


def _bmm(a, b):
    return lax.dot_general(a, b, (((2,), (1,)), ((0,), (0,))), preferred_element_type=F32)


def _bmm_nt(a, b):
    return lax.dot_general(a, b, (((2,), (2,)), ((0,), (0,))), preferred_element_type=F32)


def _bmm_tn(a, b):
    return lax.dot_general(a, b, (((1,), (1,)), ((0,), (0,))), preferred_element_type=F32)


RW_QUAD = 4 * RW_HEAD
RW_SEQS = 8


def _block_diag(x, same_head):
    reps = RW_QUAD // x.shape[1]
    return jnp.where(same_head, jnp.concatenate([x] * reps, axis=1), jnp.zeros((), x.dtype))


def _quads(x_f, x_b):
    halves = lambda x: [x[n, :, s:s + RW_QUAD] for n in range(x.shape[0]) for s in range(0, x.shape[2], RW_QUAD)]
    return jnp.stack(halves(x_f) + halves(x_b), axis=0)


def _rw_scan_kernel(nc, ncb, ktf, rtf, khf, bhf, kbf, bbf, vf, wtf, ktb, rtb, khb, bhb, kbb, bbb, vb, wtb,
                    yf_ref, yb_ref, s_ref):
    i = pl.program_id(1)

    @pl.when(i == 0)
    def _():
        s_ref[...] = jnp.zeros_like(s_ref)

    kt, rt = _quads(ktf[...], ktb[...]), _quads(rtf[...], rtb[...])
    khat, bhat = _quads(khf[...], khb[...]), _quads(bhf[...], bhb[...])
    kbar, bbar = _quads(kbf[...], kbb[...]), _quads(bbf[...], bbb[...])
    v = _quads(vf[...], vb[...])
    wt = _quads(wtf[:, pl.ds(i % ncb, 1), :], wtb[:, pl.ds((nc - 1 - i) % ncb, 1), :])
    g, c, q = kt.shape
    ri = lax.broadcasted_iota(jnp.int32, (g, c, q), 1)
    ci = lax.broadcasted_iota(jnp.int32, (g, c, q), 2) % c
    rev = lax.broadcasted_iota(jnp.int32, (g, c, q), 0) >= g // 2
    ahead = jnp.where(rev, ri - ci, ci - ri)
    strict = ahead < 0
    causal = ahead <= 0
    eye = jnp.where(ahead == 0, 1.0, 0.0)
    same_head = (lax.broadcasted_iota(jnp.int32, (1, q, q), 1) // RW_HEAD
                 == lax.broadcasted_iota(jnp.int32, (1, q, q), 2) // RW_HEAD)
    bd = lambda x: _block_diag(x, same_head)

    lh = jnp.concatenate([kt, rt], axis=1)
    gk = _bmm_nt(lh, bd(khat))
    gb = _bmm_nt(lh, bd(bhat))
    a_k = jnp.where(strict, gk[:, :c], 0.0)
    b_k = jnp.where(causal, gk[:, c:], 0.0)
    a_b = jnp.where(strict, gb[:, :c], 0.0)
    b_b = jnp.where(causal, gb[:, c:], 0.0)
    p = -a_b
    tmat = eye + p
    p = _bmm(p.astype(BF16), bd(p.astype(BF16)))
    doublings = int(math.log2(c)) - 1
    for step in range(doublings):
        p_bd = bd(p.astype(BF16))
        if step == doublings - 1:
            tmat = tmat + _bmm(tmat.astype(BF16), p_bd)
        else:
            both = _bmm(jnp.concatenate([p, tmat], axis=1).astype(BF16), p_bd)
            tmat = tmat + both[:, c:]
            p = both[:, :c]
    s0 = s_ref[...]
    ks = _bmm_nt(lh, s0.astype(BF16))
    av = _bmm(jnp.concatenate([a_k, b_k], axis=1).astype(BF16), bd(v))
    ub = _bmm(tmat.astype(BF16), bd((ks[:, :c] + av[:, :c]).astype(BF16))).astype(BF16)
    y = ks[:, c:] + av[:, c:] - _bmm(b_b.astype(BF16), bd(ub))
    upd = _bmm_tn(jnp.concatenate([v, ub], axis=1), jnp.concatenate([kbar, -bbar], axis=1))
    s_ref[...] = s0 * wt + jnp.where(same_head, upd, 0.0)
    per_seq = yf_ref.shape[2] // q
    for n in range(yf_ref.shape[0]):
        yf_ref[n] = jnp.concatenate([y[n * per_seq + j] for j in range(per_seq)], axis=1).astype(BF16)
        yb_ref[n] = jnp.concatenate([y[g // 2 + n * per_seq + j] for j in range(per_seq)], axis=1).astype(BF16)


def _rw_scan(fwd_ops, bwd_ops, ncb):
    B, L, W = fwd_ops[0].shape
    c = min(RW_CHUNK, L)
    nc = L // c
    nb = RW_SEQS if B % RW_SEQS == 0 else 1
    fwd = lambda: pl.BlockSpec((nb, c, W), lambda b, i: (b, i, 0))
    bwd = lambda: pl.BlockSpec((nb, c, W), lambda b, i: (b, nc - 1 - i, 0))
    wt_f = pl.BlockSpec((nb, ncb, W), lambda b, i: (b, i // ncb, 0))
    wt_b = pl.BlockSpec((nb, ncb, W), lambda b, i: (b, (nc - 1 - i) // ncb, 0))
    return pl.pallas_call(
        functools.partial(_rw_scan_kernel, nc, ncb),
        grid=(B // nb, nc),
        in_specs=[fwd() for _ in range(7)] + [wt_f] + [bwd() for _ in range(7)] + [wt_b],
        out_specs=[fwd(), bwd()],
        out_shape=[jax.ShapeDtypeStruct((B, L, W), BF16)] * 2,
        scratch_shapes=[pltpu.VMEM((2 * nb * W // RW_QUAD, RW_QUAD, RW_QUAD), F32)],
        compiler_params=_cparams(("parallel", "arbitrary")),
        name="rw_scan",
    )(*fwd_ops, *bwd_ops)


def _merge_kernel(x_ref, yh_ref, yf_ref, yb_ref, bonus_ref, g_ref, gates_ref, p_ref, gnw_ref, gnb_ref,
                  why_ref, wrw_ref, wo_ref, lnw_ref, lnb_ref, o_ref):
    p = p_ref[...]
    y = yf_ref[...].astype(F32) + yb_ref[...].astype(F32)
    mu = _segsum(y, p) * (1.0 / RW_HEAD)
    yc = y - mu
    var = _segsum(yc * yc, p, terms=1) * (1.0 / RW_HEAD)
    yn = yc * lax.rsqrt(var + RW_GN_EPS) * gnw_ref[...] + gnb_ref[...]
    y_r = (yn + bonus_ref[...].astype(F32)) * g_ref[...].astype(F32)
    ph = _dot(_tiles_cat(yh_ref, lambda q: (0, q), yh_ref.shape[1]).astype(BF16), why_ref[...])
    pr = _dot(y_r.astype(BF16), wrw_ref[...])
    gates = jax.nn.sigmoid(gates_ref[...].astype(F32))
    m = gates[:, :D_MODEL] * ph + gates[:, D_MODEL:] * pr
    mix = _dot(m.astype(BF16), wo_ref[...])
    o_ref[...] = _layer_norm(DN_ALPHA * x_ref[...] + mix, lnw_ref[...], lnb_ref[...])


def _merge(x2, yh, yf, yb, bonus, g, gates, p_ones, gn_w, gn_b, w_hy_out, w_rw_out, w_o, ln_w, ln_b, tm=512):
    m, d = x2.shape
    _, tiles, L, _ = yh.shape
    tm = min(tm, L)
    per_seq = L // tm
    W = RW_WIDTH
    row = lambda width: pl.BlockSpec((tm, width), lambda i: (i, 0))
    full = lambda shape: pl.BlockSpec(shape, lambda i: tuple(0 for _ in shape))
    return pl.pallas_call(
        _merge_kernel,
        grid=(m // tm,),
        in_specs=[row(d), pl.BlockSpec((1, tiles, tm, LANES), lambda i: (i // per_seq, 0, i % per_seq, 0)),
                  row(W), row(W), row(W), row(W), row(GATE_COLS),
                  full(p_ones.shape), full((1, W)), full((1, W)),
                  full((HY_WIDTH, d)), full((W, d)), full((d, d)), full((1, d)), full((1, d))],
        out_specs=row(d),
        out_shape=jax.ShapeDtypeStruct((m, d), F32),
        compiler_params=_cparams(("parallel",)),
        name="merge_ln1",
    )(x2, yh, yf, yb, bonus, g, gates, p_ones, gn_w.reshape(1, W), gn_b.reshape(1, W),
      w_hy_out.astype(BF16), w_rw_out.astype(BF16), w_o.astype(BF16), ln_w.reshape(1, d), ln_b.reshape(1, d))


def _ffn_kernel(x_ref, wg_ref, wu_ref, wd_ref, lnw_ref, lnb_ref, o_ref):
    x = x_ref[...]
    xb = x.astype(BF16)
    hidden = jax.nn.silu(_dot(xb, wg_ref[...])) * _dot(xb, wu_ref[...])
    ffn = _dot(hidden.astype(BF16), wd_ref[...])
    o_ref[...] = _layer_norm(DN_ALPHA * x + ffn, lnw_ref[...], lnb_ref[...])


def _ffn(x2, w_gate, w_up, w_down, ln_w, ln_b, tm=1024):
    m, d = x2.shape
    tm = min(tm, m)
    fh = w_gate.shape[1]
    resident = lambda shape: pl.BlockSpec(shape, lambda i: (0, 0), pipeline_mode=pl.Buffered(1))
    return pl.pallas_call(
        _ffn_kernel,
        grid=(m // tm,),
        in_specs=[pl.BlockSpec((tm, d), lambda i: (i, 0)),
                  resident((d, fh)), resident((d, fh)), resident((fh, d)),
                  resident((1, d)), resident((1, d))],
        out_specs=pl.BlockSpec((tm, d), lambda i: (i, 0)),
        out_shape=jax.ShapeDtypeStruct((m, d), F32),
        compiler_params=_cparams(("parallel",)),
        name="ffn_ln2",
    )(x2, w_gate.astype(BF16), w_up.astype(BF16), w_down.astype(BF16), ln_w.reshape(1, d), ln_b.reshape(1, d))


def _layer(x, w_in, hy_conv_w, hy_conv_b, hy_filt_w1, hy_filt_b1, hy_filt_w2, hy_filt_b2,
           hy_filt_w3, hy_filt_b3, hy_filt_w4, hy_sin_freq, hy_skip, rw_mu, rw_w0, rw_w2,
           rw_a0, rw_a2, rw_g2, rw_k_k, rw_k_a, rw_r_k, rw_gn_w, rw_gn_b, w_hy_out, w_rw_out,
           w_o, ln1_w, ln1_b, ffn_w_gate, ffn_w_up, ffn_w_down, ln2_w, ln2_b):
    B, L, D = x.shape
    x2 = x.reshape(B * L, D)
    u3, u_r, gates = _in_proj(x2, L, w_in.astype(BF16), hy_conv_w, hy_conv_b)
    y_h = _hyena(u3, hy_filt_w1, hy_filt_b1, hy_filt_w2, hy_filt_b2, hy_filt_w3, hy_filt_b3, hy_filt_w4,
                 hy_sin_freq, hy_skip)
    p_ones = _head_ones()
    fwd_ops, bwd_ops, g, bonus, ncb = _rw_prep(
        u_r.reshape(B, L, RW_COLS), rw_mu, rw_w0, rw_w2, rw_a0, rw_a2, rw_g2, rw_k_k, rw_k_a, rw_r_k, p_ones)
    yf, yb = _rw_scan(fwd_ops, bwd_ops, ncb)
    flat = lambda a: a.reshape(B * L, a.shape[-1])
    h = _merge(x2, y_h, flat(yf), flat(yb), flat(bonus), flat(g), gates, p_ones, rw_gn_w, rw_gn_b,
               w_hy_out, w_rw_out, w_o, ln1_w, ln1_b)
    out = _ffn(h, ffn_w_gate, ffn_w_up, ffn_w_down, ln2_w, ln2_b)
    return out.reshape(B, L, D)


def kernel(x, w_in, hy_conv_w, hy_conv_b, hy_filt_w1, hy_filt_b1, hy_filt_w2, hy_filt_b2, hy_filt_w3, hy_filt_b3, hy_filt_w4, hy_sin_freq, hy_skip, rw_mu, rw_w0, rw_w2, rw_a0, rw_a2, rw_g2, rw_k_k, rw_k_a, rw_r_k, rw_gn_w, rw_gn_b, w_hy_out, w_rw_out, w_o, ln1_w, ln1_b, ffn_w_gate, ffn_w_up, ffn_w_down, ln2_w, ln2_b):
    params = (w_in, hy_conv_w, hy_conv_b, hy_filt_w1, hy_filt_b1, hy_filt_w2, hy_filt_b2, hy_filt_w3,
              hy_filt_b3, hy_filt_w4, hy_sin_freq, hy_skip, rw_mu, rw_w0, rw_w2, rw_a0, rw_a2, rw_g2,
              rw_k_k, rw_k_a, rw_r_k, rw_gn_w, rw_gn_b, w_hy_out, w_rw_out, w_o, ln1_w, ln1_b,
              ffn_w_gate, ffn_w_up, ffn_w_down, ln2_w, ln2_b)
    for l in range(w_in.shape[0]):
        x = _layer(x, *[p[l] for p in params])
    return x
```

```python
import functools
import math

import numpy as np
import jax
import jax.numpy as jnp
from jax import lax
from jax.experimental import pallas as pl
from jax.experimental.pallas import tpu as pltpu

F32 = jnp.float32
BF16 = jnp.bfloat16
HIGHEST = lax.Precision.HIGHEST

D_MODEL = 1024
HY_WIDTH = 512
HY_ORDER = 2
HY_BANDS = 16
HY_FILT_HIDDEN = 64
HY_FAST_DECAY = 0.3
HY_SLOW_DECAY = 1.5
HY_DECAY_TARGET = 1e-2
HY_MAX_DECAY = math.log(HY_DECAY_TARGET) / HY_FAST_DECAY
HY_MIN_DECAY = math.log(HY_DECAY_TARGET) / HY_SLOW_DECAY
HY_COLS = 3 * HY_WIDTH
RW_WIDTH = 512
RW_HEAD = 64
RW_HEADS = RW_WIDTH // RW_HEAD
RW_LORA = 64
RW_LORA_G = 128
RW_GN_EPS = 64e-5
RW_COLS = 3 * RW_WIDTH + 4 * RW_LORA + RW_LORA_G
GATE_COLS = 2 * D_MODEL
FFN_HIDDEN = ((8 * D_MODEL + 3 * 256 - 1) // (3 * 256)) * 256
DEPTH = 1
DN_ALPHA = (2.0 * DEPTH) ** 0.25
LN_EPS = 1e-5
RW_DECAY_SCALE = math.exp(-0.5)

LANES = 128
VMEM_LIMIT = 56 * 1024 * 1024

FFT_N1 = 32
FFT_F1 = FFT_N1 // 2 + 1
RW_CHUNK = 64


def _cparams(sem, vmem=VMEM_LIMIT):
    return pltpu.CompilerParams(dimension_semantics=sem, vmem_limit_bytes=vmem)


def _dot(a, b, precision=None):
    return jnp.dot(a, b, preferred_element_type=F32, precision=precision)


def _dot_nt(a, b):
    return lax.dot_general(a, b, (((1,), (1,)), ((), ())), preferred_element_type=F32)


def _dot_tn(a, b):
    return lax.dot_general(a, b, (((0,), (0,)), ((), ())), preferred_element_type=F32)


def _layer_norm(h, w, b):
    mu = jnp.mean(h, axis=-1, keepdims=True)
    c = h - mu
    var = jnp.mean(c * c, axis=-1, keepdims=True)
    return c * lax.rsqrt(var + LN_EPS) * w + b


def _hi_lo(x):
    hi = x.astype(BF16)
    return hi, (x - hi.astype(F32)).astype(BF16)


def _segsum(x, p, terms=2):
    q = p.shape[0]
    parts = _hi_lo(x)[:terms]
    return jnp.concatenate([_dot_parts_t([t[:, s:s + q] for t in parts], p) for s in range(0, x.shape[1], q)], axis=1)


def _dot_hl(a, b):
    a_hi, a_lo = _hi_lo(a)
    b_hi, b_lo = _hi_lo(b)
    return _dot(a_hi, b_hi) + _dot(a_lo, b_hi) + _dot(a_hi, b_lo)


def _dot_parts_t(parts, m):
    out = _dot(parts[0], m)
    for part in parts[1:]:
        out = out + _dot(part, m)
    return out


def _dot_parts(m, parts):
    out = _dot(m, parts[0])
    for part in parts[1:]:
        out = out + _dot(m, part)
    return out


HY_TILES = HY_WIDTH // LANES
HALO = 8


def _in_proj_kernel(per_seq, x_ref, xp_ref, xn_ref, w_ref, cw_ref, cb_ref, u3_ref, ur_ref, g_ref):
    pos = pl.program_id(0) % per_seq
    w_h = w_ref[:, :HY_COLS]
    xb = x_ref[...].astype(BF16)
    tm = xb.shape[0]
    ext = _dot(jnp.concatenate([xp_ref[...].astype(BF16), xb, xn_ref[...].astype(BF16)], axis=0), w_h)
    u = ext[HALO:HALO + tm]
    row = lax.broadcasted_iota(jnp.int32, (tm, 1), 0)
    prev = jnp.where((row == 0) & (pos == 0), 0.0, ext[HALO - 1:HALO - 1 + tm])
    nxt = jnp.where((row == tm - 1) & (pos == per_seq - 1), 0.0, ext[HALO + 1:HALO + 1 + tm])
    conv = cw_ref[0:1, :] * prev + cw_ref[1:2, :] * u + cw_ref[2:3, :] * nxt + cb_ref[...]
    for s in range(HY_COLS // HY_WIDTH):
        for q in range(HY_TILES):
            lo = s * HY_WIDTH + q * LANES
            u3_ref[s, 0, q] = conv[:, lo:lo + LANES].astype(BF16)
    ur_ref[...] = _dot(xb, w_ref[:, HY_COLS:HY_COLS + RW_COLS])
    g_ref[...] = _dot(xb, w_ref[:, HY_COLS + RW_COLS:]).astype(BF16)


def _in_proj(x2, L, w_in_bf, conv_w, conv_b, tm=512):
    m, d = x2.shape
    n = w_in_bf.shape[1]
    tm = min(tm, L)
    per_seq = L // tm
    groups = tm // HALO
    resident = lambda shape: pl.BlockSpec(shape, lambda i: (0, 0), pipeline_mode=pl.Buffered(1))
    slabs = HY_COLS // HY_WIDTH
    return pl.pallas_call(
        functools.partial(_in_proj_kernel, per_seq),
        grid=(m // tm,),
        in_specs=[pl.BlockSpec((tm, d), lambda i: (i, 0)),
                  pl.BlockSpec((HALO, d), lambda i: (jnp.maximum(i * groups - 1, 0), 0)),
                  pl.BlockSpec((HALO, d), lambda i: (jnp.minimum((i + 1) * groups, m // HALO - 1), 0)),
                  resident((d, n)), resident((3, HY_COLS)), resident((1, HY_COLS))],
        out_specs=[pl.BlockSpec((slabs, 1, HY_TILES, tm, LANES), lambda i: (0, i // per_seq, 0, i % per_seq, 0)),
                   pl.BlockSpec((tm, RW_COLS), lambda i: (i, 0)),
                   pl.BlockSpec((tm, GATE_COLS), lambda i: (i, 0))],
        out_shape=[jax.ShapeDtypeStruct((slabs, m // L, HY_TILES, L, LANES), BF16),
                   jax.ShapeDtypeStruct((m, RW_COLS), F32),
                   jax.ShapeDtypeStruct((m, GATE_COLS), BF16)],
        compiler_params=_cparams(("parallel",)),
        name="in_proj",
    )(x2, x2, x2, w_in_bf, conv_w, conv_b.reshape(1, HY_COLS))


def _filter_kernel(L, tb, w1t_ref, w1c_ref, w1s_ref, b1_ref, w2_ref, b2_ref, w3_ref, b3_ref,
                   w4_ref, sf_ref, freq_ref, delta_ref, kc_ref):
    n = 2 * L
    i = pl.program_id(0)
    row = i * tb + lax.broadcasted_iota(jnp.int32, (tb, 1), 0)
    second = row >= L
    pos = jnp.where(second, n - row, row).astype(F32)
    t = pos / float(L - 1)
    ang = (2.0 * math.pi * pos / float(L)) * freq_ref[...]
    pre = t * w1t_ref[...] + _dot(jnp.cos(ang), w1c_ref[...], HIGHEST) \
        - _dot(jnp.sin(ang), w1s_ref[...], HIGHEST) + b1_ref[...]
    h = jnp.sin(sf_ref[0:1, :] * pre)
    h = jnp.sin(sf_ref[1:2, :] * (_dot(h, w2_ref[...], HIGHEST) + b2_ref[...]))
    h = jnp.sin(sf_ref[2:3, :] * (_dot(h, w3_ref[...], HIGHEST) + b3_ref[...]))
    h4 = _dot_hl(h, w4_ref[...])
    window = jnp.exp(-t * delta_ref[...])
    keep = jnp.where(row == L, 0.0, 1.0)
    first = jnp.where(row == 0, 1.0, 0.0)
    for o in range(HY_ORDER):
        base = o * 2 * HY_WIDTH
        fwd = h4[:, base:base + HY_WIDTH]
        bwd = h4[:, base + HY_WIDTH:base + 2 * HY_WIDTH]
        kc = window * (jnp.where(second, bwd, fwd) * keep + first * bwd)
        for q in range(HY_TILES):
            kc_ref[o, q] = kc[:, q * LANES:(q + 1) * LANES]


def _filters(L, fw1, fb1, fw2, fb2, fw3, fb3, fw4, sin_freq, tb=512):
    n = 2 * L
    tb = min(tb, n)
    freqs = jnp.linspace(1e-4, HY_BANDS - 1, HY_BANDS, dtype=F32).reshape(1, HY_BANDS)
    deltas = jnp.abs(jnp.linspace(HY_MIN_DECAY, HY_MAX_DECAY, HY_WIDTH, dtype=F32)).reshape(1, HY_WIDTH)
    hid = HY_FILT_HIDDEN
    full = lambda shape: pl.BlockSpec(shape, lambda i: tuple(0 for _ in shape))
    return pl.pallas_call(
        functools.partial(_filter_kernel, L, tb),
        grid=(n // tb,),
        in_specs=[full((1, hid)), full((HY_BANDS, hid)), full((HY_BANDS, hid)), full((1, hid)),
                  full((hid, hid)), full((1, hid)), full((hid, hid)), full((1, hid)),
                  full((hid, HY_ORDER * 2 * HY_WIDTH)), full((3, hid)),
                  full((1, HY_BANDS)), full((1, HY_WIDTH))],
        out_specs=pl.BlockSpec((HY_ORDER, HY_TILES, tb, LANES), lambda i: (0, 0, i, 0)),
        out_shape=jax.ShapeDtypeStruct((HY_ORDER, HY_TILES, n, LANES), F32),
        compiler_params=_cparams(("parallel",)),
        name="hy_filters",
    )(fw1[0:1], fw1[1:1 + HY_BANDS], fw1[1 + HY_BANDS:], fb1.reshape(1, hid), fw2, fb2.reshape(1, hid),
      fw3, fb3.reshape(1, hid), fw4, sin_freq, freqs, deltas)


def _fft_tables(n2):
    n = FFT_N1 * n2
    f1 = np.arange(FFT_F1)[:, None, None]
    a = np.arange(n2)[None, :, None]
    b = np.arange(n2)[None, None, :]
    ph = ((b * (f1 + FFT_N1 * a)) % n) * (2.0 * math.pi / n)
    cr, ci = np.cos(ph), -np.sin(ph)
    m_fwd = np.concatenate([np.concatenate([cr, -ci], axis=2), np.concatenate([ci, cr], axis=2)], axis=1)
    herm = np.where((f1 == 0) | (f1 == FFT_N1 // 2), 1.0, 2.0) / n
    dr, di = herm * cr.transpose(0, 2, 1), -herm * ci.transpose(0, 2, 1)
    m_inv = np.concatenate([np.concatenate([dr, -di], axis=2), np.concatenate([di, dr], axis=2)], axis=1)
    return jnp.asarray(m_fwd, F32).astype(BF16), jnp.asarray(m_inv, F32).astype(BF16)


def _slow_tables():
    k = (np.arange(FFT_F1)[:, None] * np.arange(FFT_N1)[None, :]) % FFT_N1
    ph = k * (2.0 * math.pi / FFT_N1)
    fwd = np.stack([np.cos(ph), -np.sin(ph)], axis=1).reshape(2 * FFT_F1, FFT_N1)
    inv = fwd[:, :FFT_N1 // 2].T
    return fwd, inv


SLOW_UNROLL = 4
ROW_GROUP = 16
SLOW_TILES = 2


def _kron_hi_lo(m):
    return jnp.concatenate(_hi_lo(jnp.asarray(np.kron(m, np.eye(ROW_GROUP)), F32)), axis=0)


def _slow_dft_kernel(n2, s1_count, f_ref, z_ref, a_ref):
    f_hl = f_ref[...]
    half = f_hl.shape[0] // 2
    st = z_ref.shape[2]

    def body(g, carry):
        s2 = pl.multiple_of(g * ROW_GROUP, ROW_GROUP)
        zs = jnp.concatenate(
            [_tiles_cat(z_ref, lambda q: (0, 0, q, pl.ds(s1 * n2 + s2, ROW_GROUP)), st) for s1 in range(s1_count)],
            axis=0).astype(BF16)
        a = _dot(f_hl, zs)
        a = (a[:half] + a[half:]).astype(BF16)
        for f in range(half // ROW_GROUP):
            _tiles_put(a_ref, lambda q: (0, q, pl.ds(f * n2 + s2, ROW_GROUP)), a[f * ROW_GROUP:(f + 1) * ROW_GROUP])
        return carry

    lax.fori_loop(0, n2 // ROW_GROUP, body, 0, unroll=min(SLOW_UNROLL, n2 // ROW_GROUP))


def _tiles_cat(ref, index, ntiles):
    return jnp.concatenate([ref[index(q) + (slice(None),)] for q in range(ntiles)], axis=1)


def _tiles_put(ref, index, value):
    for q in range(value.shape[1] // LANES):
        ref[index(q) + (slice(None),)] = value[:, q * LANES:(q + 1) * LANES]


def _slow_dft(z5, sel, s1_count):
    _, B, tiles, T, _ = z5.shape
    n2 = T // s1_count
    fwd, _ = _slow_tables()
    rows = 2 * FFT_F1
    f_hl = _kron_hi_lo(fwd[:, :s1_count])
    st = SLOW_TILES
    return pl.pallas_call(
        functools.partial(_slow_dft_kernel, n2, s1_count),
        grid=(B, tiles // st),
        in_specs=[pl.BlockSpec(f_hl.shape, lambda b, j: (0, 0)),
                  pl.BlockSpec((1, 1, st, T, LANES), lambda b, j: (sel, b, j, 0, 0))],
        out_specs=pl.BlockSpec((1, st, rows * n2, LANES), lambda b, j: (b, j, 0, 0)),
        out_shape=jax.ShapeDtypeStruct((B, tiles, rows * n2, LANES), BF16),
        compiler_params=_cparams(("parallel", "parallel")),
        name="hy_slow_dft",
    )(f_hl, z5)


def _spectrum_kernel(n2, a_ref, mf_ref, hr_ref, hi_ref):
    x = _dot(mf_ref[0], _tiles_cat(a_ref, lambda q: (0, q), a_ref.shape[1]))
    hr_ref[0] = x[:n2]
    hi_ref[0] = x[n2:]


def _spectrum(kc, m_fwd):
    order, tiles, n, _ = kc.shape
    n2 = n // FFT_N1
    C = tiles * LANES
    a = _slow_dft(kc.reshape(1, order, tiles, n, LANES), 0, FFT_N1)
    spec = lambda: pl.BlockSpec((1, n2, C), lambda f, o: (o, f, 0))
    return pl.pallas_call(
        functools.partial(_spectrum_kernel, n2),
        grid=(FFT_F1, order),
        in_specs=[pl.BlockSpec((1, tiles, 2 * n2, LANES), lambda f, o: (o, 0, f, 0)),
                  pl.BlockSpec((1, 2 * n2, 2 * n2), lambda f, o: (f, 0, 0))],
        out_specs=[spec(), spec()],
        out_shape=[jax.ShapeDtypeStruct((order, FFT_F1 * n2, C), F32)] * 2,
        compiler_params=_cparams(("parallel", "parallel")),
        name="hy_spectrum",
    )(a, m_fwd)


MID_SEQS = 4


def _conv_mid_kernel(n2, a_ref, hr_ref, hi_ref, mf_ref, mi_ref, b_ref):
    hr, hi = hr_ref[0], hi_ref[0]
    for n in range(a_ref.shape[0]):
        x = _dot(mf_ref[0], _tiles_cat(a_ref, lambda q: (n, q), a_ref.shape[1]))
        xr, xi = x[:n2], x[n2:]
        y = jnp.concatenate([xr * hr - xi * hi, xr * hi + xi * hr], axis=0).astype(BF16)
        _tiles_put(b_ref, lambda q: (n, q), _dot(mi_ref[0], y).astype(BF16))


def _conv_mid(a, order, hr, hi, m_fwd, m_inv):
    B, tiles, total, _ = a.shape
    rows = total // FFT_F1
    n2 = rows // 2
    W = tiles * LANES
    nb = MID_SEQS if B % MID_SEQS == 0 else 1
    blk = lambda: pl.BlockSpec((nb, tiles, rows, LANES), lambda f, b: (b, 0, f, 0))
    return pl.pallas_call(
        functools.partial(_conv_mid_kernel, n2),
        grid=(FFT_F1, B // nb),
        in_specs=[blk(),
                  pl.BlockSpec((1, n2, W), lambda f, b: (order, f, 0)),
                  pl.BlockSpec((1, n2, W), lambda f, b: (order, f, 0)),
                  pl.BlockSpec((1, rows, rows), lambda f, b: (f, 0, 0)),
                  pl.BlockSpec((1, rows, rows), lambda f, b: (f, 0, 0))],
        out_specs=blk(),
        out_shape=jax.ShapeDtypeStruct((B, tiles, total, LANES), BF16),
        compiler_params=_cparams(("parallel", "parallel")),
        name=f"hy_conv_mid{order}",
    )(a, hr, hi, m_fwd, m_inv)


IDFT_ROWS = tuple(f for f in range(2 * FFT_F1) if f not in (1, 2 * FFT_F1 - 1))


def _slow_idft_kernel(n2, g_ref, b_ref, z_ref, gate_ref, skip_ref, o_ref):
    g_hl = g_ref[...]
    half = g_hl.shape[0] // 2
    skip = skip_ref[...]
    st = b_ref.shape[1]

    def body(g, carry):
        t2 = pl.multiple_of(g * ROW_GROUP, ROW_GROUP)
        bs = jnp.concatenate(
            [_tiles_cat(b_ref, lambda q: (0, q, pl.ds(f * n2 + t2, ROW_GROUP)), st) for f in IDFT_ROWS], axis=0)
        y2 = _dot(g_hl, bs)
        y = y2[:half] + y2[half:]
        for t1 in range(half // ROW_GROUP):
            tok = lambda q: (0, 0, q, pl.ds(t1 * n2 + t2, ROW_GROUP))
            z = _tiles_cat(z_ref, tok, st)
            gate = _tiles_cat(gate_ref, tok, st)
            out = gate * (y[t1 * ROW_GROUP:(t1 + 1) * ROW_GROUP] + skip * z)
            _tiles_put(o_ref, tok, out.astype(o_ref.dtype))
        return carry

    lax.fori_loop(0, n2 // ROW_GROUP, body, 0, unroll=min(SLOW_UNROLL, n2 // ROW_GROUP))


def _slow_idft(bm, z5, zsel, g5, gsel, skip_row, out_dtype):
    B, tiles, total, _ = bm.shape
    t1 = FFT_N1 // 2
    n2 = total // (2 * FFT_F1)
    T = t1 * n2
    _, inv = _slow_tables()
    g_hl = _kron_hi_lo(inv[:, list(IDFT_ROWS)])
    st = SLOW_TILES
    tok = lambda sel: pl.BlockSpec((1, 1, st, T, LANES), lambda b, j: (sel, b, j, 0, 0))
    return pl.pallas_call(
        functools.partial(_slow_idft_kernel, n2),
        grid=(B, tiles // st),
        in_specs=[pl.BlockSpec(g_hl.shape, lambda b, j: (0, 0)),
                  pl.BlockSpec((1, st, total, LANES), lambda b, j: (b, j, 0, 0)),
                  tok(zsel), tok(gsel),
                  pl.BlockSpec((1, st * LANES), lambda b, j: (0, j))],
        out_specs=tok(0),
        out_shape=jax.ShapeDtypeStruct((1, B, tiles, T, LANES), out_dtype),
        compiler_params=_cparams(("parallel", "parallel")),
        name="hy_slow_idft",
    )(g_hl, bm, z5, g5, skip_row.reshape(1, tiles * LANES))


def _longconv(z5, zsel, g5, gsel, order, skip, hr, hi, m_fwd, m_inv, out_dtype):
    a = _slow_dft(z5, zsel, FFT_N1 // 2)
    bm = _conv_mid(a, order, hr, hi, m_fwd, m_inv)
    return _slow_idft(bm, z5, zsel, g5, gsel, skip[order], out_dtype)


def _hyena(u3, fw1, fb1, fw2, fb2, fw3, fb3, fw4, sin_freq, skip):
    L = u3.shape[3]
    kc = _filters(L, fw1, fb1, fw2, fb2, fw3, fb3, fw4, sin_freq)
    m_fwd, m_inv = _fft_tables(2 * L // FFT_N1)
    hr, hi = _spectrum(kc, m_fwd)
    z1 = _longconv(u3, 0, u3, 1, 0, skip, hr, hi, m_fwd, m_inv, BF16)
    return _longconv(z1, 0, u3, 2, 1, skip, hr, hi, m_fwd, m_inv, BF16)[0]


def _rw_prep_kernel(u_ref, up_ref, un_ref, mu_ref, w0_ref, w2f_ref, w2b_ref, a0_ref, a2f_ref, a2b_ref,
                    g2_ref, kk_ref, ka_ref, rk_ref, p_ref, trif_ref, trib_ref, sel_ref,
                    v_o, ktf_o, rtf_o, khf_o, bhf_o, kbf_o, bbf_o, wtf_o,
                    ktb_o, rtb_o, khb_o, bhb_o, kbb_o, bbb_o, wtb_o, g_o, bonus_o):
    j = pl.program_id(1)
    nj = pl.num_programs(1)
    u = u_ref[0]
    tb = u.shape[0]
    prow = jnp.where(j == 0, 0.0, up_ref[0, 7:8, :])
    nrow = jnp.where(j == nj - 1, 0.0, un_ref[0, 0:1, :])
    row = lax.broadcasted_iota(jnp.int32, (tb, 1), 0)
    prev = jnp.where(row == 0, prow, pltpu.roll(u, 1, 0))
    nxt = jnp.where(row == tb - 1, nrow, pltpu.roll(u, tb - 1, 0))
    mu = mu_ref[...]
    xs = (1.0 - mu) * u + (0.5 * mu) * (prev + nxt)
    W = RW_WIDTH
    r, k, v = xs[:, 0:W], xs[:, W:2 * W], xs[:, 2 * W:3 * W]
    wd = jnp.tanh(xs[:, 3 * W:3 * W + 2 * RW_LORA]).astype(BF16)
    ad = xs[:, 3 * W + 2 * RW_LORA:3 * W + 4 * RW_LORA].astype(BF16)
    gd = jax.nn.sigmoid(xs[:, 3 * W + 4 * RW_LORA:]).astype(BF16)
    p = p_ref[...]
    kkn = k * kk_ref[...]
    kk = kkn * lax.rsqrt(jnp.maximum(_segsum(kkn * kkn, p, terms=1), 1e-24))
    ka = ka_ref[...]
    lw_f = -RW_DECAY_SCALE * jax.nn.sigmoid(w0_ref[0:1, :] + _dot(wd, w2f_ref[...]))
    lw_b = -RW_DECAY_SCALE * jax.nn.sigmoid(w0_ref[1:2, :] + _dot(wd, w2b_ref[...]))
    a_f = jax.nn.sigmoid(a0_ref[0:1, :] + _dot(ad, a2f_ref[...]))
    a_b = jax.nn.sigmoid(a0_ref[1:2, :] + _dot(ad, a2b_ref[...]))
    kd_f = k * (1.0 + (a_f - 1.0) * ka)
    kd_b = k * (1.0 + (a_b - 1.0) * ka)
    v_o[0] = v.astype(BF16)
    sel = sel_ref[...]
    ncb = sel.shape[0]
    c = tb // ncb

    def scan_operands(lw, kd, b, tri, outs):
        parts = _hi_lo(lw)
        cum = _dot_parts(tri, parts)
        wt = jnp.exp(_dot_parts(sel, parts))
        wt_tok = jnp.concatenate([jnp.broadcast_to(wt[j:j + 1], (c, wt.shape[1])) for j in range(ncb)], axis=0)
        e_neg = jnp.exp(-cum)
        kh = kd * e_neg
        bh = b * e_neg
        kt_o, rt_o, kh_o, bh_o, kb_o, bb_o, wt_o = outs
        kt_o[0] = (kk * jnp.exp(cum - lw)).astype(BF16)
        rt_o[0] = (r * jnp.exp(cum)).astype(BF16)
        kh_o[0] = kh.astype(BF16)
        bh_o[0] = bh.astype(BF16)
        kb_o[0] = (kh * wt_tok).astype(BF16)
        bb_o[0] = (bh * wt_tok).astype(BF16)
        wt_o[0] = wt

    scan_operands(lw_f, kd_f, kk * a_f, trif_ref[...], (ktf_o, rtf_o, khf_o, bhf_o, kbf_o, bbf_o, wtf_o))
    scan_operands(lw_b, kd_b, kk * a_b, trib_ref[...], (ktb_o, rtb_o, khb_o, bhb_o, kbb_o, bbb_o, wtb_o))
    g_o[0] = _dot(gd, g2_ref[...]).astype(BF16)
    bonus_o[0] = (_segsum(r * (kd_f + kd_b) * rk_ref[...], p) * v).astype(BF16)


def _head_ones():
    h = np.arange(RW_QUAD) // RW_HEAD
    return (h[:, None] == h[None, :]).astype(BF16)


def _chunk_matrices(tb, c):
    t = np.arange(tb)
    same = (t[:, None] // c) == (t[None, :] // c)
    tri_f = (same & (t[None, :] <= t[:, None])).astype(BF16)
    tri_b = (same & (t[None, :] >= t[:, None])).astype(BF16)
    sel = (np.arange(tb // c)[:, None] == (t[None, :] // c)).astype(BF16)
    return tri_f, tri_b, sel


def _rw_prep(u_r, mu, w0, w2, a0, a2, g2, k_k, k_a, r_k, p_ones, tb=512):
    B, L, C = u_r.shape
    tb = min(tb, L)
    W = RW_WIDTH
    c = min(RW_CHUNK, L)
    ncb = tb // c
    tri_f, tri_b, sel = _chunk_matrices(tb, c)
    zeros = jnp.zeros((RW_LORA, W), F32)
    w2f = jnp.concatenate([w2[0], zeros], axis=0).astype(BF16)
    w2b = jnp.concatenate([zeros, w2[1]], axis=0).astype(BF16)
    a2f = jnp.concatenate([a2[0], zeros], axis=0).astype(BF16)
    a2b = jnp.concatenate([zeros, a2[1]], axis=0).astype(BF16)
    full = lambda shape: pl.BlockSpec(shape, lambda b, j: tuple(0 for _ in shape))
    tok = lambda: pl.BlockSpec((1, tb, W), lambda b, j: (b, j, 0))
    wts = lambda: pl.BlockSpec((1, ncb, W), lambda b, j: (b, j, 0))
    bf_tok = jax.ShapeDtypeStruct((B, L, W), BF16)
    wt_shape = jax.ShapeDtypeStruct((B, L // c, W), F32)
    g8 = tb // 8
    outs = pl.pallas_call(
        _rw_prep_kernel,
        grid=(B, L // tb),
        in_specs=[pl.BlockSpec((1, tb, C), lambda b, j: (b, j, 0)),
                  pl.BlockSpec((1, 8, C), lambda b, j: (b, jnp.maximum(j * g8 - 1, 0), 0)),
                  pl.BlockSpec((1, 8, C), lambda b, j: (b, jnp.minimum((j + 1) * g8, L // 8 - 1), 0)),
                  full((1, C)), full((2, W)), full((2 * RW_LORA, W)), full((2 * RW_LORA, W)),
                  full((2, W)), full((2 * RW_LORA, W)), full((2 * RW_LORA, W)),
                  full((RW_LORA_G, W)), full((1, W)), full((1, W)), full((1, W)), full(p_ones.shape),
                  full((tb, tb)), full((tb, tb)), full((ncb, tb))],
        out_specs=[tok()] + ([tok() for _ in range(6)] + [wts()]) * 2 + [tok(), tok()],
        out_shape=[bf_tok] + ([bf_tok] * 6 + [wt_shape]) * 2 + [bf_tok, bf_tok],
        compiler_params=_cparams(("parallel", "parallel")),
        name="rw_prep",
    )(u_r, u_r, u_r, mu.reshape(1, C), w0, w2f, w2b, a0, a2f, a2b, g2.astype(BF16),
      k_k.reshape(1, W), k_a.reshape(1, W), r_k.reshape(1, W), p_ones, tri_f, tri_b, sel)
    v, g, bonus = outs[0], outs[15], outs[16]
    fwd_ops = tuple(outs[1:7]) + (v, outs[7])
    bwd_ops = tuple(outs[8:14]) + (v, outs[14])
    return fwd_ops, bwd_ops, g, bonus, ncb


def _bmm(a, b):
    return lax.dot_general(a, b, (((2,), (1,)), ((0,), (0,))), preferred_element_type=F32)


def _bmm_nt(a, b):
    return lax.dot_general(a, b, (((2,), (2,)), ((0,), (0,))), preferred_element_type=F32)


def _bmm_tn(a, b):
    return lax.dot_general(a, b, (((1,), (1,)), ((0,), (0,))), preferred_element_type=F32)


RW_QUAD = 4 * RW_HEAD
RW_SEQS = 8


def _block_diag(x, same_head):
    reps = RW_QUAD // x.shape[1]
    return jnp.where(same_head, jnp.concatenate([x] * reps, axis=1), jnp.zeros((), x.dtype))


def _quads(x_f, x_b):
    halves = lambda x: [x[n, :, s:s + RW_QUAD] for n in range(x.shape[0]) for s in range(0, x.shape[2], RW_QUAD)]
    return jnp.stack(halves(x_f) + halves(x_b), axis=0)


def _rw_scan_kernel(nc, ncb, ktf, rtf, khf, bhf, kbf, bbf, vf, wtf, ktb, rtb, khb, bhb, kbb, bbb, vb, wtb,
                    yf_ref, yb_ref, s_ref):
    i = pl.program_id(1)

    @pl.when(i == 0)
    def _():
        s_ref[...] = jnp.zeros_like(s_ref)

    kt, rt = _quads(ktf[...], ktb[...]), _quads(rtf[...], rtb[...])
    khat, bhat = _quads(khf[...], khb[...]), _quads(bhf[...], bhb[...])
    kbar, bbar = _quads(kbf[...], kbb[...]), _quads(bbf[...], bbb[...])
    v = _quads(vf[...], vb[...])
    wt = _quads(wtf[:, pl.ds(i % ncb, 1), :], wtb[:, pl.ds((nc - 1 - i) % ncb, 1), :])
    g, c, q = kt.shape
    ri = lax.broadcasted_iota(jnp.int32, (1, c, q), 1)
    ci = lax.broadcasted_iota(jnp.int32, (1, c, q), 2) % c
    eye = jnp.where(ci == ri, 1.0, 0.0)
    same_head = (lax.broadcasted_iota(jnp.int32, (1, q, q), 1) // RW_HEAD
                 == lax.broadcasted_iota(jnp.int32, (1, q, q), 2) // RW_HEAD)
    bd = lambda x: _block_diag(x, same_head)

    def keep_earlier(x, strict):
        fwd = (ci < ri) if strict else (ci <= ri)
        bwd = (ci > ri) if strict else (ci >= ri)
        return jnp.concatenate([jnp.where(fwd, x[:g // 2], 0.0), jnp.where(bwd, x[g // 2:], 0.0)], axis=0)

    lh = jnp.concatenate([kt, rt], axis=1)
    gk = _bmm_nt(lh, bd(khat))
    gb = _bmm_nt(lh, bd(bhat))
    a_k = keep_earlier(gk[:, :c], True)
    b_k = keep_earlier(gk[:, c:], False)
    a_b = keep_earlier(gb[:, :c], True)
    b_b = keep_earlier(gb[:, c:], False)
    p = -a_b
    tmat = eye + p
    p = _bmm(p.astype(BF16), bd(p.astype(BF16)))
    doublings = int(math.log2(c)) - 1
    for step in range(doublings):
        p_bd = bd(p.astype(BF16))
        if step == doublings - 1:
            tmat = tmat + _bmm(tmat.astype(BF16), p_bd)
        else:
            both = _bmm(jnp.concatenate([p, tmat], axis=1).astype(BF16), p_bd)
            tmat = tmat + both[:, c:]
            p = both[:, :c]
    s0 = s_ref[...]
    ks = _bmm_nt(lh, s0.astype(BF16))
    av = _bmm(jnp.concatenate([a_k, b_k], axis=1).astype(BF16), bd(v))
    ub = _bmm(tmat.astype(BF16), bd((ks[:, :c] + av[:, :c]).astype(BF16))).astype(BF16)
    y = ks[:, c:] + av[:, c:] - _bmm(b_b.astype(BF16), bd(ub))
    upd = _bmm_tn(jnp.concatenate([v, ub], axis=1), jnp.concatenate([kbar, -bbar], axis=1))
    s_ref[...] = s0 * wt + jnp.where(same_head, upd, 0.0)
    per_seq = yf_ref.shape[2] // q
    for n in range(yf_ref.shape[0]):
        yf_ref[n] = jnp.concatenate([y[n * per_seq + j] for j in range(per_seq)], axis=1).astype(BF16)
        yb_ref[n] = jnp.concatenate([y[g // 2 + n * per_seq + j] for j in range(per_seq)], axis=1).astype(BF16)


def _rw_scan(fwd_ops, bwd_ops, ncb):
    B, L, W = fwd_ops[0].shape
    c = min(RW_CHUNK, L)
    nc = L // c
    nb = RW_SEQS if B % RW_SEQS == 0 else 1
    fwd = lambda: pl.BlockSpec((nb, c, W), lambda b, i: (b, i, 0))
    bwd = lambda: pl.BlockSpec((nb, c, W), lambda b, i: (b, nc - 1 - i, 0))
    wt_f = pl.BlockSpec((nb, ncb, W), lambda b, i: (b, i // ncb, 0))
    wt_b = pl.BlockSpec((nb, ncb, W), lambda b, i: (b, (nc - 1 - i) // ncb, 0))
    return pl.pallas_call(
        functools.partial(_rw_scan_kernel, nc, ncb),
        grid=(B // nb, nc),
        in_specs=[fwd() for _ in range(7)] + [wt_f] + [bwd() for _ in range(7)] + [wt_b],
        out_specs=[fwd(), bwd()],
        out_shape=[jax.ShapeDtypeStruct((B, L, W), BF16)] * 2,
        scratch_shapes=[pltpu.VMEM((2 * nb * W // RW_QUAD, RW_QUAD, RW_QUAD), F32)],
        compiler_params=_cparams(("parallel", "arbitrary")),
        name="rw_scan",
    )(*fwd_ops, *bwd_ops)


def _merge_kernel(x_ref, yh_ref, yf_ref, yb_ref, bonus_ref, g_ref, gates_ref, p_ref, gnw_ref, gnb_ref,
                  why_ref, wrw_ref, wo_ref, lnw_ref, lnb_ref, o_ref):
    p = p_ref[...]
    y = yf_ref[...].astype(F32) + yb_ref[...].astype(F32)
    mu = _segsum(y, p) * (1.0 / RW_HEAD)
    yc = y - mu
    var = _segsum(yc * yc, p, terms=1) * (1.0 / RW_HEAD)
    yn = yc * lax.rsqrt(var + RW_GN_EPS) * gnw_ref[...] + gnb_ref[...]
    y_r = (yn + bonus_ref[...].astype(F32)) * g_ref[...].astype(F32)
    ph = _dot(_tiles_cat(yh_ref, lambda q: (0, q), yh_ref.shape[1]).astype(BF16), why_ref[...])
    pr = _dot(y_r.astype(BF16), wrw_ref[...])
    gates = jax.nn.sigmoid(gates_ref[...].astype(F32))
    m = gates[:, :D_MODEL] * ph + gates[:, D_MODEL:] * pr
    mix = _dot(m.astype(BF16), wo_ref[...])
    o_ref[...] = _layer_norm(DN_ALPHA * x_ref[...] + mix, lnw_ref[...], lnb_ref[...])


def _merge(x2, yh, yf, yb, bonus, g, gates, p_ones, gn_w, gn_b, w_hy_out, w_rw_out, w_o, ln_w, ln_b, tm=512):
    m, d = x2.shape
    _, tiles, L, _ = yh.shape
    tm = min(tm, L)
    per_seq = L // tm
    W = RW_WIDTH
    row = lambda width: pl.BlockSpec((tm, width), lambda i: (i, 0))
    full = lambda shape: pl.BlockSpec(shape, lambda i: tuple(0 for _ in shape))
    return pl.pallas_call(
        _merge_kernel,
        grid=(m // tm,),
        in_specs=[row(d), pl.BlockSpec((1, tiles, tm, LANES), lambda i: (i // per_seq, 0, i % per_seq, 0)),
                  row(W), row(W), row(W), row(W), row(GATE_COLS),
                  full(p_ones.shape), full((1, W)), full((1, W)),
                  full((HY_WIDTH, d)), full((W, d)), full((d, d)), full((1, d)), full((1, d))],
        out_specs=row(d),
        out_shape=jax.ShapeDtypeStruct((m, d), F32),
        compiler_params=_cparams(("parallel",)),
        name="merge_ln1",
    )(x2, yh, yf, yb, bonus, g, gates, p_ones, gn_w.reshape(1, W), gn_b.reshape(1, W),
      w_hy_out.astype(BF16), w_rw_out.astype(BF16), w_o.astype(BF16), ln_w.reshape(1, d), ln_b.reshape(1, d))


def _ffn_kernel(x_ref, wg_ref, wu_ref, wd_ref, lnw_ref, lnb_ref, o_ref):
    x = x_ref[...]
    xb = x.astype(BF16)
    hidden = jax.nn.silu(_dot(xb, wg_ref[...])) * _dot(xb, wu_ref[...])
    ffn = _dot(hidden.astype(BF16), wd_ref[...])
    o_ref[...] = _layer_norm(DN_ALPHA * x + ffn, lnw_ref[...], lnb_ref[...])


def _ffn(x2, w_gate, w_up, w_down, ln_w, ln_b, tm=1024):
    m, d = x2.shape
    tm = min(tm, m)
    fh = w_gate.shape[1]
    resident = lambda shape: pl.BlockSpec(shape, lambda i: (0, 0), pipeline_mode=pl.Buffered(1))
    return pl.pallas_call(
        _ffn_kernel,
        grid=(m // tm,),
        in_specs=[pl.BlockSpec((tm, d), lambda i: (i, 0)),
                  resident((d, fh)), resident((d, fh)), resident((fh, d)),
                  resident((1, d)), resident((1, d))],
        out_specs=pl.BlockSpec((tm, d), lambda i: (i, 0)),
        out_shape=jax.ShapeDtypeStruct((m, d), F32),
        compiler_params=_cparams(("parallel",)),
        name="ffn_ln2",
    )(x2, w_gate.astype(BF16), w_up.astype(BF16), w_down.astype(BF16), ln_w.reshape(1, d), ln_b.reshape(1, d))


def _layer(x, w_in, hy_conv_w, hy_conv_b, hy_filt_w1, hy_filt_b1, hy_filt_w2, hy_filt_b2,
           hy_filt_w3, hy_filt_b3, hy_filt_w4, hy_sin_freq, hy_skip, rw_mu, rw_w0, rw_w2,
           rw_a0, rw_a2, rw_g2, rw_k_k, rw_k_a, rw_r_k, rw_gn_w, rw_gn_b, w_hy_out, w_rw_out,
           w_o, ln1_w, ln1_b, ffn_w_gate, ffn_w_up, ffn_w_down, ln2_w, ln2_b):
    B, L, D = x.shape
    x2 = x.reshape(B * L, D)
    u3, u_r, gates = _in_proj(x2, L, w_in.astype(BF16), hy_conv_w, hy_conv_b)
    y_h = _hyena(u3, hy_filt_w1, hy_filt_b1, hy_filt_w2, hy_filt_b2, hy_filt_w3, hy_filt_b3, hy_filt_w4,
                 hy_sin_freq, hy_skip)
    p_ones = _head_ones()
    fwd_ops, bwd_ops, g, bonus, ncb = _rw_prep(
        u_r.reshape(B, L, RW_COLS), rw_mu, rw_w0, rw_w2, rw_a0, rw_a2, rw_g2, rw_k_k, rw_k_a, rw_r_k, p_ones)
    yf, yb = _rw_scan(fwd_ops, bwd_ops, ncb)
    flat = lambda a: a.reshape(B * L, a.shape[-1])
    h = _merge(x2, y_h, flat(yf), flat(yb), flat(bonus), flat(g), gates, p_ones, rw_gn_w, rw_gn_b,
               w_hy_out, w_rw_out, w_o, ln1_w, ln1_b)
    out = _ffn(h, ffn_w_gate, ffn_w_up, ffn_w_down, ln2_w, ln2_b)
    return out.reshape(B, L, D)


def kernel(x, w_in, hy_conv_w, hy_conv_b, hy_filt_w1, hy_filt_b1, hy_filt_w2, hy_filt_b2, hy_filt_w3, hy_filt_b3, hy_filt_w4, hy_sin_freq, hy_skip, rw_mu, rw_w0, rw_w2, rw_a0, rw_a2, rw_g2, rw_k_k, rw_k_a, rw_r_k, rw_gn_w, rw_gn_b, w_hy_out, w_rw_out, w_o, ln1_w, ln1_b, ffn_w_gate, ffn_w_up, ffn_w_down, ln2_w, ln2_b):
    params = (w_in, hy_conv_w, hy_conv_b, hy_filt_w1, hy_filt_b1, hy_filt_w2, hy_filt_b2, hy_filt_w3,
              hy_filt_b3, hy_filt_w4, hy_sin_freq, hy_skip, rw_mu, rw_w0, rw_w2, rw_a0, rw_a2, rw_g2,
              rw_k_k, rw_k_a, rw_r_k, rw_gn_w, rw_gn_b, w_hy_out, w_rw_out, w_o, ln1_w, ln1_b,
              ffn_w_gate, ffn_w_up, ffn_w_down, ln2_w, ln2_b)
    for l in range(w_in.shape[0]):
        x = _layer(x, *[p[l] for p in params])
    return x
```

```python
import functools
import math

import numpy as np
import jax
import jax.numpy as jnp
from jax import lax
from jax.experimental import pallas as pl
from jax.experimental.pallas import tpu as pltpu

F32 = jnp.float32
BF16 = jnp.bfloat16
HIGHEST = lax.Precision.HIGHEST

D_MODEL = 1024
HY_WIDTH = 512
HY_ORDER = 2
HY_BANDS = 16
HY_FILT_HIDDEN = 64
HY_FAST_DECAY = 0.3
HY_SLOW_DECAY = 1.5
HY_DECAY_TARGET = 1e-2
HY_MAX_DECAY = math.log(HY_DECAY_TARGET) / HY_FAST_DECAY
HY_MIN_DECAY = math.log(HY_DECAY_TARGET) / HY_SLOW_DECAY
HY_COLS = 3 * HY_WIDTH
RW_WIDTH = 512
RW_HEAD = 64
RW_LORA = 64
RW_LORA_G = 128
RW_GN_EPS = 64e-5
RW_COLS = 3 * RW_WIDTH + 4 * RW_LORA + RW_LORA_G
GATE_COLS = 2 * D_MODEL
DEPTH = 1
DN_ALPHA = (2.0 * DEPTH) ** 0.25
LN_EPS = 1e-5
RW_DECAY_SCALE = math.exp(-0.5)

LANES = 128
VMEM_LIMIT = 56 * 1024 * 1024

FFT_N1 = 32
FFT_F1 = FFT_N1 // 2 + 1
RW_CHUNK = 64


def _cparams(sem, vmem=VMEM_LIMIT):
    return pltpu.CompilerParams(dimension_semantics=sem, vmem_limit_bytes=vmem)


def _dot(a, b, precision=None):
    return jnp.dot(a, b, preferred_element_type=F32, precision=precision)


def _layer_norm(h, w, b):
    mu = jnp.mean(h, axis=-1, keepdims=True)
    c = h - mu
    var = jnp.mean(c * c, axis=-1, keepdims=True)
    return c * lax.rsqrt(var + LN_EPS) * w + b


def _hi_lo(x):
    hi = x.astype(BF16)
    return hi, (x - hi.astype(F32)).astype(BF16)


def _segsum(x, p, terms=2):
    q = p.shape[0]
    parts = _hi_lo(x)[:terms]
    return jnp.concatenate([_dot_parts_t([t[:, s:s + q] for t in parts], p) for s in range(0, x.shape[1], q)], axis=1)


def _dot_hl(a, b):
    a_hi, a_lo = _hi_lo(a)
    b_hi, b_lo = _hi_lo(b)
    return _dot(a_hi, b_hi) + _dot(a_lo, b_hi) + _dot(a_hi, b_lo)


def _dot_parts_t(parts, m):
    out = _dot(parts[0], m)
    for part in parts[1:]:
        out = out + _dot(part, m)
    return out


def _dot_parts(m, parts):
    out = _dot(m, parts[0])
    for part in parts[1:]:
        out = out + _dot(m, part)
    return out


HY_TILES = HY_WIDTH // LANES
HALO = 8


def _in_proj_kernel(per_seq, x_ref, xp_ref, xn_ref, w_ref, cw_ref, cb_ref, u3_ref, ur_ref, g_ref):
    pos = pl.program_id(0) % per_seq
    w_h = w_ref[:, :HY_COLS]
    xb = x_ref[...].astype(BF16)
    tm = xb.shape[0]
    ext = _dot(jnp.concatenate([xp_ref[...].astype(BF16), xb, xn_ref[...].astype(BF16)], axis=0), w_h)
    u = ext[HALO:HALO + tm]
    row = lax.broadcasted_iota(jnp.int32, (tm, 1), 0)
    prev = jnp.where((row == 0) & (pos == 0), 0.0, ext[HALO - 1:HALO - 1 + tm])
    nxt = jnp.where((row == tm - 1) & (pos == per_seq - 1), 0.0, ext[HALO + 1:HALO + 1 + tm])
    conv = cw_ref[0:1, :] * prev + cw_ref[1:2, :] * u + cw_ref[2:3, :] * nxt + cb_ref[...]
    for s in range(HY_COLS // HY_WIDTH):
        for q in range(HY_TILES):
            lo = s * HY_WIDTH + q * LANES
            u3_ref[s, 0, q] = conv[:, lo:lo + LANES].astype(BF16)
    ur_ref[...] = _dot(xb, w_ref[:, HY_COLS:HY_COLS + RW_COLS])
    g_ref[...] = _dot(xb, w_ref[:, HY_COLS + RW_COLS:]).astype(BF16)


def _in_proj(x2, L, w_in_bf, conv_w, conv_b, tm=512):
    m, d = x2.shape
    n = w_in_bf.shape[1]
    tm = min(tm, L)
    per_seq = L // tm
    groups = tm // HALO
    resident = lambda shape: pl.BlockSpec(shape, lambda i: (0, 0), pipeline_mode=pl.Buffered(1))
    slabs = HY_COLS // HY_WIDTH
    return pl.pallas_call(
        functools.partial(_in_proj_kernel, per_seq),
        grid=(m // tm,),
        in_specs=[pl.BlockSpec((tm, d), lambda i: (i, 0)),
                  pl.BlockSpec((HALO, d), lambda i: (jnp.maximum(i * groups - 1, 0), 0)),
                  pl.BlockSpec((HALO, d), lambda i: (jnp.minimum((i + 1) * groups, m // HALO - 1), 0)),
                  resident((d, n)), resident((3, HY_COLS)), resident((1, HY_COLS))],
        out_specs=[pl.BlockSpec((slabs, 1, HY_TILES, tm, LANES), lambda i: (0, i // per_seq, 0, i % per_seq, 0)),
                   pl.BlockSpec((tm, RW_COLS), lambda i: (i, 0)),
                   pl.BlockSpec((tm, GATE_COLS), lambda i: (i, 0))],
        out_shape=[jax.ShapeDtypeStruct((slabs, m // L, HY_TILES, L, LANES), BF16),
                   jax.ShapeDtypeStruct((m, RW_COLS), F32),
                   jax.ShapeDtypeStruct((m, GATE_COLS), BF16)],
        compiler_params=_cparams(("parallel",)),
        name="in_proj",
    )(x2, x2, x2, w_in_bf, conv_w, conv_b.reshape(1, HY_COLS))


def _filter_kernel(L, tb, w1t_ref, w1c_ref, w1s_ref, b1_ref, w2_ref, b2_ref, w3_ref, b3_ref,
                   w4_ref, sf_ref, freq_ref, delta_ref, kc_ref):
    n = 2 * L
    i = pl.program_id(0)
    row = i * tb + lax.broadcasted_iota(jnp.int32, (tb, 1), 0)
    second = row >= L
    pos = jnp.where(second, n - row, row).astype(F32)
    t = pos / float(L - 1)
    ang = (2.0 * math.pi * pos / float(L)) * freq_ref[...]
    pre = t * w1t_ref[...] + _dot(jnp.cos(ang), w1c_ref[...], HIGHEST) \
        - _dot(jnp.sin(ang), w1s_ref[...], HIGHEST) + b1_ref[...]
    h = jnp.sin(sf_ref[0:1, :] * pre)
    h = jnp.sin(sf_ref[1:2, :] * (_dot(h, w2_ref[...], HIGHEST) + b2_ref[...]))
    h = jnp.sin(sf_ref[2:3, :] * (_dot(h, w3_ref[...], HIGHEST) + b3_ref[...]))
    h4 = _dot_hl(h, w4_ref[...])
    window = jnp.exp(-t * delta_ref[...])
    keep = jnp.where(row == L, 0.0, 1.0)
    first = jnp.where(row == 0, 1.0, 0.0)
    for o in range(HY_ORDER):
        base = o * 2 * HY_WIDTH
        fwd = h4[:, base:base + HY_WIDTH]
        bwd = h4[:, base + HY_WIDTH:base + 2 * HY_WIDTH]
        kc = window * (jnp.where(second, bwd, fwd) * keep + first * bwd)
        for q in range(HY_TILES):
            kc_ref[o, q] = kc[:, q * LANES:(q + 1) * LANES]


def _filters(L, fw1, fb1, fw2, fb2, fw3, fb3, fw4, sin_freq, tb=512):
    n = 2 * L
    tb = min(tb, n)
    freqs = jnp.linspace(1e-4, HY_BANDS - 1, HY_BANDS, dtype=F32).reshape(1, HY_BANDS)
    deltas = jnp.abs(jnp.linspace(HY_MIN_DECAY, HY_MAX_DECAY, HY_WIDTH, dtype=F32)).reshape(1, HY_WIDTH)
    hid = HY_FILT_HIDDEN
    full = lambda shape: pl.BlockSpec(shape, lambda i: tuple(0 for _ in shape))
    return pl.pallas_call(
        functools.partial(_filter_kernel, L, tb),
        grid=(n // tb,),
        in_specs=[full((1, hid)), full((HY_BANDS, hid)), full((HY_BANDS, hid)), full((1, hid)),
                  full((hid, hid)), full((1, hid)), full((hid, hid)), full((1, hid)),
                  full((hid, HY_ORDER * 2 * HY_WIDTH)), full((3, hid)),
                  full((1, HY_BANDS)), full((1, HY_WIDTH))],
        out_specs=pl.BlockSpec((HY_ORDER, HY_TILES, tb, LANES), lambda i: (0, 0, i, 0)),
        out_shape=jax.ShapeDtypeStruct((HY_ORDER, HY_TILES, n, LANES), F32),
        compiler_params=_cparams(("parallel",)),
        name="hy_filters",
    )(fw1[0:1], fw1[1:1 + HY_BANDS], fw1[1 + HY_BANDS:], fb1.reshape(1, hid), fw2, fb2.reshape(1, hid),
      fw3, fb3.reshape(1, hid), fw4, sin_freq, freqs, deltas)


def _fft_tables(n2):
    n = FFT_N1 * n2
    f1 = np.arange(FFT_F1)[:, None, None]
    a = np.arange(n2)[None, :, None]
    b = np.arange(n2)[None, None, :]
    ph = ((b * (f1 + FFT_N1 * a)) % n) * (2.0 * math.pi / n)
    cr, ci = np.cos(ph), -np.sin(ph)
    m_fwd = np.concatenate([np.concatenate([cr, -ci], axis=2), np.concatenate([ci, cr], axis=2)], axis=1)
    herm = np.where((f1 == 0) | (f1 == FFT_N1 // 2), 1.0, 2.0) / n
    dr, di = herm * cr.transpose(0, 2, 1), -herm * ci.transpose(0, 2, 1)
    m_inv = np.concatenate([np.concatenate([dr, -di], axis=2), np.concatenate([di, dr], axis=2)], axis=1)
    return jnp.asarray(m_fwd, F32).astype(BF16), jnp.asarray(m_inv, F32).astype(BF16)


def _slow_tables():
    k = (np.arange(FFT_F1)[:, None] * np.arange(FFT_N1)[None, :]) % FFT_N1
    ph = k * (2.0 * math.pi / FFT_N1)
    fwd = np.stack([np.cos(ph), -np.sin(ph)], axis=1).reshape(2 * FFT_F1, FFT_N1)
    inv = fwd[:, :FFT_N1 // 2].T
    return fwd, inv


SLOW_UNROLL = 4
ROW_GROUP = 16
SLOW_TILES = 2


def _kron_hi_lo(m):
    return jnp.concatenate(_hi_lo(jnp.asarray(np.kron(m, np.eye(ROW_GROUP)), F32)), axis=0)


def _slow_dft_kernel(n2, s1_count, f_ref, z_ref, a_ref):
    f_hl = f_ref[...]
    half = f_hl.shape[0] // 2
    st = z_ref.shape[2]

    def body(g, carry):
        s2 = pl.multiple_of(g * ROW_GROUP, ROW_GROUP)
        zs = jnp.concatenate(
            [_tiles_cat(z_ref, lambda q: (0, 0, q, pl.ds(s1 * n2 + s2, ROW_GROUP)), st) for s1 in range(s1_count)],
            axis=0).astype(BF16)
        a = _dot(f_hl, zs)
        a = (a[:half] + a[half:]).astype(BF16)
        for f in range(half // ROW_GROUP):
            _tiles_put(a_ref, lambda q: (0, q, pl.ds(f * n2 + s2, ROW_GROUP)), a[f * ROW_GROUP:(f + 1) * ROW_GROUP])
        return carry

    lax.fori_loop(0, n2 // ROW_GROUP, body, 0, unroll=min(SLOW_UNROLL, n2 // ROW_GROUP))


def _tiles_cat(ref, index, ntiles):
    return jnp.concatenate([ref[index(q) + (slice(None),)] for q in range(ntiles)], axis=1)


def _tiles_put(ref, index, value):
    for q in range(value.shape[1] // LANES):
        ref[index(q) + (slice(None),)] = value[:, q * LANES:(q + 1) * LANES]


def _slow_dft(z5, sel, s1_count):
    _, B, tiles, T, _ = z5.shape
    n2 = T // s1_count
    fwd, _ = _slow_tables()
    rows = 2 * FFT_F1
    f_hl = _kron_hi_lo(fwd[:, :s1_count])
    st = SLOW_TILES
    return pl.pallas_call(
        functools.partial(_slow_dft_kernel, n2, s1_count),
        grid=(B, tiles // st),
        in_specs=[pl.BlockSpec(f_hl.shape, lambda b, j: (0, 0)),
                  pl.BlockSpec((1, 1, st, T, LANES), lambda b, j: (sel, b, j, 0, 0))],
        out_specs=pl.BlockSpec((1, st, rows * n2, LANES), lambda b, j: (b, j, 0, 0)),
        out_shape=jax.ShapeDtypeStruct((B, tiles, rows * n2, LANES), BF16),
        compiler_params=_cparams(("parallel", "parallel")),
        name="hy_slow_dft",
    )(f_hl, z5)


def _spectrum_kernel(n2, a_ref, mf_ref, hr_ref, hi_ref):
    x = _dot(mf_ref[0], _tiles_cat(a_ref, lambda q: (0, q), a_ref.shape[1]))
    hr_ref[0] = x[:n2]
    hi_ref[0] = x[n2:]


def _spectrum(kc, m_fwd):
    order, tiles, n, _ = kc.shape
    n2 = n // FFT_N1
    C = tiles * LANES
    a = _slow_dft(kc.reshape(1, order, tiles, n, LANES), 0, FFT_N1)
    spec = lambda: pl.BlockSpec((1, n2, C), lambda f, o: (o, f, 0))
    return pl.pallas_call(
        functools.partial(_spectrum_kernel, n2),
        grid=(FFT_F1, order),
        in_specs=[pl.BlockSpec((1, tiles, 2 * n2, LANES), lambda f, o: (o, 0, f, 0)),
                  pl.BlockSpec((1, 2 * n2, 2 * n2), lambda f, o: (f, 0, 0))],
        out_specs=[spec(), spec()],
        out_shape=[jax.ShapeDtypeStruct((order, FFT_F1 * n2, C), F32)] * 2,
        compiler_params=_cparams(("parallel", "parallel")),
        name="hy_spectrum",
    )(a, m_fwd)


MID_SEQS = 8


def _conv_mid_kernel(n2, a_ref, hr_ref, hi_ref, mf_ref, mi_ref, b_ref):
    hr, hi = hr_ref[0], hi_ref[0]
    for n in range(a_ref.shape[0]):
        x = _dot(mf_ref[0], _tiles_cat(a_ref, lambda q: (n, q), a_ref.shape[1]))
        xr, xi = x[:n2], x[n2:]
        y = jnp.concatenate([xr * hr - xi * hi, xr * hi + xi * hr], axis=0).astype(BF16)
        _tiles_put(b_ref, lambda q: (n, q), _dot(mi_ref[0], y).astype(BF16))


def _conv_mid(a, order, hr, hi, m_fwd, m_inv):
    B, tiles, total, _ = a.shape
    rows = total // FFT_F1
    n2 = rows // 2
    W = tiles * LANES
    nb = MID_SEQS if B % MID_SEQS == 0 else 1
    blk = lambda: pl.BlockSpec((nb, tiles, rows, LANES), lambda f, b: (b, 0, f, 0))
    return pl.pallas_call(
        functools.partial(_conv_mid_kernel, n2),
        grid=(FFT_F1, B // nb),
        in_specs=[blk(),
                  pl.BlockSpec((1, n2, W), lambda f, b: (order, f, 0)),
                  pl.BlockSpec((1, n2, W), lambda f, b: (order, f, 0)),
                  pl.BlockSpec((1, rows, rows), lambda f, b: (f, 0, 0)),
                  pl.BlockSpec((1, rows, rows), lambda f, b: (f, 0, 0))],
        out_specs=blk(),
        out_shape=jax.ShapeDtypeStruct((B, tiles, total, LANES), BF16),
        compiler_params=_cparams(("parallel", "parallel")),
        name=f"hy_conv_mid{order}",
    )(a, hr, hi, m_fwd, m_inv)


IDFT_ROWS = tuple(f for f in range(2 * FFT_F1) if f not in (1, 2 * FFT_F1 - 1))


def _slow_idft_kernel(n2, g_ref, b_ref, z_ref, gate_ref, skip_ref, o_ref):
    g_hl = g_ref[...]
    half = g_hl.shape[0] // 2
    skip = skip_ref[...]
    st = b_ref.shape[1]

    def body(g, carry):
        t2 = pl.multiple_of(g * ROW_GROUP, ROW_GROUP)
        bs = jnp.concatenate(
            [_tiles_cat(b_ref, lambda q: (0, q, pl.ds(f * n2 + t2, ROW_GROUP)), st) for f in IDFT_ROWS], axis=0)
        y2 = _dot(g_hl, bs)
        y = y2[:half] + y2[half:]
        for t1 in range(half // ROW_GROUP):
            tok = lambda q: (0, 0, q, pl.ds(t1 * n2 + t2, ROW_GROUP))
            z = _tiles_cat(z_ref, tok, st)
            gate = _tiles_cat(gate_ref, tok, st)
            out = gate * (y[t1 * ROW_GROUP:(t1 + 1) * ROW_GROUP] + skip * z)
            _tiles_put(o_ref, tok, out.astype(o_ref.dtype))
        return carry

    lax.fori_loop(0, n2 // ROW_GROUP, body, 0, unroll=min(SLOW_UNROLL, n2 // ROW_GROUP))


def _slow_idft(bm, z5, zsel, g5, gsel, skip_row, out_dtype):
    B, tiles, total, _ = bm.shape
    t1 = FFT_N1 // 2
    n2 = total // (2 * FFT_F1)
    T = t1 * n2
    _, inv = _slow_tables()
    g_hl = _kron_hi_lo(inv[:, list(IDFT_ROWS)])
    st = SLOW_TILES
    tok = lambda sel: pl.BlockSpec((1, 1, st, T, LANES), lambda b, j: (sel, b, j, 0, 0))
    return pl.pallas_call(
        functools.partial(_slow_idft_kernel, n2),
        grid=(B, tiles // st),
        in_specs=[pl.BlockSpec(g_hl.shape, lambda b, j: (0, 0)),
                  pl.BlockSpec((1, st, total, LANES), lambda b, j: (b, j, 0, 0)),
                  tok(zsel), tok(gsel),
                  pl.BlockSpec((1, st * LANES), lambda b, j: (0, j))],
        out_specs=tok(0),
        out_shape=jax.ShapeDtypeStruct((1, B, tiles, T, LANES), out_dtype),
        compiler_params=_cparams(("parallel", "parallel")),
        name="hy_slow_idft",
    )(g_hl, bm, z5, g5, skip_row.reshape(1, tiles * LANES))


def _longconv(z5, zsel, g5, gsel, order, skip, hr, hi, m_fwd, m_inv, out_dtype):
    a = _slow_dft(z5, zsel, FFT_N1 // 2)
    bm = _conv_mid(a, order, hr, hi, m_fwd, m_inv)
    return _slow_idft(bm, z5, zsel, g5, gsel, skip[order], out_dtype)


def _hyena(u3, fw1, fb1, fw2, fb2, fw3, fb3, fw4, sin_freq, skip):
    L = u3.shape[3]
    kc = _filters(L, fw1, fb1, fw2, fb2, fw3, fb3, fw4, sin_freq)
    m_fwd, m_inv = _fft_tables(2 * L // FFT_N1)
    hr, hi = _spectrum(kc, m_fwd)
    z1 = _longconv(u3, 0, u3, 1, 0, skip, hr, hi, m_fwd, m_inv, BF16)
    return _longconv(z1, 0, u3, 2, 1, skip, hr, hi, m_fwd, m_inv, BF16)[0]


def _rw_prep_kernel(u_ref, up_ref, un_ref, mu_ref, w0_ref, w2f_ref, w2b_ref, a0_ref, a2f_ref, a2b_ref,
                    g2_ref, kk_ref, ka_ref, rk_ref, p_ref, trif_ref, trib_ref, sel_ref,
                    v_o, ktf_o, rtf_o, khf_o, bhf_o, kbf_o, bbf_o, wtf_o,
                    ktb_o, rtb_o, khb_o, bhb_o, kbb_o, bbb_o, wtb_o, g_o, bonus_o):
    j = pl.program_id(1)
    nj = pl.num_programs(1)
    u = u_ref[0]
    tb = u.shape[0]
    prow = jnp.where(j == 0, 0.0, up_ref[0, 7:8, :])
    nrow = jnp.where(j == nj - 1, 0.0, un_ref[0, 0:1, :])
    row = lax.broadcasted_iota(jnp.int32, (tb, 1), 0)
    prev = jnp.where(row == 0, prow, pltpu.roll(u, 1, 0))
    nxt = jnp.where(row == tb - 1, nrow, pltpu.roll(u, tb - 1, 0))
    mu = mu_ref[...]
    xs = (1.0 - mu) * u + (0.5 * mu) * (prev + nxt)
    W = RW_WIDTH
    r, k, v = xs[:, 0:W], xs[:, W:2 * W], xs[:, 2 * W:3 * W]
    wd = jnp.tanh(xs[:, 3 * W:3 * W + 2 * RW_LORA]).astype(BF16)
    ad = xs[:, 3 * W + 2 * RW_LORA:3 * W + 4 * RW_LORA].astype(BF16)
    gd = jax.nn.sigmoid(xs[:, 3 * W + 4 * RW_LORA:]).astype(BF16)
    p = p_ref[...]
    kkn = k * kk_ref[...]
    kk = kkn * lax.rsqrt(jnp.maximum(_segsum(kkn * kkn, p, terms=1), 1e-24))
    ka = ka_ref[...]
    lw_f = -RW_DECAY_SCALE * jax.nn.sigmoid(w0_ref[0:1, :] + _dot(wd, w2f_ref[...]))
    lw_b = -RW_DECAY_SCALE * jax.nn.sigmoid(w0_ref[1:2, :] + _dot(wd, w2b_ref[...]))
    a_f = jax.nn.sigmoid(a0_ref[0:1, :] + _dot(ad, a2f_ref[...]))
    a_b = jax.nn.sigmoid(a0_ref[1:2, :] + _dot(ad, a2b_ref[...]))
    kd_f = k * (1.0 + (a_f - 1.0) * ka)
    kd_b = k * (1.0 + (a_b - 1.0) * ka)
    v_o[0] = v.astype(BF16)
    sel = sel_ref[...]
    ncb = sel.shape[0]
    c = tb // ncb

    def scan_operands(lw, kd, b, tri, outs):
        parts = _hi_lo(lw)
        cum = _dot_parts(tri, parts)
        wt = jnp.exp(_dot_parts(sel, parts))
        wt_tok = jnp.concatenate([jnp.broadcast_to(wt[j:j + 1], (c, wt.shape[1])) for j in range(ncb)], axis=0)
        e_neg = jnp.exp(-cum)
        kh = kd * e_neg
        bh = b * e_neg
        kt_o, rt_o, kh_o, bh_o, kb_o, bb_o, wt_o = outs
        kt_o[0] = (kk * jnp.exp(cum - lw)).astype(BF16)
        rt_o[0] = (r * jnp.exp(cum)).astype(BF16)
        kh_o[0] = kh.astype(BF16)
        bh_o[0] = bh.astype(BF16)
        kb_o[0] = (kh * wt_tok).astype(BF16)
        bb_o[0] = (bh * wt_tok).astype(BF16)
        wt_o[0] = wt

    scan_operands(lw_f, kd_f, kk * a_f, trif_ref[...], (ktf_o, rtf_o, khf_o, bhf_o, kbf_o, bbf_o, wtf_o))
    scan_operands(lw_b, kd_b, kk * a_b, trib_ref[...], (ktb_o, rtb_o, khb_o, bhb_o, kbb_o, bbb_o, wtb_o))
    g_o[0] = _dot(gd, g2_ref[...]).astype(BF16)
    bonus_o[0] = (_segsum(r * (kd_f + kd_b) * rk_ref[...], p) * v).astype(BF16)


def _head_ones():
    h = np.arange(RW_QUAD) // RW_HEAD
    return (h[:, None] == h[None, :]).astype(BF16)


def _chunk_matrices(tb, c):
    t = np.arange(tb)
    same = (t[:, None] // c) == (t[None, :] // c)
    tri_f = (same & (t[None, :] <= t[:, None])).astype(BF16)
    tri_b = (same & (t[None, :] >= t[:, None])).astype(BF16)
    sel = (np.arange(tb // c)[:, None] == (t[None, :] // c)).astype(BF16)
    return tri_f, tri_b, sel


def _rw_prep(u_r, mu, w0, w2, a0, a2, g2, k_k, k_a, r_k, p_ones, tb=512):
    B, L, C = u_r.shape
    tb = min(tb, L)
    W = RW_WIDTH
    c = min(RW_CHUNK, L)
    ncb = tb // c
    tri_f, tri_b, sel = _chunk_matrices(tb, c)
    zeros = jnp.zeros((RW_LORA, W), F32)
    w2f = jnp.concatenate([w2[0], zeros], axis=0).astype(BF16)
    w2b = jnp.concatenate([zeros, w2[1]], axis=0).astype(BF16)
    a2f = jnp.concatenate([a2[0], zeros], axis=0).astype(BF16)
    a2b = jnp.concatenate([zeros, a2[1]], axis=0).astype(BF16)
    full = lambda shape: pl.BlockSpec(shape, lambda b, j: tuple(0 for _ in shape))
    tok = lambda: pl.BlockSpec((1, tb, W), lambda b, j: (b, j, 0))
    wts = lambda: pl.BlockSpec((1, ncb, W), lambda b, j: (b, j, 0))
    bf_tok = jax.ShapeDtypeStruct((B, L, W), BF16)
    wt_shape = jax.ShapeDtypeStruct((B, L // c, W), F32)
    g8 = tb // 8
    outs = pl.pallas_call(
        _rw_prep_kernel,
        grid=(B, L // tb),
        in_specs=[pl.BlockSpec((1, tb, C), lambda b, j: (b, j, 0)),
                  pl.BlockSpec((1, 8, C), lambda b, j: (b, jnp.maximum(j * g8 - 1, 0), 0)),
                  pl.BlockSpec((1, 8, C), lambda b, j: (b, jnp.minimum((j + 1) * g8, L // 8 - 1), 0)),
                  full((1, C)), full((2, W)), full((2 * RW_LORA, W)), full((2 * RW_LORA, W)),
                  full((2, W)), full((2 * RW_LORA, W)), full((2 * RW_LORA, W)),
                  full((RW_LORA_G, W)), full((1, W)), full((1, W)), full((1, W)), full(p_ones.shape),
                  full((tb, tb)), full((tb, tb)), full((ncb, tb))],
        out_specs=[tok()] + ([tok() for _ in range(6)] + [wts()]) * 2 + [tok(), tok()],
        out_shape=[bf_tok] + ([bf_tok] * 6 + [wt_shape]) * 2 + [bf_tok, bf_tok],
        compiler_params=_cparams(("parallel", "parallel")),
        name="rw_prep",
    )(u_r, u_r, u_r, mu.reshape(1, C), w0, w2f, w2b, a0, a2f, a2b, g2.astype(BF16),
      k_k.reshape(1, W), k_a.reshape(1, W), r_k.reshape(1, W), p_ones, tri_f, tri_b, sel)
    v, g, bonus = outs[0], outs[15], outs[16]
    fwd_ops = tuple(outs[1:7]) + (v, outs[7])
    bwd_ops = tuple(outs[8:14]) + (v, outs[14])
    return fwd_ops, bwd_ops, g, bonus, ncb


def _bmm(a, b):
    return lax.dot_general(a, b, (((2,), (1,)), ((0,), (0,))), preferred_element_type=F32)


def _bmm_nt(a, b):
    return lax.dot_general(a, b, (((2,), (2,)), ((0,), (0,))), preferred_element_type=F32)


def _bmm_tn(a, b):
    return lax.dot_general(a, b, (((1,), (1,)), ((0,), (0,))), preferred_element_type=F32)


RW_QUAD = 4 * RW_HEAD
RW_SEQS = 8


def _block_diag(x, same_head):
    reps = RW_QUAD // x.shape[1]
    return jnp.where(same_head, jnp.concatenate([x] * reps, axis=1), jnp.zeros((), x.dtype))


def _quads(x_f, x_b):
    halves = lambda x: [x[n, :, s:s + RW_QUAD] for n in range(x.shape[0]) for s in range(0, x.shape[2], RW_QUAD)]
    return jnp.stack(halves(x_f) + halves(x_b), axis=0)


def _rw_scan_kernel(nc, ncb, ktf, rtf, khf, bhf, kbf, bbf, vf, wtf, ktb, rtb, khb, bhb, kbb, bbb, vb, wtb,
                    yf_ref, yb_ref, s_ref):
    i = pl.program_id(1)

    @pl.when(i == 0)
    def _():
        s_ref[...] = jnp.zeros_like(s_ref)

    kt, rt = _quads(ktf[...], ktb[...]), _quads(rtf[...], rtb[...])
    khat, bhat = _quads(khf[...], khb[...]), _quads(bhf[...], bhb[...])
    kbar, bbar = _quads(kbf[...], kbb[...]), _quads(bbf[...], bbb[...])
    v = _quads(vf[...], vb[...])
    wt = _quads(wtf[:, pl.ds(i % ncb, 1), :], wtb[:, pl.ds((nc - 1 - i) % ncb, 1), :])
    g, c, q = kt.shape
    ri = lax.broadcasted_iota(jnp.int32, (1, c, q), 1)
    ci = lax.broadcasted_iota(jnp.int32, (1, c, q), 2) % c
    eye = jnp.where(ci == ri, 1.0, 0.0)
    same_head = (lax.broadcasted_iota(jnp.int32, (1, q, q), 1) // RW_HEAD
                 == lax.broadcasted_iota(jnp.int32, (1, q, q), 2) // RW_HEAD)
    bd = lambda x: _block_diag(x, same_head)

    def keep_earlier(x, strict):
        fwd = (ci < ri) if strict else (ci <= ri)
        bwd = (ci > ri) if strict else (ci >= ri)
        return jnp.concatenate([jnp.where(fwd, x[:g // 2], 0.0), jnp.where(bwd, x[g // 2:], 0.0)], axis=0)

    lh = jnp.concatenate([kt, rt], axis=1)
    gk = _bmm_nt(lh, bd(khat))
    gb = _bmm_nt(lh, bd(bhat))
    a_k = keep_earlier(gk[:, :c], True)
    b_k = keep_earlier(gk[:, c:], False)
    a_b = keep_earlier(gb[:, :c], True)
    b_b = keep_earlier(gb[:, c:], False)
    p = -a_b
    tmat = eye + p
    p = _bmm(p.astype(BF16), bd(p.astype(BF16)))
    doublings = int(math.log2(c)) - 1
    for step in range(doublings):
        p_bd = bd(p.astype(BF16))
        if step == doublings - 1:
            tmat = tmat + _bmm(tmat.astype(BF16), p_bd)
        else:
            both = _bmm(jnp.concatenate([p, tmat], axis=1).astype(BF16), p_bd)
            tmat = tmat + both[:, c:]
            p = both[:, :c]
    s0 = s_ref[...]
    ks = _bmm_nt(lh, s0.astype(BF16))
    av = _bmm(jnp.concatenate([a_k, b_k], axis=1).astype(BF16), bd(v))
    ub = _bmm(tmat.astype(BF16), bd((ks[:, :c] + av[:, :c]).astype(BF16))).astype(BF16)
    y = ks[:, c:] + av[:, c:] - _bmm(b_b.astype(BF16), bd(ub))
    upd = _bmm_tn(jnp.concatenate([v, ub], axis=1), jnp.concatenate([kbar, -bbar], axis=1))
    s_ref[...] = s0 * wt + jnp.where(same_head, upd, 0.0)
    per_seq = yf_ref.shape[2] // q
    for n in range(yf_ref.shape[0]):
        yf_ref[n] = jnp.concatenate([y[n * per_seq + j] for j in range(per_seq)], axis=1).astype(BF16)
        yb_ref[n] = jnp.concatenate([y[g // 2 + n * per_seq + j] for j in range(per_seq)], axis=1).astype(BF16)


def _rw_scan(fwd_ops, bwd_ops, ncb):
    B, L, W = fwd_ops[0].shape
    c = min(RW_CHUNK, L)
    nc = L // c
    nb = RW_SEQS if B % RW_SEQS == 0 else 1
    fwd = lambda: pl.BlockSpec((nb, c, W), lambda b, i: (b, i, 0))
    bwd = lambda: pl.BlockSpec((nb, c, W), lambda b, i: (b, nc - 1 - i, 0))
    wt_f = pl.BlockSpec((nb, ncb, W), lambda b, i: (b, i // ncb, 0))
    wt_b = pl.BlockSpec((nb, ncb, W), lambda b, i: (b, (nc - 1 - i) // ncb, 0))
    return pl.pallas_call(
        functools.partial(_rw_scan_kernel, nc, ncb),
        grid=(B // nb, nc),
        in_specs=[fwd() for _ in range(7)] + [wt_f] + [bwd() for _ in range(7)] + [wt_b],
        out_specs=[fwd(), bwd()],
        out_shape=[jax.ShapeDtypeStruct((B, L, W), BF16)] * 2,
        scratch_shapes=[pltpu.VMEM((2 * nb * W // RW_QUAD, RW_QUAD, RW_QUAD), F32)],
        compiler_params=_cparams(("parallel", "arbitrary")),
        name="rw_scan",
    )(*fwd_ops, *bwd_ops)


def _merge_kernel(x_ref, yh_ref, yf_ref, yb_ref, bonus_ref, g_ref, gates_ref, p_ref, gnw_ref, gnb_ref,
                  why_ref, wrw_ref, wo_ref, lnw_ref, lnb_ref, o_ref):
    p = p_ref[...]
    y = yf_ref[...].astype(F32) + yb_ref[...].astype(F32)
    mu = _segsum(y, p) * (1.0 / RW_HEAD)
    yc = y - mu
    var = _segsum(yc * yc, p, terms=1) * (1.0 / RW_HEAD)
    yn = yc * lax.rsqrt(var + RW_GN_EPS) * gnw_ref[...] + gnb_ref[...]
    y_r = (yn + bonus_ref[...].astype(F32)) * g_ref[...].astype(F32)
    ph = _dot(_tiles_cat(yh_ref, lambda q: (0, q), yh_ref.shape[1]).astype(BF16), why_ref[...])
    pr = _dot(y_r.astype(BF16), wrw_ref[...])
    gates = jax.nn.sigmoid(gates_ref[...].astype(F32))
    m = gates[:, :D_MODEL] * ph + gates[:, D_MODEL:] * pr
    mix = _dot(m.astype(BF16), wo_ref[...])
    o_ref[...] = _layer_norm(DN_ALPHA * x_ref[...] + mix, lnw_ref[...], lnb_ref[...])


def _merge(x2, yh, yf, yb, bonus, g, gates, p_ones, gn_w, gn_b, w_hy_out, w_rw_out, w_o, ln_w, ln_b, tm=512):
    m, d = x2.shape
    _, tiles, L, _ = yh.shape
    tm = min(tm, L)
    per_seq = L // tm
    W = RW_WIDTH
    row = lambda width: pl.BlockSpec((tm, width), lambda i: (i, 0))
    full = lambda shape: pl.BlockSpec(shape, lambda i: tuple(0 for _ in shape))
    return pl.pallas_call(
        _merge_kernel,
        grid=(m // tm,),
        in_specs=[row(d), pl.BlockSpec((1, tiles, tm, LANES), lambda i: (i // per_seq, 0, i % per_seq, 0)),
                  row(W), row(W), row(W), row(W), row(GATE_COLS),
                  full(p_ones.shape), full((1, W)), full((1, W)),
                  full((HY_WIDTH, d)), full((W, d)), full((d, d)), full((1, d)), full((1, d))],
        out_specs=row(d),
        out_shape=jax.ShapeDtypeStruct((m, d), F32),
        compiler_params=_cparams(("parallel",)),
        name="merge_ln1",
    )(x2, yh, yf, yb, bonus, g, gates, p_ones, gn_w.reshape(1, W), gn_b.reshape(1, W),
      w_hy_out.astype(BF16), w_rw_out.astype(BF16), w_o.astype(BF16), ln_w.reshape(1, d), ln_b.reshape(1, d))


def _ffn_kernel(x_ref, wg_ref, wu_ref, wd_ref, lnw_ref, lnb_ref, o_ref):
    x = x_ref[...]
    xb = x.astype(BF16)
    hidden = jax.nn.silu(_dot(xb, wg_ref[...])) * _dot(xb, wu_ref[...])
    ffn = _dot(hidden.astype(BF16), wd_ref[...])
    o_ref[...] = _layer_norm(DN_ALPHA * x + ffn, lnw_ref[...], lnb_ref[...])


def _ffn(x2, w_gate, w_up, w_down, ln_w, ln_b, tm=1024):
    m, d = x2.shape
    tm = min(tm, m)
    fh = w_gate.shape[1]
    resident = lambda shape: pl.BlockSpec(shape, lambda i: (0, 0), pipeline_mode=pl.Buffered(1))
    return pl.pallas_call(
        _ffn_kernel,
        grid=(m // tm,),
        in_specs=[pl.BlockSpec((tm, d), lambda i: (i, 0)),
                  resident((d, fh)), resident((d, fh)), resident((fh, d)),
                  resident((1, d)), resident((1, d))],
        out_specs=pl.BlockSpec((tm, d), lambda i: (i, 0)),
        out_shape=jax.ShapeDtypeStruct((m, d), F32),
        compiler_params=_cparams(("parallel",)),
        name="ffn_ln2",
    )(x2, w_gate.astype(BF16), w_up.astype(BF16), w_down.astype(BF16), ln_w.reshape(1, d), ln_b.reshape(1, d))


def _layer(x, w_in, hy_conv_w, hy_conv_b, hy_filt_w1, hy_filt_b1, hy_filt_w2, hy_filt_b2,
           hy_filt_w3, hy_filt_b3, hy_filt_w4, hy_sin_freq, hy_skip, rw_mu, rw_w0, rw_w2,
           rw_a0, rw_a2, rw_g2, rw_k_k, rw_k_a, rw_r_k, rw_gn_w, rw_gn_b, w_hy_out, w_rw_out,
           w_o, ln1_w, ln1_b, ffn_w_gate, ffn_w_up, ffn_w_down, ln2_w, ln2_b):
    B, L, D = x.shape
    x2 = x.reshape(B * L, D)
    u3, u_r, gates = _in_proj(x2, L, w_in.astype(BF16), hy_conv_w, hy_conv_b)
    y_h = _hyena(u3, hy_filt_w1, hy_filt_b1, hy_filt_w2, hy_filt_b2, hy_filt_w3, hy_filt_b3, hy_filt_w4,
                 hy_sin_freq, hy_skip)
    p_ones = _head_ones()
    fwd_ops, bwd_ops, g, bonus, ncb = _rw_prep(
        u_r.reshape(B, L, RW_COLS), rw_mu, rw_w0, rw_w2, rw_a0, rw_a2, rw_g2, rw_k_k, rw_k_a, rw_r_k, p_ones)
    yf, yb = _rw_scan(fwd_ops, bwd_ops, ncb)
    flat = lambda a: a.reshape(B * L, a.shape[-1])
    h = _merge(x2, y_h, flat(yf), flat(yb), flat(bonus), flat(g), gates, p_ones, rw_gn_w, rw_gn_b,
               w_hy_out, w_rw_out, w_o, ln1_w, ln1_b)
    out = _ffn(h, ffn_w_gate, ffn_w_up, ffn_w_down, ln2_w, ln2_b)
    return out.reshape(B, L, D)


def kernel(x, w_in, hy_conv_w, hy_conv_b, hy_filt_w1, hy_filt_b1, hy_filt_w2, hy_filt_b2, hy_filt_w3, hy_filt_b3, hy_filt_w4, hy_sin_freq, hy_skip, rw_mu, rw_w0, rw_w2, rw_a0, rw_a2, rw_g2, rw_k_k, rw_k_a, rw_r_k, rw_gn_w, rw_gn_b, w_hy_out, w_rw_out, w_o, ln1_w, ln1_b, ffn_w_gate, ffn_w_up, ffn_w_down, ln2_w, ln2_b):
    params = (w_in, hy_conv_w, hy_conv_b, hy_filt_w1, hy_filt_b1, hy_filt_w2, hy_filt_b2, hy_filt_w3,
              hy_filt_b3, hy_filt_w4, hy_sin_freq, hy_skip, rw_mu, rw_w0, rw_w2, rw_a0, rw_a2, rw_g2,
              rw_k_k, rw_k_a, rw_r_k, rw_gn_w, rw_gn_b, w_hy_out, w_rw_out, w_o, ln1_w, ln1_b,
              ffn_w_gate, ffn_w_up, ffn_w_down, ln2_w, ln2_b)
    for l in range(w_in.shape[0]):
        x = _layer(x, *[p[l] for p in params])
    return x
```

```python
import functools
import math

import numpy as np
import jax
import jax.numpy as jnp
from jax import lax
from jax.experimental import pallas as pl
from jax.experimental.pallas import tpu as pltpu

F32 = jnp.float32
BF16 = jnp.bfloat16
HIGHEST = lax.Precision.HIGHEST

D_MODEL = 1024
HY_WIDTH = 512
HY_ORDER = 2
HY_BANDS = 16
HY_FILT_HIDDEN = 64
HY_FAST_DECAY = 0.3
HY_SLOW_DECAY = 1.5
HY_DECAY_TARGET = 1e-2
HY_MAX_DECAY = math.log(HY_DECAY_TARGET) / HY_FAST_DECAY
HY_MIN_DECAY = math.log(HY_DECAY_TARGET) / HY_SLOW_DECAY
HY_COLS = 3 * HY_WIDTH
RW_WIDTH = 512
RW_HEAD = 64
RW_LORA = 64
RW_LORA_G = 128
RW_GN_EPS = 64e-5
RW_COLS = 3 * RW_WIDTH + 4 * RW_LORA + RW_LORA_G
GATE_COLS = 2 * D_MODEL
DEPTH = 1
DN_ALPHA = (2.0 * DEPTH) ** 0.25
LN_EPS = 1e-5
RW_DECAY_SCALE = math.exp(-0.5)

LANES = 128
VMEM_LIMIT = 56 * 1024 * 1024

FFT_N1 = 32
FFT_F1 = FFT_N1 // 2 + 1
RW_CHUNK = 64


def _cparams(sem, vmem=VMEM_LIMIT):
    return pltpu.CompilerParams(dimension_semantics=sem, vmem_limit_bytes=vmem)


def _dot(a, b, precision=None):
    return jnp.dot(a, b, preferred_element_type=F32, precision=precision)


def _layer_norm(h, w, b):
    mu = jnp.mean(h, axis=-1, keepdims=True)
    c = h - mu
    var = jnp.mean(c * c, axis=-1, keepdims=True)
    return c * lax.rsqrt(var + LN_EPS) * w + b


def _hi_lo(x):
    hi = x.astype(BF16)
    return hi, (x - hi.astype(F32)).astype(BF16)


def _segsum(x, p, terms=2):
    q = p.shape[0]
    parts = _hi_lo(x)[:terms]
    return jnp.concatenate([_dot_parts_t([t[:, s:s + q] for t in parts], p) for s in range(0, x.shape[1], q)], axis=1)


def _dot_hl(a, b):
    a_hi, a_lo = _hi_lo(a)
    b_hi, b_lo = _hi_lo(b)
    return _dot(a_hi, b_hi) + _dot(a_lo, b_hi) + _dot(a_hi, b_lo)


def _dot_parts_t(parts, m):
    out = _dot(parts[0], m)
    for part in parts[1:]:
        out = out + _dot(part, m)
    return out


def _dot_parts(m, parts):
    out = _dot(m, parts[0])
    for part in parts[1:]:
        out = out + _dot(m, part)
    return out


HY_TILES = HY_WIDTH // LANES
HALO = 8


def _in_proj_kernel(per_seq, x_ref, xp_ref, xn_ref, w_ref, cw_ref, cb_ref, u3_ref, ur_ref, g_ref):
    pos = pl.program_id(0) % per_seq
    w_h = w_ref[:, :HY_COLS]
    xb = x_ref[...].astype(BF16)
    tm = xb.shape[0]
    ext = _dot(jnp.concatenate([xp_ref[...].astype(BF16), xb, xn_ref[...].astype(BF16)], axis=0), w_h)
    u = ext[HALO:HALO + tm]
    row = lax.broadcasted_iota(jnp.int32, (tm, 1), 0)
    prev = jnp.where((row == 0) & (pos == 0), 0.0, ext[HALO - 1:HALO - 1 + tm])
    nxt = jnp.where((row == tm - 1) & (pos == per_seq - 1), 0.0, ext[HALO + 1:HALO + 1 + tm])
    conv = cw_ref[0:1, :] * prev + cw_ref[1:2, :] * u + cw_ref[2:3, :] * nxt + cb_ref[...]
    for s in range(HY_COLS // HY_WIDTH):
        for q in range(HY_TILES):
            lo = s * HY_WIDTH + q * LANES
            u3_ref[s, 0, q] = conv[:, lo:lo + LANES].astype(BF16)
    ur_ref[...] = _dot(xb, w_ref[:, HY_COLS:HY_COLS + RW_COLS])
    g_ref[...] = _dot(xb, w_ref[:, HY_COLS + RW_COLS:]).astype(BF16)


def _in_proj(x2, L, w_in_bf, conv_w, conv_b, tm=512):
    m, d = x2.shape
    n = w_in_bf.shape[1]
    tm = min(tm, L)
    per_seq = L // tm
    groups = tm // HALO
    resident = lambda shape: pl.BlockSpec(shape, lambda i: (0, 0), pipeline_mode=pl.Buffered(1))
    slabs = HY_COLS // HY_WIDTH
    return pl.pallas_call(
        functools.partial(_in_proj_kernel, per_seq),
        grid=(m // tm,),
        in_specs=[pl.BlockSpec((tm, d), lambda i: (i, 0)),
                  pl.BlockSpec((HALO, d), lambda i: (jnp.maximum(i * groups - 1, 0), 0)),
                  pl.BlockSpec((HALO, d), lambda i: (jnp.minimum((i + 1) * groups, m // HALO - 1), 0)),
                  resident((d, n)), resident((3, HY_COLS)), resident((1, HY_COLS))],
        out_specs=[pl.BlockSpec((slabs, 1, HY_TILES, tm, LANES), lambda i: (0, i // per_seq, 0, i % per_seq, 0)),
                   pl.BlockSpec((tm, RW_COLS), lambda i: (i, 0)),
                   pl.BlockSpec((tm, GATE_COLS), lambda i: (i, 0))],
        out_shape=[jax.ShapeDtypeStruct((slabs, m // L, HY_TILES, L, LANES), BF16),
                   jax.ShapeDtypeStruct((m, RW_COLS), F32),
                   jax.ShapeDtypeStruct((m, GATE_COLS), BF16)],
        compiler_params=_cparams(("parallel",)),
        name="in_proj",
    )(x2, x2, x2, w_in_bf, conv_w, conv_b.reshape(1, HY_COLS))


def _filter_kernel(L, tb, w1t_ref, w1c_ref, w1s_ref, b1_ref, w2_ref, b2_ref, w3_ref, b3_ref,
                   w4_ref, sf_ref, freq_ref, delta_ref, kc_ref):
    n = 2 * L
    i = pl.program_id(0)
    row = i * tb + lax.broadcasted_iota(jnp.int32, (tb, 1), 0)
    second = row >= L
    pos = jnp.where(second, n - row, row).astype(F32)
    t = pos / float(L - 1)
    ang = (2.0 * math.pi * pos / float(L)) * freq_ref[...]
    pre = t * w1t_ref[...] + _dot(jnp.cos(ang), w1c_ref[...], HIGHEST) \
        - _dot(jnp.sin(ang), w1s_ref[...], HIGHEST) + b1_ref[...]
    h = jnp.sin(sf_ref[0:1, :] * pre)
    h = jnp.sin(sf_ref[1:2, :] * (_dot(h, w2_ref[...], HIGHEST) + b2_ref[...]))
    h = jnp.sin(sf_ref[2:3, :] * (_dot(h, w3_ref[...], HIGHEST) + b3_ref[...]))
    h4 = _dot_hl(h, w4_ref[...])
    window = jnp.exp(-t * delta_ref[...])
    keep = jnp.where(row == L, 0.0, 1.0)
    first = jnp.where(row == 0, 1.0, 0.0)
    for o in range(HY_ORDER):
        base = o * 2 * HY_WIDTH
        fwd = h4[:, base:base + HY_WIDTH]
        bwd = h4[:, base + HY_WIDTH:base + 2 * HY_WIDTH]
        kc = window * (jnp.where(second, bwd, fwd) * keep + first * bwd)
        for q in range(HY_TILES):
            kc_ref[o, q] = kc[:, q * LANES:(q + 1) * LANES]


def _filters(L, fw1, fb1, fw2, fb2, fw3, fb3, fw4, sin_freq, tb=512):
    n = 2 * L
    tb = min(tb, n)
    freqs = jnp.linspace(1e-4, HY_BANDS - 1, HY_BANDS, dtype=F32).reshape(1, HY_BANDS)
    deltas = jnp.abs(jnp.linspace(HY_MIN_DECAY, HY_MAX_DECAY, HY_WIDTH, dtype=F32)).reshape(1, HY_WIDTH)
    hid = HY_FILT_HIDDEN
    full = lambda shape: pl.BlockSpec(shape, lambda i: tuple(0 for _ in shape))
    return pl.pallas_call(
        functools.partial(_filter_kernel, L, tb),
        grid=(n // tb,),
        in_specs=[full((1, hid)), full((HY_BANDS, hid)), full((HY_BANDS, hid)), full((1, hid)),
                  full((hid, hid)), full((1, hid)), full((hid, hid)), full((1, hid)),
                  full((hid, HY_ORDER * 2 * HY_WIDTH)), full((3, hid)),
                  full((1, HY_BANDS)), full((1, HY_WIDTH))],
        out_specs=pl.BlockSpec((HY_ORDER, HY_TILES, tb, LANES), lambda i: (0, 0, i, 0)),
        out_shape=jax.ShapeDtypeStruct((HY_ORDER, HY_TILES, n, LANES), F32),
        compiler_params=_cparams(("parallel",)),
        name="hy_filters",
    )(fw1[0:1], fw1[1:1 + HY_BANDS], fw1[1 + HY_BANDS:], fb1.reshape(1, hid), fw2, fb2.reshape(1, hid),
      fw3, fb3.reshape(1, hid), fw4, sin_freq, freqs, deltas)


def _fft_tables(n2):
    n = FFT_N1 * n2
    f1 = np.arange(FFT_F1)[:, None, None]
    a = np.arange(n2)[None, :, None]
    b = np.arange(n2)[None, None, :]
    ph = ((b * (f1 + FFT_N1 * a)) % n) * (2.0 * math.pi / n)
    cr, ci = np.cos(ph), -np.sin(ph)
    m_fwd = np.concatenate([np.concatenate([cr, -ci], axis=2), np.concatenate([ci, cr], axis=2)], axis=1)
    herm = np.where((f1 == 0) | (f1 == FFT_N1 // 2), 1.0, 2.0) / n
    dr, di = herm * cr.transpose(0, 2, 1), -herm * ci.transpose(0, 2, 1)
    m_inv = np.concatenate([np.concatenate([dr, -di], axis=2), np.concatenate([di, dr], axis=2)], axis=1)
    return jnp.asarray(m_fwd, F32).astype(BF16), jnp.asarray(m_inv, F32).astype(BF16)


def _slow_tables():
    k = (np.arange(FFT_F1)[:, None] * np.arange(FFT_N1)[None, :]) % FFT_N1
    ph = k * (2.0 * math.pi / FFT_N1)
    fwd = np.stack([np.cos(ph), -np.sin(ph)], axis=1).reshape(2 * FFT_F1, FFT_N1)
    inv = fwd[:, :FFT_N1 // 2].T
    return fwd, inv


SLOW_UNROLL = 4
ROW_GROUP = 16
SLOW_TILES = 4


def _kron_hi_lo(m):
    return jnp.concatenate(_hi_lo(jnp.asarray(np.kron(m, np.eye(ROW_GROUP)), F32)), axis=0)


def _slow_dft_kernel(n2, s1_count, f_ref, z_ref, a_ref):
    f_hl = f_ref[...]
    half = f_hl.shape[0] // 2
    st = z_ref.shape[2]

    def body(g, carry):
        s2 = pl.multiple_of(g * ROW_GROUP, ROW_GROUP)
        zs = jnp.concatenate(
            [_tiles_cat(z_ref, lambda q: (0, 0, q, pl.ds(s1 * n2 + s2, ROW_GROUP)), st) for s1 in range(s1_count)],
            axis=0).astype(BF16)
        a = _dot(f_hl, zs)
        a = (a[:half] + a[half:]).astype(BF16)
        for f in range(half // ROW_GROUP):
            _tiles_put(a_ref, lambda q: (0, q, pl.ds(f * n2 + s2, ROW_GROUP)), a[f * ROW_GROUP:(f + 1) * ROW_GROUP])
        return carry

    lax.fori_loop(0, n2 // ROW_GROUP, body, 0, unroll=min(SLOW_UNROLL, n2 // ROW_GROUP))


def _tiles_cat(ref, index, ntiles):
    return jnp.concatenate([ref[index(q) + (slice(None),)] for q in range(ntiles)], axis=1)


def _tiles_put(ref, index, value):
    for q in range(value.shape[1] // LANES):
        ref[index(q) + (slice(None),)] = value[:, q * LANES:(q + 1) * LANES]


def _slow_dft(z5, sel, s1_count):
    _, B, tiles, T, _ = z5.shape
    n2 = T // s1_count
    fwd, _ = _slow_tables()
    rows = 2 * FFT_F1
    f_hl = _kron_hi_lo(fwd[:, :s1_count])
    st = SLOW_TILES
    return pl.pallas_call(
        functools.partial(_slow_dft_kernel, n2, s1_count),
        grid=(B, tiles // st),
        in_specs=[pl.BlockSpec(f_hl.shape, lambda b, j: (0, 0)),
                  pl.BlockSpec((1, 1, st, T, LANES), lambda b, j: (sel, b, j, 0, 0))],
        out_specs=pl.BlockSpec((1, st, rows * n2, LANES), lambda b, j: (b, j, 0, 0)),
        out_shape=jax.ShapeDtypeStruct((B, tiles, rows * n2, LANES), BF16),
        compiler_params=_cparams(("parallel", "parallel")),
        name="hy_slow_dft",
    )(f_hl, z5)


def _spectrum_kernel(n2, a_ref, mf_ref, hr_ref, hi_ref):
    x = _dot(mf_ref[0], _tiles_cat(a_ref, lambda q: (0, q), a_ref.shape[1]))
    hr_ref[0] = x[:n2]
    hi_ref[0] = x[n2:]


def _spectrum(kc, m_fwd):
    order, tiles, n, _ = kc.shape
    n2 = n // FFT_N1
    C = tiles * LANES
    a = _slow_dft(kc.reshape(1, order, tiles, n, LANES), 0, FFT_N1)
    spec = lambda: pl.BlockSpec((1, n2, C), lambda f, o: (o, f, 0))
    return pl.pallas_call(
        functools.partial(_spectrum_kernel, n2),
        grid=(FFT_F1, order),
        in_specs=[pl.BlockSpec((1, tiles, 2 * n2, LANES), lambda f, o: (o, 0, f, 0)),
                  pl.BlockSpec((1, 2 * n2, 2 * n2), lambda f, o: (f, 0, 0))],
        out_specs=[spec(), spec()],
        out_shape=[jax.ShapeDtypeStruct((order, FFT_F1 * n2, C), F32)] * 2,
        compiler_params=_cparams(("parallel", "parallel")),
        name="hy_spectrum",
    )(a, m_fwd)


MID_SEQS = 8


def _conv_mid_kernel(n2, a_ref, hr_ref, hi_ref, mf_ref, mi_ref, b_ref):
    hr, hi = hr_ref[0], hi_ref[0]
    for n in range(a_ref.shape[0]):
        x = _dot(mf_ref[0], _tiles_cat(a_ref, lambda q: (n, q), a_ref.shape[1]))
        xr, xi = x[:n2], x[n2:]
        y = jnp.concatenate([xr * hr - xi * hi, xr * hi + xi * hr], axis=0).astype(BF16)
        _tiles_put(b_ref, lambda q: (n, q), _dot(mi_ref[0], y).astype(BF16))


def _conv_mid(a, order, hr, hi, m_fwd, m_inv):
    B, tiles, total, _ = a.shape
    rows = total // FFT_F1
    n2 = rows // 2
    W = tiles * LANES
    nb = MID_SEQS if B % MID_SEQS == 0 else 1
    blk = lambda: pl.BlockSpec((nb, tiles, rows, LANES), lambda f, b: (b, 0, f, 0))
    return pl.pallas_call(
        functools.partial(_conv_mid_kernel, n2),
        grid=(FFT_F1, B // nb),
        in_specs=[blk(),
                  pl.BlockSpec((1, n2, W), lambda f, b: (order, f, 0)),
                  pl.BlockSpec((1, n2, W), lambda f, b: (order, f, 0)),
                  pl.BlockSpec((1, rows, rows), lambda f, b: (f, 0, 0)),
                  pl.BlockSpec((1, rows, rows), lambda f, b: (f, 0, 0))],
        out_specs=blk(),
        out_shape=jax.ShapeDtypeStruct((B, tiles, total, LANES), BF16),
        compiler_params=_cparams(("parallel", "parallel")),
        name=f"hy_conv_mid{order}",
    )(a, hr, hi, m_fwd, m_inv)


IDFT_ROWS = tuple(f for f in range(2 * FFT_F1) if f not in (1, 2 * FFT_F1 - 1))


def _slow_idft_kernel(n2, g_ref, b_ref, z_ref, gate_ref, skip_ref, o_ref):
    g_hl = g_ref[...]
    half = g_hl.shape[0] // 2
    skip = skip_ref[...]
    st = b_ref.shape[1]

    def body(g, carry):
        t2 = pl.multiple_of(g * ROW_GROUP, ROW_GROUP)
        bs = jnp.concatenate(
            [_tiles_cat(b_ref, lambda q: (0, q, pl.ds(f * n2 + t2, ROW_GROUP)), st) for f in IDFT_ROWS], axis=0)
        y2 = _dot(g_hl, bs)
        y = y2[:half] + y2[half:]
        for t1 in range(half // ROW_GROUP):
            tok = lambda q: (0, 0, q, pl.ds(t1 * n2 + t2, ROW_GROUP))
            z = _tiles_cat(z_ref, tok, st)
            gate = _tiles_cat(gate_ref, tok, st)
            out = gate * (y[t1 * ROW_GROUP:(t1 + 1) * ROW_GROUP] + skip * z)
            _tiles_put(o_ref, tok, out.astype(o_ref.dtype))
        return carry

    lax.fori_loop(0, n2 // ROW_GROUP, body, 0, unroll=min(SLOW_UNROLL, n2 // ROW_GROUP))


def _slow_idft(bm, z5, zsel, g5, gsel, skip_row, out_dtype):
    B, tiles, total, _ = bm.shape
    t1 = FFT_N1 // 2
    n2 = total // (2 * FFT_F1)
    T = t1 * n2
    _, inv = _slow_tables()
    g_hl = _kron_hi_lo(inv[:, list(IDFT_ROWS)])
    st = SLOW_TILES
    tok = lambda sel: pl.BlockSpec((1, 1, st, T, LANES), lambda b, j: (sel, b, j, 0, 0))
    return pl.pallas_call(
        functools.partial(_slow_idft_kernel, n2),
        grid=(B, tiles // st),
        in_specs=[pl.BlockSpec(g_hl.shape, lambda b, j: (0, 0)),
                  pl.BlockSpec((1, st, total, LANES), lambda b, j: (b, j, 0, 0)),
                  tok(zsel), tok(gsel),
                  pl.BlockSpec((1, st * LANES), lambda b, j: (0, j))],
        out_specs=tok(0),
        out_shape=jax.ShapeDtypeStruct((1, B, tiles, T, LANES), out_dtype),
        compiler_params=_cparams(("parallel", "parallel")),
        name="hy_slow_idft",
    )(g_hl, bm, z5, g5, skip_row.reshape(1, tiles * LANES))


def _longconv(z5, zsel, g5, gsel, order, skip, hr, hi, m_fwd, m_inv, out_dtype):
    a = _slow_dft(z5, zsel, FFT_N1 // 2)
    bm = _conv_mid(a, order, hr, hi, m_fwd, m_inv)
    return _slow_idft(bm, z5, zsel, g5, gsel, skip[order], out_dtype)


def _hyena(u3, fw1, fb1, fw2, fb2, fw3, fb3, fw4, sin_freq, skip):
    L = u3.shape[3]
    kc = _filters(L, fw1, fb1, fw2, fb2, fw3, fb3, fw4, sin_freq)
    m_fwd, m_inv = _fft_tables(2 * L // FFT_N1)
    hr, hi = _spectrum(kc, m_fwd)
    z1 = _longconv(u3, 0, u3, 1, 0, skip, hr, hi, m_fwd, m_inv, BF16)
    return _longconv(z1, 0, u3, 2, 1, skip, hr, hi, m_fwd, m_inv, BF16)[0]


def _rw_prep_kernel(u_ref, up_ref, un_ref, mu_ref, w0_ref, w2f_ref, w2b_ref, a0_ref, a2f_ref, a2b_ref,
                    g2_ref, kk_ref, ka_ref, rk_ref, p_ref, trif_ref, trib_ref, sel_ref,
                    v_o, ktf_o, rtf_o, khf_o, bhf_o, kbf_o, bbf_o, wtf_o,
                    ktb_o, rtb_o, khb_o, bhb_o, kbb_o, bbb_o, wtb_o, g_o, bonus_o):
    j = pl.program_id(1)
    nj = pl.num_programs(1)
    u = u_ref[0]
    tb = u.shape[0]
    prow = jnp.where(j == 0, 0.0, up_ref[0, 7:8, :])
    nrow = jnp.where(j == nj - 1, 0.0, un_ref[0, 0:1, :])
    row = lax.broadcasted_iota(jnp.int32, (tb, 1), 0)
    prev = jnp.where(row == 0, prow, pltpu.roll(u, 1, 0))
    nxt = jnp.where(row == tb - 1, nrow, pltpu.roll(u, tb - 1, 0))
    mu = mu_ref[...]
    xs = (1.0 - mu) * u + (0.5 * mu) * (prev + nxt)
    W = RW_WIDTH
    r, k, v = xs[:, 0:W], xs[:, W:2 * W], xs[:, 2 * W:3 * W]
    wd = jnp.tanh(xs[:, 3 * W:3 * W + 2 * RW_LORA]).astype(BF16)
    ad = xs[:, 3 * W + 2 * RW_LORA:3 * W + 4 * RW_LORA].astype(BF16)
    gd = jax.nn.sigmoid(xs[:, 3 * W + 4 * RW_LORA:]).astype(BF16)
    p = p_ref[...]
    kkn = k * kk_ref[...]
    kk = kkn * lax.rsqrt(jnp.maximum(_segsum(kkn * kkn, p, terms=1), 1e-24))
    ka = ka_ref[...]
    lw_f = -RW_DECAY_SCALE * jax.nn.sigmoid(w0_ref[0:1, :] + _dot(wd, w2f_ref[...]))
    lw_b = -RW_DECAY_SCALE * jax.nn.sigmoid(w0_ref[1:2, :] + _dot(wd, w2b_ref[...]))
    a_f = jax.nn.sigmoid(a0_ref[0:1, :] + _dot(ad, a2f_ref[...]))
    a_b = jax.nn.sigmoid(a0_ref[1:2, :] + _dot(ad, a2b_ref[...]))
    kd_f = k * (1.0 + (a_f - 1.0) * ka)
    kd_b = k * (1.0 + (a_b - 1.0) * ka)
    v_o[0] = v.astype(BF16)
    sel = sel_ref[...]
    ncb = sel.shape[0]
    c = tb // ncb

    def scan_operands(lw, kd, b, tri, outs):
        parts = _hi_lo(lw)
        cum = _dot_parts(tri, parts)
        wt = jnp.exp(_dot_parts(sel, parts))
        wt_tok = jnp.concatenate([jnp.broadcast_to(wt[j:j + 1], (c, wt.shape[1])) for j in range(ncb)], axis=0)
        e_neg = jnp.exp(-cum)
        kh = kd * e_neg
        bh = b * e_neg
        kt_o, rt_o, kh_o, bh_o, kb_o, bb_o, wt_o = outs
        kt_o[0] = (kk * jnp.exp(cum - lw)).astype(BF16)
        rt_o[0] = (r * jnp.exp(cum)).astype(BF16)
        kh_o[0] = kh.astype(BF16)
        bh_o[0] = bh.astype(BF16)
        kb_o[0] = (kh * wt_tok).astype(BF16)
        bb_o[0] = (bh * wt_tok).astype(BF16)
        wt_o[0] = wt

    scan_operands(lw_f, kd_f, kk * a_f, trif_ref[...], (ktf_o, rtf_o, khf_o, bhf_o, kbf_o, bbf_o, wtf_o))
    scan_operands(lw_b, kd_b, kk * a_b, trib_ref[...], (ktb_o, rtb_o, khb_o, bhb_o, kbb_o, bbb_o, wtb_o))
    g_o[0] = _dot(gd, g2_ref[...]).astype(BF16)
    bonus_o[0] = (_segsum(r * (kd_f + kd_b) * rk_ref[...], p) * v).astype(BF16)


def _head_ones():
    h = np.arange(RW_QUAD) // RW_HEAD
    return (h[:, None] == h[None, :]).astype(BF16)


def _chunk_matrices(tb, c):
    t = np.arange(tb)
    same = (t[:, None] // c) == (t[None, :] // c)
    tri_f = (same & (t[None, :] <= t[:, None])).astype(BF16)
    tri_b = (same & (t[None, :] >= t[:, None])).astype(BF16)
    sel = (np.arange(tb // c)[:, None] == (t[None, :] // c)).astype(BF16)
    return tri_f, tri_b, sel


def _rw_prep(u_r, mu, w0, w2, a0, a2, g2, k_k, k_a, r_k, p_ones, tb=512):
    B, L, C = u_r.shape
    tb = min(tb, L)
    W = RW_WIDTH
    c = min(RW_CHUNK, L)
    ncb = tb // c
    tri_f, tri_b, sel = _chunk_matrices(tb, c)
    zeros = jnp.zeros((RW_LORA, W), F32)
    w2f = jnp.concatenate([w2[0], zeros], axis=0).astype(BF16)
    w2b = jnp.concatenate([zeros, w2[1]], axis=0).astype(BF16)
    a2f = jnp.concatenate([a2[0], zeros], axis=0).astype(BF16)
    a2b = jnp.concatenate([zeros, a2[1]], axis=0).astype(BF16)
    full = lambda shape: pl.BlockSpec(shape, lambda b, j: tuple(0 for _ in shape))
    tok = lambda: pl.BlockSpec((1, tb, W), lambda b, j: (b, j, 0))
    wts = lambda: pl.BlockSpec((1, ncb, W), lambda b, j: (b, j, 0))
    bf_tok = jax.ShapeDtypeStruct((B, L, W), BF16)
    wt_shape = jax.ShapeDtypeStruct((B, L // c, W), F32)
    g8 = tb // 8
    outs = pl.pallas_call(
        _rw_prep_kernel,
        grid=(B, L // tb),
        in_specs=[pl.BlockSpec((1, tb, C), lambda b, j: (b, j, 0)),
                  pl.BlockSpec((1, 8, C), lambda b, j: (b, jnp.maximum(j * g8 - 1, 0), 0)),
                  pl.BlockSpec((1, 8, C), lambda b, j: (b, jnp.minimum((j + 1) * g8, L // 8 - 1), 0)),
                  full((1, C)), full((2, W)), full((2 * RW_LORA, W)), full((2 * RW_LORA, W)),
                  full((2, W)), full((2 * RW_LORA, W)), full((2 * RW_LORA, W)),
                  full((RW_LORA_G, W)), full((1, W)), full((1, W)), full((1, W)), full(p_ones.shape),
                  full((tb, tb)), full((tb, tb)), full((ncb, tb))],
        out_specs=[tok()] + ([tok() for _ in range(6)] + [wts()]) * 2 + [tok(), tok()],
        out_shape=[bf_tok] + ([bf_tok] * 6 + [wt_shape]) * 2 + [bf_tok, bf_tok],
        compiler_params=_cparams(("parallel", "parallel")),
        name="rw_prep",
    )(u_r, u_r, u_r, mu.reshape(1, C), w0, w2f, w2b, a0, a2f, a2b, g2.astype(BF16),
      k_k.reshape(1, W), k_a.reshape(1, W), r_k.reshape(1, W), p_ones, tri_f, tri_b, sel)
    v, g, bonus = outs[0], outs[15], outs[16]
    fwd_ops = tuple(outs[1:7]) + (v, outs[7])
    bwd_ops = tuple(outs[8:14]) + (v, outs[14])
    return fwd_ops, bwd_ops, g, bonus, ncb


def _bmm(a, b):
    return lax.dot_general(a, b, (((2,), (1,)), ((0,), (0,))), preferred_element_type=F32)


def _bmm_nt(a, b):
    return lax.dot_general(a, b, (((2,), (2,)), ((0,), (0,))), preferred_element_type=F32)


def _bmm_tn(a, b):
    return lax.dot_general(a, b, (((1,), (1,)), ((0,), (0,))), preferred_element_type=F32)


RW_QUAD = 4 * RW_HEAD
RW_SEQS = 8


def _block_diag(x, same_head):
    reps = RW_QUAD // x.shape[1]
    return jnp.where(same_head, jnp.concatenate([x] * reps, axis=1), jnp.zeros((), x.dtype))


def _quads(x_f, x_b):
    halves = lambda x: [x[n, :, s:s + RW_QUAD] for n in range(x.shape[0]) for s in range(0, x.shape[2], RW_QUAD)]
    return jnp.stack(halves(x_f) + halves(x_b), axis=0)


def _rw_scan_kernel(nc, ncb, ktf, rtf, khf, bhf, kbf, bbf, vf, wtf, ktb, rtb, khb, bhb, kbb, bbb, vb, wtb,
                    yf_ref, yb_ref, s_ref):
    i = pl.program_id(1)

    @pl.when(i == 0)
    def _():
        s_ref[...] = jnp.zeros_like(s_ref)

    kt, rt = _quads(ktf[...], ktb[...]), _quads(rtf[...], rtb[...])
    khat, bhat = _quads(khf[...], khb[...]), _quads(bhf[...], bhb[...])
    kbar, bbar = _quads(kbf[...], kbb[...]), _quads(bbf[...], bbb[...])
    v = _quads(vf[...], vb[...])
    wt = _quads(wtf[:, pl.ds(i % ncb, 1), :], wtb[:, pl.ds((nc - 1 - i) % ncb, 1), :])
    g, c, q = kt.shape
    ri = lax.broadcasted_iota(jnp.int32, (1, c, q), 1)
    ci = lax.broadcasted_iota(jnp.int32, (1, c, q), 2) % c
    eye = jnp.where(ci == ri, 1.0, 0.0)
    same_head = (lax.broadcasted_iota(jnp.int32, (1, q, q), 1) // RW_HEAD
                 == lax.broadcasted_iota(jnp.int32, (1, q, q), 2) // RW_HEAD)
    bd = lambda x: _block_diag(x, same_head)

    def keep_earlier(x, strict):
        fwd = (ci < ri) if strict else (ci <= ri)
        bwd = (ci > ri) if strict else (ci >= ri)
        return jnp.concatenate([jnp.where(fwd, x[:g // 2], 0.0), jnp.where(bwd, x[g // 2:], 0.0)], axis=0)

    lh = jnp.concatenate([kt, rt], axis=1)
    gk = _bmm_nt(lh, bd(khat))
    gb = _bmm_nt(lh, bd(bhat))
    a_k = keep_earlier(gk[:, :c], True)
    b_k = keep_earlier(gk[:, c:], False)
    a_b = keep_earlier(gb[:, :c], True)
    b_b = keep_earlier(gb[:, c:], False)
    p = -a_b
    tmat = eye + p
    p = _bmm(p.astype(BF16), bd(p.astype(BF16)))
    doublings = int(math.log2(c)) - 1
    for step in range(doublings):
        p_bd = bd(p.astype(BF16))
        if step == doublings - 1:
            tmat = tmat + _bmm(tmat.astype(BF16), p_bd)
        else:
            both = _bmm(jnp.concatenate([p, tmat], axis=1).astype(BF16), p_bd)
            tmat = tmat + both[:, c:]
            p = both[:, :c]
    s0 = s_ref[...]
    ks = _bmm_nt(lh, s0.astype(BF16))
    av = _bmm(jnp.concatenate([a_k, b_k], axis=1).astype(BF16), bd(v))
    ub = _bmm(tmat.astype(BF16), bd((ks[:, :c] + av[:, :c]).astype(BF16))).astype(BF16)
    y = ks[:, c:] + av[:, c:] - _bmm(b_b.astype(BF16), bd(ub))
    upd = _bmm_tn(jnp.concatenate([v, ub], axis=1), jnp.concatenate([kbar, -bbar], axis=1))
    s_ref[...] = s0 * wt + jnp.where(same_head, upd, 0.0)
    per_seq = yf_ref.shape[2] // q
    for n in range(yf_ref.shape[0]):
        yf_ref[n] = jnp.concatenate([y[n * per_seq + j] for j in range(per_seq)], axis=1).astype(BF16)
        yb_ref[n] = jnp.concatenate([y[g // 2 + n * per_seq + j] for j in range(per_seq)], axis=1).astype(BF16)


def _rw_scan(fwd_ops, bwd_ops, ncb):
    B, L, W = fwd_ops[0].shape
    c = min(RW_CHUNK, L)
    nc = L // c
    nb = RW_SEQS if B % RW_SEQS == 0 else 1
    fwd = lambda: pl.BlockSpec((nb, c, W), lambda b, i: (b, i, 0))
    bwd = lambda: pl.BlockSpec((nb, c, W), lambda b, i: (b, nc - 1 - i, 0))
    wt_f = pl.BlockSpec((nb, ncb, W), lambda b, i: (b, i // ncb, 0))
    wt_b = pl.BlockSpec((nb, ncb, W), lambda b, i: (b, (nc - 1 - i) // ncb, 0))
    return pl.pallas_call(
        functools.partial(_rw_scan_kernel, nc, ncb),
        grid=(B // nb, nc),
        in_specs=[fwd() for _ in range(7)] + [wt_f] + [bwd() for _ in range(7)] + [wt_b],
        out_specs=[fwd(), bwd()],
        out_shape=[jax.ShapeDtypeStruct((B, L, W), BF16)] * 2,
        scratch_shapes=[pltpu.VMEM((2 * nb * W // RW_QUAD, RW_QUAD, RW_QUAD), F32)],
        compiler_params=_cparams(("parallel", "arbitrary")),
        name="rw_scan",
    )(*fwd_ops, *bwd_ops)


def _merge_kernel(x_ref, yh_ref, yf_ref, yb_ref, bonus_ref, g_ref, gates_ref, p_ref, gnw_ref, gnb_ref,
                  why_ref, wrw_ref, wo_ref, lnw_ref, lnb_ref, o_ref):
    p = p_ref[...]
    y = yf_ref[...].astype(F32) + yb_ref[...].astype(F32)
    mu = _segsum(y, p) * (1.0 / RW_HEAD)
    yc = y - mu
    var = _segsum(yc * yc, p, terms=1) * (1.0 / RW_HEAD)
    yn = yc * lax.rsqrt(var + RW_GN_EPS) * gnw_ref[...] + gnb_ref[...]
    y_r = (yn + bonus_ref[...].astype(F32)) * g_ref[...].astype(F32)
    ph = _dot(_tiles_cat(yh_ref, lambda q: (0, q), yh_ref.shape[1]).astype(BF16), why_ref[...])
    pr = _dot(y_r.astype(BF16), wrw_ref[...])
    gates = jax.nn.sigmoid(gates_ref[...].astype(F32))
    m = gates[:, :D_MODEL] * ph + gates[:, D_MODEL:] * pr
    mix = _dot(m.astype(BF16), wo_ref[...])
    o_ref[...] = _layer_norm(DN_ALPHA * x_ref[...] + mix, lnw_ref[...], lnb_ref[...])


def _merge(x2, yh, yf, yb, bonus, g, gates, p_ones, gn_w, gn_b, w_hy_out, w_rw_out, w_o, ln_w, ln_b, tm=512):
    m, d = x2.shape
    _, tiles, L, _ = yh.shape
    tm = min(tm, L)
    per_seq = L // tm
    W = RW_WIDTH
    row = lambda width: pl.BlockSpec((tm, width), lambda i: (i, 0))
    full = lambda shape: pl.BlockSpec(shape, lambda i: tuple(0 for _ in shape))
    return pl.pallas_call(
        _merge_kernel,
        grid=(m // tm,),
        in_specs=[row(d), pl.BlockSpec((1, tiles, tm, LANES), lambda i: (i // per_seq, 0, i % per_seq, 0)),
                  row(W), row(W), row(W), row(W), row(GATE_COLS),
                  full(p_ones.shape), full((1, W)), full((1, W)),
                  full((HY_WIDTH, d)), full((W, d)), full((d, d)), full((1, d)), full((1, d))],
        out_specs=row(d),
        out_shape=jax.ShapeDtypeStruct((m, d), F32),
        compiler_params=_cparams(("parallel",)),
        name="merge_ln1",
    )(x2, yh, yf, yb, bonus, g, gates, p_ones, gn_w.reshape(1, W), gn_b.reshape(1, W),
      w_hy_out.astype(BF16), w_rw_out.astype(BF16), w_o.astype(BF16), ln_w.reshape(1, d), ln_b.reshape(1, d))


def _ffn_kernel(x_ref, wg_ref, wu_ref, wd_ref, lnw_ref, lnb_ref, o_ref):
    x = x_ref[...]
    xb = x.astype(BF16)
    hidden = jax.nn.silu(_dot(xb, wg_ref[...])) * _dot(xb, wu_ref[...])
    ffn = _dot(hidden.astype(BF16), wd_ref[...])
    o_ref[...] = _layer_norm(DN_ALPHA * x + ffn, lnw_ref[...], lnb_ref[...])


def _ffn(x2, w_gate, w_up, w_down, ln_w, ln_b, tm=1024):
    m, d = x2.shape
    tm = min(tm, m)
    fh = w_gate.shape[1]
    resident = lambda shape: pl.BlockSpec(shape, lambda i: (0, 0), pipeline_mode=pl.Buffered(1))
    return pl.pallas_call(
        _ffn_kernel,
        grid=(m // tm,),
        in_specs=[pl.BlockSpec((tm, d), lambda i: (i, 0)),
                  resident((d, fh)), resident((d, fh)), resident((fh, d)),
                  resident((1, d)), resident((1, d))],
        out_specs=pl.BlockSpec((tm, d), lambda i: (i, 0)),
        out_shape=jax.ShapeDtypeStruct((m, d), F32),
        compiler_params=_cparams(("parallel",)),
        name="ffn_ln2",
    )(x2, w_gate.astype(BF16), w_up.astype(BF16), w_down.astype(BF16), ln_w.reshape(1, d), ln_b.reshape(1, d))


def _layer(x, w_in, hy_conv_w, hy_conv_b, hy_filt_w1, hy_filt_b1, hy_filt_w2, hy_filt_b2,
           hy_filt_w3, hy_filt_b3, hy_filt_w4, hy_sin_freq, hy_skip, rw_mu, rw_w0, rw_w2,
           rw_a0, rw_a2, rw_g2, rw_k_k, rw_k_a, rw_r_k, rw_gn_w, rw_gn_b, w_hy_out, w_rw_out,
           w_o, ln1_w, ln1_b, ffn_w_gate, ffn_w_up, ffn_w_down, ln2_w, ln2_b):
    B, L, D = x.shape
    x2 = x.reshape(B * L, D)
    u3, u_r, gates = _in_proj(x2, L, w_in.astype(BF16), hy_conv_w, hy_conv_b)
    y_h = _hyena(u3, hy_filt_w1, hy_filt_b1, hy_filt_w2, hy_filt_b2, hy_filt_w3, hy_filt_b3, hy_filt_w4,
                 hy_sin_freq, hy_skip)
    p_ones = _head_ones()
    fwd_ops, bwd_ops, g, bonus, ncb = _rw_prep(
        u_r.reshape(B, L, RW_COLS), rw_mu, rw_w0, rw_w2, rw_a0, rw_a2, rw_g2, rw_k_k, rw_k_a, rw_r_k, p_ones)
    yf, yb = _rw_scan(fwd_ops, bwd_ops, ncb)
    flat = lambda a: a.reshape(B * L, a.shape[-1])
    h = _merge(x2, y_h, flat(yf), flat(yb), flat(bonus), flat(g), gates, p_ones, rw_gn_w, rw_gn_b,
               w_hy_out, w_rw_out, w_o, ln1_w, ln1_b)
    out = _ffn(h, ffn_w_gate, ffn_w_up, ffn_w_down, ln2_w, ln2_b)
    return out.reshape(B, L, D)


def kernel(x, w_in, hy_conv_w, hy_conv_b, hy_filt_w1, hy_filt_b1, hy_filt_w2, hy_filt_b2, hy_filt_w3, hy_filt_b3, hy_filt_w4, hy_sin_freq, hy_skip, rw_mu, rw_w0, rw_w2, rw_a0, rw_a2, rw_g2, rw_k_k, rw_k_a, rw_r_k, rw_gn_w, rw_gn_b, w_hy_out, w_rw_out, w_o, ln1_w, ln1_b, ffn_w_gate, ffn_w_up, ffn_w_down, ln2_w, ln2_b):
    params = (w_in, hy_conv_w, hy_conv_b, hy_filt_w1, hy_filt_b1, hy_filt_w2, hy_filt_b2, hy_filt_w3,
              hy_filt_b3, hy_filt_w4, hy_sin_freq, hy_skip, rw_mu, rw_w0, rw_w2, rw_a0, rw_a2, rw_g2,
              rw_k_k, rw_k_a, rw_r_k, rw_gn_w, rw_gn_b, w_hy_out, w_rw_out, w_o, ln1_w, ln1_b,
              ffn_w_gate, ffn_w_up, ffn_w_down, ln2_w, ln2_b)
    for l in range(w_in.shape[0]):
        x = _layer(x, *[p[l] for p in params])
    return x
```
